```python
import jax
import jax.numpy as jnp
from jax import lax
import numpy as np

D_MODEL = 2048
BATCH = 4
SEQ = 4096
DEPTH = 4
DEC_BATCH = 8
DEC_SEQ = 32
PAST_LEN = 1024

CHUNK = 64
Q_BLOCK = 128
MIX = D_MODEL
H_A = 8
D_HA = 64
W_A = H_A * D_HA
H_B = 8
NOPE = 128
ROPE = 64
V_HD = 128
Q_RANK = 512
KV_RANK = 256
W_B = H_B * V_HD
ROPE_THETA = 10000.0
C_C = MIX - W_A - W_B
CONV_W = 31
IN_SIZES = (W_A, W_A, W_A, H_A, Q_RANK, KV_RANK, ROPE, C_C, C_C)
IN_TOTAL = 3 * W_A + H_A + Q_RANK + KV_RANK + ROPE + 2 * C_C
N_GROUPS = 4
EXPERTS_PER_GROUP = 8
N_EXPERTS = N_GROUPS * EXPERTS_PER_GROUP
TOP_K = 2
D_EXPERT = 512
MOE_BLOCK = 64
EPS = 1e-6
NEG_INF = -1e30

kernel_name = 'hybrid_fox_mla_conformer_hmoe_stream_step'


def rms_norm(x, g):
    xf = x.astype(jnp.float32)
    y = xf * lax.rsqrt(jnp.mean(xf * xf, axis=-1, keepdims=True) + EPS)
    return (y * g.astype(jnp.float32)).astype(x.dtype)


def layer_norm(x, g, b):
    xf = x.astype(jnp.float32)
    mu = jnp.mean(xf, axis=-1, keepdims=True)
    var = jnp.mean(jnp.square(xf - mu), axis=-1, keepdims=True)
    y = (xf - mu) * lax.rsqrt(var + EPS) * g.astype(jnp.float32) + b.astype(jnp.float32)
    return y.astype(x.dtype)


def rope(x, pos):
    half = ROPE // 2
    inv = ROPE_THETA ** (-jnp.arange(half, dtype=jnp.float32) / half)
    ang = pos.astype(jnp.float32)[:, None] * inv[None, :]
    cos = jnp.cos(ang)[None, :, None, :]
    sin = jnp.sin(ang)[None, :, None, :]
    xf = x.astype(jnp.float32)
    x1, x2 = xf[..., :half], xf[..., half:]
    return jnp.concatenate([x1 * cos - x2 * sin, x2 * cos + x1 * sin], axis=-1).astype(x.dtype)


def split_columns(z, sizes):
    out, start = [], 0
    for s in sizes:
        out.append(z[..., start:start + s])
        start += s
    return out


def sweep_query_blocks(n_q, block_fn):
    if n_q % Q_BLOCK:
        return block_fn(0, n_q)
    out = lax.map(lambda i: block_fn(i * Q_BLOCK, Q_BLOCK), jnp.arange(n_q // Q_BLOCK))
    nb, b, qb, h, d = out.shape
    return jnp.moveaxis(out, 0, 1).reshape(b, nb * qb, h, d)


def fox_attention(q, k, v, c_q, c_k, q_pos, k_pos):
    scale = q.shape[-1] ** -0.5

    def block(q0, nq):
        qb = lax.dynamic_slice_in_dim(q, q0, nq, axis=1)
        cb = lax.dynamic_slice_in_dim(c_q, q0, nq, axis=2)
        pb = lax.dynamic_slice_in_dim(q_pos, q0, nq, axis=0)
        logits = jnp.einsum('bqhd,bkhd->bhqk', qb, k, preferred_element_type=jnp.float32) * scale
        logits = logits + cb[..., :, None] - c_k[..., None, :]
        logits = jnp.where(k_pos[None, :] <= pb[:, None], logits, NEG_INF)
        p = jax.nn.softmax(logits, axis=-1).astype(v.dtype)
        return jnp.einsum('bhqk,bkhd->bqhd', p, v)

    return sweep_query_blocks(q.shape[1], block)


def mla_attention(q_nope, q_rope, k_nope, k_rope, v, q_pos, k_pos):
    scale = (NOPE + ROPE) ** -0.5
    k_chunk = k_pos // CHUNK

    def block(q0, nq):
        qn = lax.dynamic_slice_in_dim(q_nope, q0, nq, axis=1)
        qr = lax.dynamic_slice_in_dim(q_rope, q0, nq, axis=1)
        pb = lax.dynamic_slice_in_dim(q_pos, q0, nq, axis=0)
        logits = (jnp.einsum('bqhd,bkhd->bhqk', qn, k_nope, preferred_element_type=jnp.float32)
                  + jnp.einsum('bqhr,bkr->bhqk', qr, k_rope, preferred_element_type=jnp.float32)) * scale
        logits = jnp.where(k_chunk[None, :] <= (pb // CHUNK)[:, None], logits, NEG_INF)
        p = jax.nn.softmax(logits, axis=-1).astype(v.dtype)
        return jnp.einsum('bhqk,bkhd->bqhd', p, v)

    return sweep_query_blocks(q_nope.shape[1], block)


def conformer_conv(u, buf, lp):
    xp = jnp.concatenate([buf.astype(u.dtype), u], axis=1)
    y = lax.conv_general_dilated(xp, lp['w_dw'][:, None, :], window_strides=(1,), padding='VALID',
                                 dimension_numbers=('NWC', 'WIO', 'NWC'), feature_group_count=C_C)
    y = layer_norm(y + lp['b_dw'], lp['ln_g'], lp['ln_b'])
    return jax.nn.silu(y), xp[:, -(CONV_W - 1):]


def token_mixers(h, past, lp):
    b, t, _ = h.shape
    qa, ka, va, fa, cq, ckv, kr, ua, ug = split_columns(h @ lp['w_in'], IN_SIZES)
    n_past = 0 if past is None else past[0].shape[1]
    q_pos = n_past + jnp.arange(t)
    k_pos = jnp.arange(n_past + t)
    qa = qa.reshape(b, t, H_A, D_HA)
    ka = ka.reshape(b, t, H_A, D_HA)
    va = va.reshape(b, t, H_A, D_HA)
    logf = jax.nn.log_sigmoid(fa.astype(jnp.float32) + lp['b_f'].astype(jnp.float32))
    q = (rms_norm(cq, lp['g_q']) @ lp['w_uq']).reshape(b, t, H_B, NOPE + ROPE)
    q_nope, q_rope = q[..., :NOPE], rope(q[..., NOPE:], q_pos)
    ckv = rms_norm(ckv, lp['g_kv'])
    k_rope = rope(kr[:, :, None, :], q_pos)[:, :, 0]
    u = ua * jax.nn.sigmoid(ug)
    if past is None:
        k_all, v_all, logf_all, ckv_all, kr_all = ka, va, logf, ckv, k_rope
        conv_buf = jnp.zeros((b, CONV_W - 1, C_C), u.dtype)
    else:
        pk, pv, plf, pckv, pkr, conv_buf = past
        k_all = jnp.concatenate([pk.astype(ka.dtype), ka], axis=1)
        v_all = jnp.concatenate([pv.astype(va.dtype), va], axis=1)
        logf_all = jnp.concatenate([plf.astype(jnp.float32), logf], axis=1)
        ckv_all = jnp.concatenate([pckv.astype(ckv.dtype), ckv], axis=1)
        kr_all = jnp.concatenate([pkr.astype(k_rope.dtype), k_rope], axis=1)
    c = jnp.cumsum(logf_all, axis=1).transpose(0, 2, 1)
    o_a = fox_attention(qa, k_all, v_all, c[..., n_past:], c, q_pos, k_pos)
    kv = (ckv_all @ lp['w_ukv']).reshape(b, n_past + t, H_B, NOPE + V_HD)
    o_b = mla_attention(q_nope, q_rope, kv[..., :NOPE], kr_all, kv[..., NOPE:], q_pos, k_pos)
    o_c, conv_state = conformer_conv(u, conv_buf, lp)
    g = lp['g_out']
    o = jnp.concatenate([rms_norm(o_a.reshape(b, t, W_A), g[:W_A]),
                         rms_norm(o_b.reshape(b, t, W_B), g[W_A:W_A + W_B]),
                         rms_norm(o_c, g[W_A + W_B:])], axis=-1)
    return o @ lp['w_out'], (ka, va, logf, ckv, k_rope, conv_state)


def routed_experts(xf, expert_ids, gate, w_gate, w_up, w_down):
    n, d = xf.shape
    a = n * TOP_K
    flat_e = expert_ids.reshape(-1)
    order = jnp.argsort(flat_e)
    e_sorted = flat_e[order]
    tok_sorted = (order // TOP_K).astype(jnp.int32)
    w_sorted = gate.reshape(-1)[order]
    counts = jnp.bincount(flat_e, length=N_EXPERTS)
    padded = (counts + MOE_BLOCK - 1) // MOE_BLOCK * MOE_BLOCK
    pad_end = jnp.cumsum(padded)
    pad_start = pad_end - padded
    start = jnp.cumsum(counts) - counts
    dest = pad_start[e_sorted] + jnp.arange(a) - start[e_sorted]
    n_blocks = -(-a // MOE_BLOCK) + N_EXPERTS
    rows = n_blocks * MOE_BLOCK
    slot_tok = jnp.full((rows,), n, jnp.int32).at[dest].set(tok_sorted)
    slot_w = jnp.zeros((rows,), gate.dtype).at[dest].set(w_sorted)
    block_e = jnp.minimum(jnp.searchsorted(pad_end, jnp.arange(n_blocks) * MOE_BLOCK, side='right'),
                          N_EXPERTS - 1)
    x_pad = jnp.concatenate([xf, jnp.zeros((1, d), xf.dtype)], axis=0)
    xs = x_pad[slot_tok].reshape(n_blocks, MOE_BLOCK, d)

    def expert_block(args):
        xb, e = args
        hid = jax.nn.silu(xb @ w_gate[e]) * (xb @ w_up[e])
        return hid @ w_down[e]

    ys = lax.map(expert_block, (xs, block_e)).reshape(rows, d)
    ys = ys * slot_w[:, None].astype(ys.dtype)
    return jax.ops.segment_sum(ys, slot_tok, num_segments=n + 1)[:n]


def hier_moe(x, lp):
    b, t, d = x.shape
    n = b * t
    xf = x.reshape(n, d)
    idx = jnp.arange(n)
    g_logits = (xf @ lp['w_rg']).astype(jnp.float32) + lp['b_rg'].astype(jnp.float32)
    g_sel = jnp.argmax(g_logits, axis=-1)
    g_w = jax.nn.softmax(g_logits, axis=-1)[idx, g_sel]
    e_logits = ((xf @ lp['w_re']).astype(jnp.float32).reshape(n, N_GROUPS, EXPERTS_PER_GROUP)
                + lp['b_re'].astype(jnp.float32))[idx, g_sel]
    top_v, top_i = lax.top_k(e_logits, TOP_K)
    gate = jax.nn.softmax(top_v, axis=-1) * g_w[:, None]
    expert_ids = (g_sel[:, None] * EXPERTS_PER_GROUP + top_i).astype(jnp.int32)
    y = routed_experts(xf, expert_ids, gate, lp['w_gate'], lp['w_up'], lp['w_down'])
    return y.reshape(b, t, d).astype(x.dtype)


def trunk_layer(x, past, lp):
    mix, state = token_mixers(rms_norm(x, lp['g_mix']), past, lp)
    x = x + mix
    x = x + hier_moe(rms_norm(x, lp['g_ffn']), lp)
    return x, state


def setup_inputs(seed: int = 0) -> dict:
    key = jax.random.key(seed)
    keys = iter(jax.random.split(key, 32))

    def nrm(shape, scale):
        return scale * jax.random.normal(next(keys), shape, jnp.float32)

    def gain(shape):
        return 1.0 + 0.02 * jax.random.normal(next(keys), shape, jnp.float32)

    L = DEPTH
    return {
        'x_prompt': nrm((BATCH, SEQ, D_MODEL), 1.0),
        'x_sample': nrm((DEC_BATCH, DEC_SEQ, D_MODEL), 1.0),
        'cache_fox_k': nrm((L, DEC_BATCH, PAST_LEN, H_A, D_HA), 1.0),
        'cache_fox_v': nrm((L, DEC_BATCH, PAST_LEN, H_A, D_HA), 1.0),
        'cache_fox_logf': jax.nn.log_sigmoid(2.0 + nrm((L, DEC_BATCH, PAST_LEN, H_A), 1.0)),
        'cache_mla_ckv': nrm((L, DEC_BATCH, PAST_LEN, KV_RANK), 1.0),
        'cache_mla_krope': nrm((L, DEC_BATCH, PAST_LEN, ROPE), 1.0),
        'state_conv': nrm((L, DEC_BATCH, CONV_W - 1, C_C), 0.5),
        'g_mix': gain((L, D_MODEL)),
        'w_in': nrm((L, D_MODEL, IN_TOTAL), D_MODEL ** -0.5),
        'b_f': 2.0 + nrm((L, H_A), 0.5),
        'g_q': gain((L, Q_RANK)),
        'w_uq': nrm((L, Q_RANK, H_B * (NOPE + ROPE)), Q_RANK ** -0.5),
        'g_kv': gain((L, KV_RANK)),
        'w_ukv': nrm((L, KV_RANK, H_B * (NOPE + V_HD)), KV_RANK ** -0.5),
        'w_dw': nrm((L, CONV_W, C_C), CONV_W ** -0.5),
        'b_dw': nrm((L, C_C), 0.02),
        'ln_g': gain((L, C_C)),
        'ln_b': nrm((L, C_C), 0.02),
        'g_out': gain((L, MIX)),
        'w_out': nrm((L, MIX, D_MODEL), MIX ** -0.5),
        'g_ffn': gain((L, D_MODEL)),
        'w_rg': nrm((L, D_MODEL, N_GROUPS), D_MODEL ** -0.5),
        'b_rg': nrm((L, N_GROUPS), 0.01),
        'w_re': nrm((L, D_MODEL, N_EXPERTS), D_MODEL ** -0.5),
        'b_re': nrm((L, N_GROUPS, EXPERTS_PER_GROUP), 0.01),
        'w_gate': nrm((L, N_EXPERTS, D_MODEL, D_EXPERT), D_MODEL ** -0.5),
        'w_up': nrm((L, N_EXPERTS, D_MODEL, D_EXPERT), D_MODEL ** -0.5),
        'w_down': nrm((L, N_EXPERTS, D_EXPERT, D_MODEL), D_EXPERT ** -0.5),
        'g_final': gain((D_MODEL,)),
    }


def reference(x_prompt, x_sample, cache_fox_k, cache_fox_v, cache_fox_logf, cache_mla_ckv,
              cache_mla_krope, state_conv, g_mix, w_in, b_f, g_q, w_uq, g_kv, w_ukv, w_dw, b_dw,
              ln_g, ln_b, g_out, w_out, g_ffn, w_rg, b_rg, w_re, b_re, w_gate, w_up, w_down, g_final):
    xp, xs = x_prompt, x_sample
    p_states, s_states = [], []
    for l in range(DEPTH):
        lp = dict(g_mix=g_mix[l], w_in=w_in[l], b_f=b_f[l], g_q=g_q[l], w_uq=w_uq[l], g_kv=g_kv[l],
                  w_ukv=w_ukv[l], w_dw=w_dw[l], b_dw=b_dw[l], ln_g=ln_g[l], ln_b=ln_b[l],
                  g_out=g_out[l], w_out=w_out[l], g_ffn=g_ffn[l], w_rg=w_rg[l], b_rg=b_rg[l],
                  w_re=w_re[l], b_re=b_re[l], w_gate=w_gate[l], w_up=w_up[l], w_down=w_down[l])
        past = (cache_fox_k[l], cache_fox_v[l], cache_fox_logf[l], cache_mla_ckv[l],
                cache_mla_krope[l], state_conv[l])
        xp, st_p = trunk_layer(xp, None, lp)
        p_states.append(st_p)
        xs, st_s = trunk_layer(xs, past, lp)
        s_states.append(st_s)
    y_prompt = rms_norm(xp, g_final)
    y_sample = rms_norm(xs, g_final)
    p_fox_k, p_fox_v, p_fox_logf, p_mla_ckv, p_mla_krope, p_conv = [jnp.stack(a) for a in zip(*p_states)]
    s_fox_k, s_fox_v, s_fox_logf, s_mla_ckv, s_mla_krope, s_conv = [jnp.stack(a) for a in zip(*s_states)]
    return (y_prompt, y_sample, p_fox_k, p_fox_v, p_fox_logf, p_mla_ckv, p_mla_krope, p_conv,
            s_fox_k, s_fox_v, s_fox_logf, s_mla_ckv, s_mla_krope, s_conv)
```

```python
import functools

import numpy as np
import jax
import jax.numpy as jnp
from jax import lax
from jax.experimental import pallas as pl
from jax.experimental.pallas import tpu as pltpu

F32 = jnp.float32
BF16 = jnp.bfloat16

D_MODEL = 2048
DEPTH = 4
CHUNK = 64
H_A, D_HA = 8, 64
W_A = H_A * D_HA
H_B, NOPE, ROPE, V_HD = 8, 128, 64, 128
Q_RANK, KV_RANK = 512, 256
W_B = H_B * V_HD
C_C = D_MODEL - W_A - W_B
CONV_W = 31
_CHUNK_SHIFT = CHUNK.bit_length() - 1
assert 1 << _CHUNK_SHIFT == CHUNK
ROPE_THETA = 10000.0
N_GROUPS, EXPERTS_PER_GROUP = 4, 8
N_EXPERTS = N_GROUPS * EXPERTS_PER_GROUP
TOP_K = 2
D_EXPERT = 512
EPS = 1e-6
NEG_INF = -1e30

LANES = 128
SUBLANES = 8
TQ_PROMPT = 256
CONV_TILE = 256
MLA_SLAB = NOPE + LANES
ROW_TILE = 256
EXPERT_TILE = 256
VMEM_LIMIT = 56 * 1024 * 1024

_C_QA, _C_KA, _C_VA, _C_CQ, _C_CKV, _C_UA, _C_UG, _C_KRF = 0, 512, 1024, 1536, 2048, 2304, 2816, 3328
IN_COLS = _C_KRF + LANES
_FA_LANE = ROPE


def _cparams(*sem):
    return pltpu.CompilerParams(dimension_semantics=sem, vmem_limit_bytes=VMEM_LIMIT)


def _const_spec(shape):
    nd = len(shape)
    return pl.BlockSpec(shape, lambda *_: (0,) * nd)


def _rms(x, g):
    return x * lax.rsqrt(jnp.mean(x * x, axis=-1, keepdims=True) + EPS) * g


def _rope_block(x, cos, sin_signed):
    lane = lax.broadcasted_iota(jnp.int32, x.shape, 1)
    partner = jnp.where(lane < ROPE // 2, pltpu.roll(x, LANES - ROPE // 2, 1), pltpu.roll(x, ROPE // 2, 1))
    return x * cos + partner * sin_signed


def _in_proj_kernel(x_ref, gmix_ref, win_ref, gq_ref, gkv_ref, wuq_ref, bf_ref, cos_ref, sin_ref,
                    qa_ref, ka_ref, va_ref, kab_ref, vab_ref, logf_ref, qmla_ref, ckv_ref, kr_ref, u_ref):
    h = _rms(x_ref[...], gmix_ref[...]).astype(BF16)

    def proj(c0, width):
        return jnp.dot(h, win_ref[:, c0:c0 + width], preferred_element_type=F32)

    qa_ref[...] = proj(_C_QA, W_A).astype(BF16)
    ka = proj(_C_KA, W_A)
    ka_ref[...] = ka
    kab_ref[...] = ka.astype(BF16)
    va = proj(_C_VA, W_A)
    va_ref[...] = va
    vab_ref[...] = va.astype(BF16)

    cos = cos_ref[...]
    sin = sin_ref[...]
    krf = proj(_C_KRF, LANES)
    kr_ref[...] = _rope_block(krf, cos, sin)
    z = krf + bf_ref[...]
    logsig = jnp.minimum(z, 0.0) - jnp.log(1.0 + jnp.exp(-jnp.abs(z)))
    logf_ref[...] = logsig[:, _FA_LANE:_FA_LANE + H_A]

    ckv_ref[...] = _rms(proj(_C_CKV, KV_RANK), gkv_ref[...])

    u_ref[...] = proj(_C_UA, C_C) * jax.nn.sigmoid(proj(_C_UG, C_C))

    cqn = _rms(proj(_C_CQ, Q_RANK), gq_ref[...]).astype(BF16)
    scale = (NOPE + ROPE) ** -0.5
    for hd in range(H_B):
        q = jnp.dot(cqn, wuq_ref[:, hd * MLA_SLAB:(hd + 1) * MLA_SLAB], preferred_element_type=F32)
        qmla_ref[:, hd * MLA_SLAB:hd * MLA_SLAB + NOPE] = (q[:, :NOPE] * scale).astype(BF16)
        qmla_ref[:, hd * MLA_SLAB + NOPE:(hd + 1) * MLA_SLAB] = (
            _rope_block(q[:, NOPE:], cos, sin) * scale).astype(BF16)


def _in_proj(x, g_mix, w_in_p, g_q, g_kv, w_uq_p, b_f128, cos, sin):
    n = x.shape[0]
    tm = ROW_TILE
    row = lambda w: pl.BlockSpec((tm, w), lambda i: (i, 0))
    out_shapes = (
        jax.ShapeDtypeStruct((n, W_A), BF16),
        jax.ShapeDtypeStruct((n, W_A), F32),
        jax.ShapeDtypeStruct((n, W_A), F32),
        jax.ShapeDtypeStruct((n, W_A), BF16),
        jax.ShapeDtypeStruct((n, W_A), BF16),
        jax.ShapeDtypeStruct((n, H_A), F32),
        jax.ShapeDtypeStruct((n, H_B * MLA_SLAB), BF16),
        jax.ShapeDtypeStruct((n, KV_RANK), F32),
        jax.ShapeDtypeStruct((n, LANES), F32),
        jax.ShapeDtypeStruct((n, C_C), F32),
    )
    return pl.pallas_call(
        _in_proj_kernel,
        grid=(n // tm,),
        in_specs=[row(D_MODEL), _const_spec((1, D_MODEL)), _const_spec((D_MODEL, IN_COLS)),
                  _const_spec((1, Q_RANK)), _const_spec((1, KV_RANK)),
                  _const_spec((Q_RANK, H_B * MLA_SLAB)), _const_spec((1, LANES)), row(LANES), row(LANES)],
        out_specs=[row(W_A), row(W_A), row(W_A), row(W_A), row(W_A), row(H_A), row(H_B * MLA_SLAB),
                   row(KV_RANK), row(LANES), row(C_C)],
        out_shape=out_shapes,
        compiler_params=_cparams("parallel"),
        name="in_proj",
    )(x, g_mix, w_in_p, g_q, g_kv, w_uq_p, b_f128, cos, sin)


def _cumsum_kernel(x_ref, o_ref):
    c = x_ref[...]
    t = c.shape[-1]
    lane = lax.broadcasted_iota(jnp.int32, c.shape, 1)
    s = 1
    while s < t:
        c = c + jnp.where(lane >= s, pltpu.roll(c, s, 1), 0.0)
        s *= 2
    o_ref[...] = c


def _cumsum_time(logf_t):
    b, hh, t = logf_t.shape
    spec = pl.BlockSpec((None, hh, t), lambda i: (i, 0, 0))
    return pl.pallas_call(
        _cumsum_kernel, grid=(b,), in_specs=[spec], out_specs=spec,
        out_shape=jax.ShapeDtypeStruct((b, hh, t), F32),
        compiler_params=_cparams("parallel"), name="cumsum_logf",
    )(logf_t)


def _softmax_step(s, v, m_ref, l_ref, acc_ref):
    m_old = m_ref[...]
    m_new = jnp.maximum(m_old, jnp.max(s, axis=-1, keepdims=True))
    alpha = jnp.exp(m_old - m_new)
    p = jnp.exp(s - m_new)
    l_ref[...] = alpha * l_ref[...] + jnp.sum(p, axis=-1, keepdims=True)
    acc_ref[...] = alpha * acc_ref[...] + jnp.dot(p.astype(BF16), v, preferred_element_type=F32)
    m_ref[...] = m_new


def _qk(q, k):
    return lax.dot_general(q, k, (((1,), (1,)), ((), ())), preferred_element_type=F32)


def _tile_extent(n_past, tq, tk, single_tile):
    if single_tile:
        return n_past // tk, n_past
    start = n_past + pl.program_id(2) * tq
    return start // tk, pl.multiple_of(start, tq)


def _fox_kernel(q_ref, k_ref, v_ref, cq_ref, ck_ref, o_ref, m_ref, l_ref, acc_ref, *, tq, tk, n_past, single_tile):
    q = q_ref[...]
    lane = lax.broadcasted_iota(jnp.int32, q.shape, 1)
    n_open, d0 = _tile_extent(n_past, tq, tk, single_tile)
    row = lax.broadcasted_iota(jnp.int32, (tq, tq), 0)
    col = lax.broadcasted_iota(jnp.int32, (tq, tq), 1)
    outs = []
    for j in range(2):
        qj = jnp.where((lane >= j * D_HA) & (lane < (j + 1) * D_HA), q, jnp.zeros_like(q))
        cq = cq_ref[:, j:j + 1]
        m_ref[...] = jnp.full(m_ref.shape, NEG_INF, F32)
        l_ref[...] = jnp.zeros(l_ref.shape, F32)
        acc_ref[...] = jnp.zeros(acc_ref.shape, F32)

        def open_chunk(i, carry):
            k0 = pl.multiple_of(i * tk, tk)
            s = _qk(qj, k_ref[pl.ds(k0, tk), :]) + (cq - ck_ref[j:j + 1, pl.ds(k0, tk)])
            _softmax_step(s, v_ref[pl.ds(k0, tk), :], m_ref, l_ref, acc_ref)
            return carry

        lax.fori_loop(0, n_open, open_chunk, 0)
        s = _qk(qj, k_ref[pl.ds(d0, tq), :]) + (cq - ck_ref[j:j + 1, pl.ds(d0, tq)])
        s = jnp.where(col <= row, s, NEG_INF)
        _softmax_step(s, v_ref[pl.ds(d0, tq), :], m_ref, l_ref, acc_ref)
        outs.append(acc_ref[...] / l_ref[...])
    o_ref[...] = jnp.where(lane < D_HA, outs[0], outs[1])


def _fox_attention(q, k, v, c_q, c_k, *, n_past, tq, tk):
    b, t_q, _ = q.shape
    t_k = k.shape[1]
    t_kp = c_k.shape[-1]
    assert t_q % tq == 0 and n_past % tk == 0 and (t_q == tq or tq % tk == 0)
    kern = functools.partial(_fox_kernel, tq=tq, tk=tk, n_past=n_past, single_tile=t_q == tq)
    return pl.pallas_call(
        kern,
        grid=(b, H_A // 2, t_q // tq),
        in_specs=[pl.BlockSpec((None, tq, LANES), lambda bi, hp, qi: (bi, qi, hp)),
                  pl.BlockSpec((None, t_k, LANES), lambda bi, hp, qi: (bi, 0, hp)),
                  pl.BlockSpec((None, t_k, LANES), lambda bi, hp, qi: (bi, 0, hp)),
                  pl.BlockSpec((None, None, tq, 2), lambda bi, hp, qi: (bi, hp, qi, 0)),
                  pl.BlockSpec((None, None, 2, t_kp), lambda bi, hp, qi: (bi, hp, 0, 0))],
        out_specs=pl.BlockSpec((None, tq, LANES), lambda bi, hp, qi: (bi, qi, hp)),
        out_shape=jax.ShapeDtypeStruct((b, t_q, W_A), F32),
        scratch_shapes=[pltpu.VMEM((tq, 1), F32), pltpu.VMEM((tq, 1), F32), pltpu.VMEM((tq, LANES), F32)],
        compiler_params=_cparams("parallel", "parallel", "arbitrary"),
        name="fox_attention",
    )(q, k, v, c_q, c_k)


def _mla_kernel(q_ref, k_ref, v_ref, o_ref, m_ref, l_ref, acc_ref, *, tq, tk, n_past, single_tile):
    q = q_ref[...]
    n_open, d0 = _tile_extent(n_past, tq, tk, single_tile)
    row = lax.broadcasted_iota(jnp.int32, (tq, tq), 0)
    col = lax.broadcasted_iota(jnp.int32, (tq, tq), 1)
    m_ref[...] = jnp.full(m_ref.shape, NEG_INF, F32)
    l_ref[...] = jnp.zeros(l_ref.shape, F32)
    acc_ref[...] = jnp.zeros(acc_ref.shape, F32)

    def open_chunk(i, carry):
        k0 = pl.multiple_of(i * tk, tk)
        _softmax_step(_qk(q, k_ref[pl.ds(k0, tk), :]), v_ref[pl.ds(k0, tk), :], m_ref, l_ref, acc_ref)
        return carry

    lax.fori_loop(0, n_open, open_chunk, 0)
    s = _qk(q, k_ref[pl.ds(d0, tq), :])
    s = jnp.where((d0 + col) >> _CHUNK_SHIFT <= (d0 + row) >> _CHUNK_SHIFT, s, NEG_INF)
    _softmax_step(s, v_ref[pl.ds(d0, tq), :], m_ref, l_ref, acc_ref)
    o_ref[...] = acc_ref[...] / l_ref[...]


def _mla_attention(q, k, v, *, n_past, tq, tk):
    b, t_q, _ = q.shape
    t_k = k.shape[1]
    assert t_q % tq == 0 and n_past % tk == 0 and (t_q == tq or tq % tk == 0)
    kern = functools.partial(_mla_kernel, tq=tq, tk=tk, n_past=n_past, single_tile=t_q == tq)
    return pl.pallas_call(
        kern,
        grid=(b, H_B, t_q // tq),
        in_specs=[pl.BlockSpec((None, tq, MLA_SLAB), lambda bi, hd, qi: (bi, qi, hd)),
                  pl.BlockSpec((None, t_k, MLA_SLAB), lambda bi, hd, qi: (bi, 0, hd)),
                  pl.BlockSpec((None, t_k, V_HD), lambda bi, hd, qi: (bi, 0, hd))],
        out_specs=pl.BlockSpec((None, tq, V_HD), lambda bi, hd, qi: (bi, qi, hd)),
        out_shape=jax.ShapeDtypeStruct((b, t_q, W_B), F32),
        scratch_shapes=[pltpu.VMEM((tq, 1), F32), pltpu.VMEM((tq, 1), F32), pltpu.VMEM((tq, V_HD), F32)],
        compiler_params=_cparams("parallel", "parallel", "arbitrary"),
        name="mla_attention",
    )(q, k, v)


def _kv_up_kernel(ckv_ref, kr_ref, wuk_ref, wuv_ref, k_ref, v_ref):
    c = ckv_ref[...].astype(BF16)
    kr = kr_ref[...].astype(BF16)
    kn = jnp.dot(c, wuk_ref[...], preferred_element_type=F32).astype(BF16)
    for hd in range(H_B):
        k_ref[:, hd * MLA_SLAB:hd * MLA_SLAB + NOPE] = kn[:, hd * NOPE:(hd + 1) * NOPE]
        k_ref[:, hd * MLA_SLAB + NOPE:(hd + 1) * MLA_SLAB] = kr
    v_ref[...] = jnp.dot(c, wuv_ref[...], preferred_element_type=F32).astype(BF16)


def _kv_up(ckv_n, krope128, w_uk, w_uv):
    r = ckv_n.shape[0]
    tm = ROW_TILE
    row = lambda w: pl.BlockSpec((tm, w), lambda i: (i, 0))
    return pl.pallas_call(
        _kv_up_kernel, grid=(r // tm,),
        in_specs=[row(KV_RANK), row(LANES), _const_spec((KV_RANK, H_B * NOPE)),
                  _const_spec((KV_RANK, H_B * V_HD))],
        out_specs=[row(H_B * MLA_SLAB), row(H_B * V_HD)],
        out_shape=(jax.ShapeDtypeStruct((r, H_B * MLA_SLAB), BF16), jax.ShapeDtypeStruct((r, H_B * V_HD), BF16)),
        compiler_params=_cparams("parallel"), name="mla_kv_up",
    )(ckv_n, krope128, w_uk, w_uv)


_CONV_SUB = 64
_CONV_HALO = -(-CONV_W // SUBLANES) * SUBLANES


def _conv_kernel(xp_ref, w_ref, b_ref, g_ref, beta_ref, o_ref, *, tt):
    t0 = pl.program_id(1) * tt
    w = w_ref[...]
    rows = min(_CONV_SUB, tt)
    for sub in range(tt // rows):
        base = pl.multiple_of(t0 + sub * rows, rows)
        xa = xp_ref[pl.ds(base, rows + _CONV_HALO), :]
        acc = None
        for b in range(SUBLANES):
            z = None
            for a in range(-(-CONV_W // SUBLANES)):
                tap = SUBLANES * a + b
                if tap < CONV_W:
                    term = xa[SUBLANES * a:SUBLANES * a + rows + SUBLANES, :] * w[tap:tap + 1, :]
                    z = term if z is None else z + term
            acc = z[b:b + rows, :] if acc is None else acc + z[b:b + rows, :]
        y = acc + b_ref[...]
        mu = jnp.mean(y, axis=-1, keepdims=True)
        yc = y - mu
        var = jnp.mean(yc * yc, axis=-1, keepdims=True)
        y = yc * lax.rsqrt(var + EPS) * g_ref[...] + beta_ref[...]
        o_ref[sub * rows:(sub + 1) * rows, :] = y * jax.nn.sigmoid(y)


def _conv_module(xp, w_dw, b_dw, ln_g, ln_b, *, t, tt):
    b, t_p, _ = xp.shape
    kern = functools.partial(_conv_kernel, tt=tt)
    return pl.pallas_call(
        kern, grid=(b, t // tt),
        in_specs=[pl.BlockSpec((None, t_p, C_C), lambda bi, ti: (bi, 0, 0)),
                  _const_spec((CONV_W, C_C)), _const_spec((1, C_C)), _const_spec((1, C_C)), _const_spec((1, C_C))],
        out_specs=pl.BlockSpec((None, tt, C_C), lambda bi, ti: (bi, ti, 0)),
        out_shape=jax.ShapeDtypeStruct((b, t, C_C), F32),
        compiler_params=_cparams("parallel", "arbitrary"), name="conv_module",
    )(xp, w_dw, b_dw, ln_g, ln_b)


def _out_proj_kernel(oa_ref, ob_ref, oc_ref, x_ref, gout_ref, wout_ref, gffn_ref, wr_ref, br_ref,
                     xn_ref, h2_ref, rl_ref):
    g = gout_ref[...]
    na = _rms(oa_ref[...], g[:, :W_A]).astype(BF16)
    nb = _rms(ob_ref[...], g[:, W_A:W_A + W_B]).astype(BF16)
    nc = _rms(oc_ref[...], g[:, W_A + W_B:]).astype(BF16)
    mix = jnp.dot(na, wout_ref[:W_A, :], preferred_element_type=F32)
    mix = mix + jnp.dot(nb, wout_ref[W_A:W_A + W_B, :], preferred_element_type=F32)
    mix = mix + jnp.dot(nc, wout_ref[W_A + W_B:, :], preferred_element_type=F32)
    xn = x_ref[...] + mix
    xn_ref[...] = xn
    h2 = _rms(xn, gffn_ref[...]).astype(BF16)
    h2_ref[...] = h2
    rl_ref[...] = jnp.dot(h2, wr_ref[...], preferred_element_type=F32) + br_ref[...]


def _out_proj(o_a, o_b, o_c, x, g_out, w_out, g_ffn, w_r, b_r):
    n = x.shape[0]
    tm = ROW_TILE
    row = lambda w: pl.BlockSpec((tm, w), lambda i: (i, 0))
    return pl.pallas_call(
        _out_proj_kernel, grid=(n // tm,),
        in_specs=[row(W_A), row(W_B), row(C_C), row(D_MODEL), _const_spec((1, D_MODEL)),
                  _const_spec((D_MODEL, D_MODEL)), _const_spec((1, D_MODEL)),
                  _const_spec((D_MODEL, LANES)), _const_spec((1, LANES))],
        out_specs=[row(D_MODEL), row(D_MODEL), row(LANES)],
        out_shape=(jax.ShapeDtypeStruct((n, D_MODEL), F32), jax.ShapeDtypeStruct((n, D_MODEL), BF16),
                   jax.ShapeDtypeStruct((n, LANES), F32)),
        compiler_params=_cparams("parallel"), name="out_proj_router",
    )(o_a, o_b, o_c, x, g_out, w_out, g_ffn, w_r, b_r)


def _expert_kernel(be_ref, nv_ref, xs_ref, wg_ref, wu_ref, wd_ref, sw_ref, y_ref, wgb_ref, wub_ref, wdb_ref):
    i = pl.program_id(0)
    prev = be_ref[jnp.maximum(i - 1, 0)]

    @pl.when((i == 0) | (be_ref[i] != prev))
    def _():
        wgb_ref[...] = wg_ref[...].astype(BF16)
        wub_ref[...] = wu_ref[...].astype(BF16)
        wdb_ref[...] = wd_ref[...].astype(BF16)

    @pl.when(i < nv_ref[0])
    def _():
        x = xs_ref[...]
        gate = jnp.dot(x, wgb_ref[...], preferred_element_type=F32)
        up = jnp.dot(x, wub_ref[...], preferred_element_type=F32)
        hid = (gate * jax.nn.sigmoid(gate) * up).astype(BF16)
        y_ref[...] = jnp.dot(hid, wdb_ref[...], preferred_element_type=F32) * sw_ref[...]

    @pl.when(i >= nv_ref[0])
    def _():
        y_ref[...] = jnp.zeros(y_ref.shape, F32)


def _expert_blocks(block_e, n_valid, xs, w_gate, w_up, w_down, slot_w):
    rows = xs.shape[0]
    te = EXPERT_TILE
    grid_spec = pltpu.PrefetchScalarGridSpec(
        num_scalar_prefetch=2, grid=(rows // te,),
        in_specs=[pl.BlockSpec((te, D_MODEL), lambda i, be, nv: (i, 0)),
                  pl.BlockSpec((None, D_MODEL, D_EXPERT), lambda i, be, nv: (be[i], 0, 0)),
                  pl.BlockSpec((None, D_MODEL, D_EXPERT), lambda i, be, nv: (be[i], 0, 0)),
                  pl.BlockSpec((None, D_EXPERT, D_MODEL), lambda i, be, nv: (be[i], 0, 0)),
                  pl.BlockSpec((te, 1), lambda i, be, nv: (i, 0))],
        out_specs=pl.BlockSpec((te, D_MODEL), lambda i, be, nv: (i, 0)),
        scratch_shapes=[pltpu.VMEM((D_MODEL, D_EXPERT), BF16), pltpu.VMEM((D_MODEL, D_EXPERT), BF16),
                        pltpu.VMEM((D_EXPERT, D_MODEL), BF16)])
    return pl.pallas_call(
        _expert_kernel, grid_spec=grid_spec,
        out_shape=jax.ShapeDtypeStruct((rows, D_MODEL), F32),
        compiler_params=_cparams("arbitrary"), name="expert_blocks",
    )(block_e, n_valid, xs, w_gate, w_up, w_down, slot_w)


def _hier_moe(h2, rlog, w_gate, w_up, w_down):
    n = h2.shape[0]
    idx = jnp.arange(n)
    g_logits = rlog[:, :N_GROUPS]
    g_sel = jnp.argmax(g_logits, axis=-1)
    g_w = jax.nn.softmax(g_logits, axis=-1)[idx, g_sel]
    e_logits = rlog[:, N_GROUPS:N_GROUPS + N_EXPERTS].reshape(n, N_GROUPS, EXPERTS_PER_GROUP)[idx, g_sel]
    top_v, top_i = lax.top_k(e_logits, TOP_K)
    gate = jax.nn.softmax(top_v, axis=-1) * g_w[:, None]
    expert_ids = (g_sel[:, None] * EXPERTS_PER_GROUP + top_i).astype(jnp.int32)

    te = EXPERT_TILE
    a = n * TOP_K
    flat_e = expert_ids.reshape(-1)
    order = jnp.argsort(flat_e)
    e_sorted = flat_e[order]
    counts = jnp.bincount(flat_e, length=N_EXPERTS)
    padded = (counts + te - 1) // te * te
    pad_end = jnp.cumsum(padded)
    pad_start = pad_end - padded
    start = jnp.cumsum(counts) - counts
    dest_sorted = (pad_start[e_sorted] + jnp.arange(a) - start[e_sorted]).astype(jnp.int32)
    n_blocks = -(-a // te) + N_EXPERTS
    rows = n_blocks * te
    slot_tok = jnp.full((rows,), n, jnp.int32).at[dest_sorted].set((order // TOP_K).astype(jnp.int32))
    slot_w = jnp.zeros((rows,), F32).at[dest_sorted].set(gate.reshape(-1)[order])
    block_e = jnp.minimum(jnp.searchsorted(pad_end, jnp.arange(n_blocks) * te, side='right'),
                          N_EXPERTS - 1).astype(jnp.int32)
    n_valid = (pad_end[-1] // te).astype(jnp.int32).reshape(1)
    x_pad = jnp.concatenate([h2, jnp.zeros((1, D_MODEL), h2.dtype)], axis=0)
    xs = x_pad[slot_tok]
    ys = _expert_blocks(block_e, n_valid, xs, w_gate, w_up, w_down, slot_w[:, None])
    pos = jnp.zeros((a,), jnp.int32).at[order].set(dest_sorted).reshape(n, TOP_K)
    return ys[pos[:, 0]] + ys[pos[:, 1]]


def _final_norm_kernel(x_ref, g_ref, o_ref):
    o_ref[...] = _rms(x_ref[...], g_ref[...])


def _final_norm(x, g):
    n = x.shape[0]
    row = pl.BlockSpec((ROW_TILE, D_MODEL), lambda i: (i, 0))
    return pl.pallas_call(
        _final_norm_kernel, grid=(n // ROW_TILE,), in_specs=[row, _const_spec((1, D_MODEL))], out_specs=row,
        out_shape=jax.ShapeDtypeStruct((n, D_MODEL), F32), compiler_params=_cparams("parallel"),
        name="final_norm",
    )(x, g)


def _rope_tables(pos):
    half = ROPE // 2
    inv = ROPE_THETA ** (-jnp.arange(half, dtype=F32) / half)
    ang = pos.astype(F32)[:, None] * inv[None, :]
    zeros = jnp.zeros((pos.shape[0], LANES - ROPE), F32)
    cos = jnp.concatenate([jnp.cos(ang), jnp.cos(ang), zeros], axis=-1)
    sin = jnp.concatenate([-jnp.sin(ang), jnp.sin(ang), zeros], axis=-1)
    return cos, sin


def _prep_w_in(w_in):
    qa, ka, va, fa, cq, ckv, kr, ua, ug = jnp.split(
        w_in, np.cumsum([W_A, W_A, W_A, H_A, Q_RANK, KV_RANK, ROPE, C_C])[:].tolist(), axis=1)
    pad = jnp.zeros((D_MODEL, LANES - ROPE - H_A), w_in.dtype)
    return jnp.concatenate([qa * (D_HA ** -0.5), ka, va, cq, ckv, ua, ug, kr, fa, pad], axis=1).astype(BF16)


def _prep_w_uq(w_uq):
    w = w_uq.reshape(Q_RANK, H_B, NOPE + ROPE)
    w = jnp.pad(w, ((0, 0), (0, 0), (0, MLA_SLAB - NOPE - ROPE)))
    return w.reshape(Q_RANK, H_B * MLA_SLAB).astype(BF16)


def _prep_w_ukv(w_ukv):
    w = w_ukv.reshape(KV_RANK, H_B, NOPE + V_HD)
    return (w[:, :, :NOPE].reshape(KV_RANK, H_B * NOPE).astype(BF16),
            w[:, :, NOPE:].reshape(KV_RANK, H_B * V_HD).astype(BF16))


def _pad_lanes(x, width):
    return jnp.pad(x, ((0, 0),) * (x.ndim - 1) + ((0, width - x.shape[-1]),))


def _round_up(x, m):
    return -(-x // m) * m


def kernel(x_prompt, x_sample, cache_fox_k, cache_fox_v, cache_fox_logf, cache_mla_ckv, cache_mla_krope, state_conv, g_mix, w_in, b_f, g_q, w_uq, g_kv, w_ukv, w_dw, b_dw, ln_g, ln_b, g_out, w_out, g_ffn, w_rg, b_rg, w_re, b_re, w_gate, w_up, w_down, g_final):
    bp, tp, _ = x_prompt.shape
    bs, ts, _ = x_sample.shape
    n_past = cache_fox_k.shape[2]
    n_p, n_s = bp * tp, bs * ts
    depth = g_mix.shape[0]

    x = jnp.concatenate([x_prompt.reshape(n_p, D_MODEL), x_sample.reshape(n_s, D_MODEL)], axis=0)
    cos_p, sin_p = _rope_tables(jnp.arange(tp))
    cos_s, sin_s = _rope_tables(n_past + jnp.arange(ts))
    cos = jnp.concatenate([jnp.tile(cos_p, (bp, 1)), jnp.tile(cos_s, (bs, 1))], axis=0)
    sin = jnp.concatenate([jnp.tile(sin_p, (bp, 1)), jnp.tile(sin_s, (bs, 1))], axis=0)

    tq_p = TQ_PROMPT
    conv_tt = CONV_TILE
    tkp_s = _round_up(n_past + ts, LANES)
    states_p, states_s = [], []
    for l in range(depth):
        w_in_p = _prep_w_in(w_in[l])
        w_uq_p = _prep_w_uq(w_uq[l])
        w_uk, w_uv = _prep_w_ukv(w_ukv[l])
        b_f128 = jnp.pad(b_f[l], (_FA_LANE, LANES - _FA_LANE - H_A))[None, :]
        qa, ka, va, ka_b, va_b, logf, q_mla, ckv_n, kr128, u = _in_proj(
            x, g_mix[l][None], w_in_p, g_q[l][None], g_kv[l][None], w_uq_p, b_f128, cos, sin)

        lf_p = logf[:n_p].reshape(bp, tp, H_A)
        lf_s = logf[n_p:].reshape(bs, ts, H_A)
        c_p = _cumsum_time(lf_p.transpose(0, 2, 1))
        lf_all = jnp.concatenate([cache_fox_logf[l].astype(F32), lf_s], axis=1)
        c_s = _cumsum_time(_pad_lanes(lf_all.transpose(0, 2, 1), tkp_s))

        def split_heads(c):
            return c.reshape(c.shape[0], H_A // 2, 2, c.shape[-1])

        o_a_p = _fox_attention(
            qa[:n_p].reshape(bp, tp, W_A), ka_b[:n_p].reshape(bp, tp, W_A), va_b[:n_p].reshape(bp, tp, W_A),
            split_heads(c_p).transpose(0, 1, 3, 2), split_heads(c_p), n_past=0, tq=tq_p, tk=tq_p)
        k_all = jnp.concatenate([cache_fox_k[l].reshape(bs, n_past, W_A).astype(BF16),
                                 ka_b[n_p:].reshape(bs, ts, W_A)], axis=1)
        v_all = jnp.concatenate([cache_fox_v[l].reshape(bs, n_past, W_A).astype(BF16),
                                 va_b[n_p:].reshape(bs, ts, W_A)], axis=1)
        o_a_s = _fox_attention(
            qa[n_p:].reshape(bs, ts, W_A), k_all, v_all,
            split_heads(c_s[:, :, n_past:n_past + ts]).transpose(0, 1, 3, 2), split_heads(c_s),
            n_past=n_past, tq=ts, tk=512)

        ckv_s = ckv_n[n_p:].reshape(bs, ts, KV_RANK)
        kr_s = kr128[n_p:].reshape(bs, ts, LANES)
        ckv_all = jnp.concatenate([cache_mla_ckv[l].astype(F32), ckv_s], axis=1)
        kr_all = jnp.concatenate([_pad_lanes(cache_mla_krope[l].astype(F32), LANES), kr_s], axis=1)
        n_ks = bs * (n_past + ts)
        k_mla, v_mla = _kv_up(
            jnp.concatenate([ckv_n[:n_p], ckv_all.reshape(n_ks, KV_RANK)], axis=0),
            jnp.concatenate([kr128[:n_p], kr_all.reshape(n_ks, LANES)], axis=0), w_uk, w_uv)
        o_b_p = _mla_attention(
            q_mla[:n_p].reshape(bp, tp, H_B * MLA_SLAB), k_mla[:n_p].reshape(bp, tp, H_B * MLA_SLAB),
            v_mla[:n_p].reshape(bp, tp, W_B), n_past=0, tq=tq_p, tk=tq_p)
        o_b_s = _mla_attention(
            q_mla[n_p:].reshape(bs, ts, H_B * MLA_SLAB), k_mla[n_p:].reshape(bs, n_past + ts, H_B * MLA_SLAB),
            v_mla[n_p:].reshape(bs, n_past + ts, W_B), n_past=n_past, tq=ts, tk=512)

        u_p = u[:n_p].reshape(bp, tp, C_C)
        u_s = u[n_p:].reshape(bs, ts, C_C)
        halo = CONV_W - 1
        xp_p = jnp.pad(u_p, ((0, 0), (halo, _CONV_HALO - halo), (0, 0)))
        xp_s = jnp.concatenate([state_conv[l].astype(F32), u_s], axis=1)
        conv_p = xp_p[:, tp:tp + halo]
        conv_s = xp_s[:, ts:ts + halo]
        xp_s = jnp.pad(xp_s, ((0, 0), (0, _CONV_HALO - halo), (0, 0)))
        conv_w = (w_dw[l], b_dw[l][None], ln_g[l][None], ln_b[l][None])
        o_c_p = _conv_module(xp_p, *conv_w, t=tp, tt=conv_tt)
        o_c_s = _conv_module(xp_s, *conv_w, t=ts, tt=ts)

        o_a = jnp.concatenate([o_a_p.reshape(n_p, W_A), o_a_s.reshape(n_s, W_A)], axis=0)
        o_b = jnp.concatenate([o_b_p.reshape(n_p, W_B), o_b_s.reshape(n_s, W_B)], axis=0)
        o_c = jnp.concatenate([o_c_p.reshape(n_p, C_C), o_c_s.reshape(n_s, C_C)], axis=0)
        w_r = _pad_lanes(jnp.concatenate([w_rg[l], w_re[l]], axis=1), LANES).astype(BF16)
        b_r = _pad_lanes(jnp.concatenate([b_rg[l], b_re[l].reshape(-1)])[None, :].astype(F32), LANES)
        x, h2, rlog = _out_proj(o_a, o_b, o_c, x, g_out[l][None], w_out[l].astype(BF16), g_ffn[l][None], w_r, b_r)
        x = x + _hier_moe(h2, rlog, w_gate[l], w_up[l], w_down[l])

        states_p.append((ka[:n_p].reshape(bp, tp, H_A, D_HA), va[:n_p].reshape(bp, tp, H_A, D_HA), lf_p,
                         ckv_n[:n_p].reshape(bp, tp, KV_RANK), kr128[:n_p, :ROPE].reshape(bp, tp, ROPE), conv_p))
        states_s.append((ka[n_p:].reshape(bs, ts, H_A, D_HA), va[n_p:].reshape(bs, ts, H_A, D_HA), lf_s,
                         ckv_s, kr_s[:, :, :ROPE], conv_s))

    y = _final_norm(x, g_final[None])
    p_out = [jnp.stack(a) for a in zip(*states_p)]
    s_out = [jnp.stack(a) for a in zip(*states_s)]
    return (y[:n_p].reshape(bp, tp, D_MODEL), y[n_p:].reshape(bs, ts, D_MODEL), *p_out, *s_out)
```

```python
import functools

import numpy as np
import jax
import jax.numpy as jnp
from jax import lax
from jax.experimental import pallas as pl
from jax.experimental.pallas import tpu as pltpu

F32 = jnp.float32
BF16 = jnp.bfloat16

D_MODEL = 2048
DEPTH = 4
CHUNK = 64
H_A, D_HA = 8, 64
W_A = H_A * D_HA
H_B, NOPE, ROPE, V_HD = 8, 128, 64, 128
Q_RANK, KV_RANK = 512, 256
W_B = H_B * V_HD
C_C = D_MODEL - W_A - W_B
CONV_W = 31
_CHUNK_SHIFT = CHUNK.bit_length() - 1
assert 1 << _CHUNK_SHIFT == CHUNK
ROPE_THETA = 10000.0
N_GROUPS, EXPERTS_PER_GROUP = 4, 8
N_EXPERTS = N_GROUPS * EXPERTS_PER_GROUP
TOP_K = 2
D_EXPERT = 512
EPS = 1e-6
NEG_INF = -1e30

LANES = 128
SUBLANES = 8
TQ_PROMPT = 512
TK_SAMPLE = 512
MLA_HEADS_PER_STEP = 4
FOX_PAIRS_PER_STEP = 2
CONV_TILE = 256
MLA_SLAB = NOPE + LANES
ROW_TILE = 256
EXPERT_TILE = 256
VMEM_LIMIT = 56 * 1024 * 1024

_C_QA, _C_KA, _C_VA, _C_CQ, _C_CKV, _C_UA, _C_UG, _C_KRF = 0, 512, 1024, 1536, 2048, 2304, 2816, 3328
IN_COLS = _C_KRF + LANES
_FA_LANE = ROPE


def _cparams(*sem):
    return pltpu.CompilerParams(dimension_semantics=sem, vmem_limit_bytes=VMEM_LIMIT)


def _const_spec(shape):
    nd = len(shape)
    return pl.BlockSpec(shape, lambda *_: (0,) * nd)


def _rms(x, g):
    return x * lax.rsqrt(jnp.mean(x * x, axis=-1, keepdims=True) + EPS) * g


def _rope_block(x, cos, sin_signed):
    lane = lax.broadcasted_iota(jnp.int32, x.shape, 1)
    partner = jnp.where(lane < ROPE // 2, pltpu.roll(x, LANES - ROPE // 2, 1), pltpu.roll(x, ROPE // 2, 1))
    return x * cos + partner * sin_signed


def _in_proj_kernel(x_ref, gmix_ref, win_ref, gq_ref, gkv_ref, wuq_ref, bf_ref, cos_ref, sin_ref,
                    qa_ref, ka_ref, va_ref, kab_ref, vab_ref, logf_ref, qmla_ref, ckv_ref, kr_ref, u_ref):
    h = _rms(x_ref[...], gmix_ref[...]).astype(BF16)

    def proj(c0, width):
        return jnp.dot(h, win_ref[:, c0:c0 + width], preferred_element_type=F32)

    qa_ref[...] = proj(_C_QA, W_A).astype(BF16)
    ka = proj(_C_KA, W_A)
    ka_ref[...] = ka
    kab_ref[...] = ka.astype(BF16)
    va = proj(_C_VA, W_A)
    va_ref[...] = va
    vab_ref[...] = va.astype(BF16)

    cos = cos_ref[...]
    sin = sin_ref[...]
    krf = proj(_C_KRF, LANES)
    kr_ref[...] = _rope_block(krf, cos, sin)
    z = krf + bf_ref[...]
    logsig = jnp.minimum(z, 0.0) - jnp.log(1.0 + jnp.exp(-jnp.abs(z)))
    logf_ref[...] = logsig[:, _FA_LANE:_FA_LANE + H_A]

    ckv_ref[...] = _rms(proj(_C_CKV, KV_RANK), gkv_ref[...])

    u_ref[...] = proj(_C_UA, C_C) * jax.nn.sigmoid(proj(_C_UG, C_C))

    cqn = _rms(proj(_C_CQ, Q_RANK), gq_ref[...]).astype(BF16)
    scale = (NOPE + ROPE) ** -0.5
    for hd in range(H_B):
        q = jnp.dot(cqn, wuq_ref[:, hd * MLA_SLAB:(hd + 1) * MLA_SLAB], preferred_element_type=F32)
        qmla_ref[:, hd * MLA_SLAB:hd * MLA_SLAB + NOPE] = (q[:, :NOPE] * scale).astype(BF16)
        qmla_ref[:, hd * MLA_SLAB + NOPE:(hd + 1) * MLA_SLAB] = (
            _rope_block(q[:, NOPE:], cos, sin) * scale).astype(BF16)


def _in_proj(x, g_mix, w_in_p, g_q, g_kv, w_uq_p, b_f128, cos, sin):
    n = x.shape[0]
    tm = ROW_TILE
    row = lambda w: pl.BlockSpec((tm, w), lambda i: (i, 0))
    out_shapes = (
        jax.ShapeDtypeStruct((n, W_A), BF16),
        jax.ShapeDtypeStruct((n, W_A), F32),
        jax.ShapeDtypeStruct((n, W_A), F32),
        jax.ShapeDtypeStruct((n, W_A), BF16),
        jax.ShapeDtypeStruct((n, W_A), BF16),
        jax.ShapeDtypeStruct((n, H_A), F32),
        jax.ShapeDtypeStruct((n, H_B * MLA_SLAB), BF16),
        jax.ShapeDtypeStruct((n, KV_RANK), F32),
        jax.ShapeDtypeStruct((n, LANES), F32),
        jax.ShapeDtypeStruct((n, C_C), F32),
    )
    return pl.pallas_call(
        _in_proj_kernel,
        grid=(n // tm,),
        in_specs=[row(D_MODEL), _const_spec((1, D_MODEL)), _const_spec((D_MODEL, IN_COLS)),
                  _const_spec((1, Q_RANK)), _const_spec((1, KV_RANK)),
                  _const_spec((Q_RANK, H_B * MLA_SLAB)), _const_spec((1, LANES)), row(LANES), row(LANES)],
        out_specs=[row(W_A), row(W_A), row(W_A), row(W_A), row(W_A), row(H_A), row(H_B * MLA_SLAB),
                   row(KV_RANK), row(LANES), row(C_C)],
        out_shape=out_shapes,
        compiler_params=_cparams("parallel"),
        name="in_proj",
    )(x, g_mix, w_in_p, g_q, g_kv, w_uq_p, b_f128, cos, sin)


def _cumsum_kernel(x_ref, o_ref):
    c = x_ref[...]
    t = c.shape[-1]
    lane = lax.broadcasted_iota(jnp.int32, c.shape, 1)
    s = 1
    while s < t:
        c = c + jnp.where(lane >= s, pltpu.roll(c, s, 1), 0.0)
        s *= 2
    o_ref[...] = c


def _cumsum_time(logf_t):
    b, hh, t = logf_t.shape
    spec = pl.BlockSpec((None, hh, t), lambda i: (i, 0, 0))
    return pl.pallas_call(
        _cumsum_kernel, grid=(b,), in_specs=[spec], out_specs=spec,
        out_shape=jax.ShapeDtypeStruct((b, hh, t), F32),
        compiler_params=_cparams("parallel"), name="cumsum_logf",
    )(logf_t)


def _softmax_step(s, v, row_bias, m_ref, l_ref, acc_ref):
    width = s.shape[1]
    s_max = jnp.max(s, axis=-1, keepdims=True)
    m_old = m_ref[...]
    m_new = jnp.maximum(m_old, s_max if row_bias is None else s_max + row_bias)
    alpha = jnp.exp(m_old - m_new)
    offset = m_new if row_bias is None else m_new - row_bias
    if width % LANES == 0:
        p = [jnp.exp(s[:, c * LANES:(c + 1) * LANES] - offset) for c in range(width // LANES)]
        l_add = functools.reduce(lambda a, b: a + b, p)
        p = p[0].astype(BF16) if len(p) == 1 else jnp.concatenate([x.astype(BF16) for x in p], axis=1)
    else:
        p = jnp.exp(s - offset[:, :1])
        lane = lax.broadcasted_iota(jnp.int32, m_old.shape, 1)
        l_add = jnp.where(lane == 0, jnp.sum(p, axis=-1, keepdims=True), 0.0)
        p = p.astype(BF16)
    l_ref[...] = alpha * l_ref[...] + l_add
    acc_ref[...] = alpha * acc_ref[...] + jnp.dot(p, v, preferred_element_type=F32)
    m_ref[...] = m_new


def _flash_sweep(heads, n_open, d0, tq, tk, diag_mask, m_ref, l_ref, acc_ref):
    for h in range(len(heads)):
        m_ref[h] = jnp.full(m_ref.shape[1:], NEG_INF, F32)
        l_ref[h] = jnp.zeros(l_ref.shape[1:], F32)
        acc_ref[h] = jnp.zeros(acc_ref.shape[1:], F32)

    def chunk(k0, width, mask):
        for h, (q, key_chunk, value_chunk, col_bias, row_bias) in enumerate(heads):
            s = _qk(q, key_chunk(k0, width))
            if col_bias is not None:
                s = s + col_bias(k0, width)
            if mask is not None:
                s = jnp.where(mask, s, NEG_INF)
            _softmax_step(s, value_chunk(k0, width), row_bias, m_ref.at[h], l_ref.at[h], acc_ref.at[h])

    def open_chunk(i, carry):
        chunk(pl.multiple_of(i * tk, tk), tk, None)
        return carry

    lax.fori_loop(0, n_open, open_chunk, 0)
    chunk(d0, tq, diag_mask)
    return [acc_ref[h] / jnp.sum(l_ref[h], axis=-1, keepdims=True) for h in range(len(heads))]


def _qk(q, k):
    return lax.dot_general(q, k, (((1,), (1,)), ((), ())), preferred_element_type=F32)


def _tile_extent(n_past, tq, tk, single_tile):
    if single_tile:
        return n_past // tk, n_past
    start = n_past + pl.program_id(2) * tq
    return start // tk, pl.multiple_of(start, tq)


def _fox_kernel(q_ref, k_ref, v_ref, cq_ref, ck_ref, o_ref, m_ref, l_ref, acc_ref, *, tq, tk, n_past, single_tile):
    lane = lax.broadcasted_iota(jnp.int32, (tq, LANES), 1)
    n_open, d0 = _tile_extent(n_past, tq, tk, single_tile)
    row = lax.broadcasted_iota(jnp.int32, (tq, tq), 0)
    col = lax.broadcasted_iota(jnp.int32, (tq, tq), 1)
    heads = []
    for pair in range(FOX_PAIRS_PER_STEP):
        lanes = slice(pair * LANES, (pair + 1) * LANES)
        q = q_ref[:, lanes]
        key_chunk = functools.partial(lambda k0, width, lanes: k_ref[pl.ds(k0, width), lanes], lanes=lanes)
        value_chunk = functools.partial(lambda k0, width, lanes: v_ref[pl.ds(k0, width), lanes], lanes=lanes)
        for j in range(2):
            qj = jnp.where((lane >= j * D_HA) & (lane < (j + 1) * D_HA), q, jnp.zeros_like(q))
            col_bias = functools.partial(
                lambda k0, width, pair, j: -ck_ref[pair, j:j + 1, pl.ds(k0, width)], pair=pair, j=j)
            row_bias = jnp.broadcast_to(cq_ref[pair, :, j:j + 1], (tq, LANES))
            heads.append((qj, key_chunk, value_chunk, col_bias, row_bias))
    outs = _flash_sweep(heads, n_open, d0, tq, tk, col <= row, m_ref, l_ref, acc_ref)
    for pair in range(FOX_PAIRS_PER_STEP):
        o_ref[:, pair * LANES:(pair + 1) * LANES] = jnp.where(lane < D_HA, outs[2 * pair], outs[2 * pair + 1])


def _fox_attention(q, k, v, c_q, c_k, *, n_past, tq, tk):
    b, t_q, _ = q.shape
    t_k = k.shape[1]
    t_kp = c_k.shape[-1]
    assert t_q % tq == 0 and n_past % tk == 0 and (t_q == tq or tq % tk == 0)
    npair = FOX_PAIRS_PER_STEP
    kern = functools.partial(_fox_kernel, tq=tq, tk=tk, n_past=n_past, single_tile=t_q == tq)
    return pl.pallas_call(
        kern,
        grid=(b, H_A // (2 * npair), t_q // tq),
        in_specs=[pl.BlockSpec((None, tq, npair * LANES), lambda bi, hp, qi: (bi, qi, hp)),
                  pl.BlockSpec((None, t_k, npair * LANES), lambda bi, hp, qi: (bi, 0, hp)),
                  pl.BlockSpec((None, t_k, npair * LANES), lambda bi, hp, qi: (bi, 0, hp)),
                  pl.BlockSpec((None, npair, tq, 2), lambda bi, hp, qi: (bi, hp, qi, 0)),
                  pl.BlockSpec((None, npair, 2, t_kp), lambda bi, hp, qi: (bi, hp, 0, 0))],
        out_specs=pl.BlockSpec((None, tq, npair * LANES), lambda bi, hp, qi: (bi, qi, hp)),
        out_shape=jax.ShapeDtypeStruct((b, t_q, W_A), F32),
        scratch_shapes=[pltpu.VMEM((2 * npair, tq, LANES), F32)] * 3,
        compiler_params=_cparams("parallel", "parallel", "arbitrary"),
        name="fox_attention",
    )(q, k, v, c_q, c_k)


def _mla_kernel(q_ref, k_ref, v_ref, o_ref, m_ref, l_ref, acc_ref, *, tq, tk, n_past, single_tile):
    n_open, d0 = _tile_extent(n_past, tq, tk, single_tile)
    row = lax.broadcasted_iota(jnp.int32, (tq, tq), 0)
    col = lax.broadcasted_iota(jnp.int32, (tq, tq), 1)
    heads = []
    for j in range(MLA_HEADS_PER_STEP):
        key_chunk = functools.partial(
            lambda k0, width, j: k_ref[pl.ds(k0, width), j * MLA_SLAB:(j + 1) * MLA_SLAB], j=j)
        value_chunk = functools.partial(lambda k0, width, j: v_ref[pl.ds(k0, width), j * V_HD:(j + 1) * V_HD], j=j)
        heads.append((q_ref[:, j * MLA_SLAB:(j + 1) * MLA_SLAB], key_chunk, value_chunk, None, None))
    mask = (d0 + col) >> _CHUNK_SHIFT <= (d0 + row) >> _CHUNK_SHIFT
    outs = _flash_sweep(heads, n_open, d0, tq, tk, mask, m_ref, l_ref, acc_ref)
    for j in range(MLA_HEADS_PER_STEP):
        o_ref[:, j * V_HD:(j + 1) * V_HD] = outs[j]


def _mla_attention(q, k, v, *, n_past, tq, tk):
    b, t_q, _ = q.shape
    t_k = k.shape[1]
    assert t_q % tq == 0 and n_past % tk == 0 and (t_q == tq or tq % tk == 0)
    nh = MLA_HEADS_PER_STEP
    kern = functools.partial(_mla_kernel, tq=tq, tk=tk, n_past=n_past, single_tile=t_q == tq)
    return pl.pallas_call(
        kern,
        grid=(b, H_B // nh, t_q // tq),
        in_specs=[pl.BlockSpec((None, tq, nh * MLA_SLAB), lambda bi, hd, qi: (bi, qi, hd)),
                  pl.BlockSpec((None, t_k, nh * MLA_SLAB), lambda bi, hd, qi: (bi, 0, hd)),
                  pl.BlockSpec((None, t_k, nh * V_HD), lambda bi, hd, qi: (bi, 0, hd))],
        out_specs=pl.BlockSpec((None, tq, nh * V_HD), lambda bi, hd, qi: (bi, qi, hd)),
        out_shape=jax.ShapeDtypeStruct((b, t_q, W_B), F32),
        scratch_shapes=[pltpu.VMEM((nh, tq, LANES), F32), pltpu.VMEM((nh, tq, LANES), F32),
                        pltpu.VMEM((nh, tq, V_HD), F32)],
        compiler_params=_cparams("parallel", "parallel", "arbitrary"),
        name="mla_attention",
    )(q, k, v)


def _kv_up_kernel(ckv_ref, kr_ref, wuk_ref, wuv_ref, k_ref, v_ref):
    c = ckv_ref[...].astype(BF16)
    kr = kr_ref[...].astype(BF16)
    kn = jnp.dot(c, wuk_ref[...], preferred_element_type=F32).astype(BF16)
    for hd in range(H_B):
        k_ref[:, hd * MLA_SLAB:hd * MLA_SLAB + NOPE] = kn[:, hd * NOPE:(hd + 1) * NOPE]
        k_ref[:, hd * MLA_SLAB + NOPE:(hd + 1) * MLA_SLAB] = kr
    v_ref[...] = jnp.dot(c, wuv_ref[...], preferred_element_type=F32).astype(BF16)


def _kv_up(ckv_n, krope128, w_uk, w_uv):
    r = ckv_n.shape[0]
    tm = ROW_TILE
    row = lambda w: pl.BlockSpec((tm, w), lambda i: (i, 0))
    return pl.pallas_call(
        _kv_up_kernel, grid=(r // tm,),
        in_specs=[row(KV_RANK), row(LANES), _const_spec((KV_RANK, H_B * NOPE)),
                  _const_spec((KV_RANK, H_B * V_HD))],
        out_specs=[row(H_B * MLA_SLAB), row(H_B * V_HD)],
        out_shape=(jax.ShapeDtypeStruct((r, H_B * MLA_SLAB), BF16), jax.ShapeDtypeStruct((r, H_B * V_HD), BF16)),
        compiler_params=_cparams("parallel"), name="mla_kv_up",
    )(ckv_n, krope128, w_uk, w_uv)


_CONV_SUB = 64
_CONV_HALO = -(-CONV_W // SUBLANES) * SUBLANES


def _conv_kernel(xp_ref, w_ref, b_ref, g_ref, beta_ref, o_ref, *, tt):
    t0 = pl.program_id(1) * tt
    w = w_ref[...]
    rows = min(_CONV_SUB, tt)
    for sub in range(tt // rows):
        base = pl.multiple_of(t0 + sub * rows, rows)
        xa = xp_ref[pl.ds(base, rows + _CONV_HALO), :]
        acc = None
        for b in range(SUBLANES):
            z = None
            for a in range(-(-CONV_W // SUBLANES)):
                tap = SUBLANES * a + b
                if tap < CONV_W:
                    term = xa[SUBLANES * a:SUBLANES * a + rows + SUBLANES, :] * w[tap:tap + 1, :]
                    z = term if z is None else z + term
            acc = z[b:b + rows, :] if acc is None else acc + z[b:b + rows, :]
        y = acc + b_ref[...]
        mu = jnp.mean(y, axis=-1, keepdims=True)
        yc = y - mu
        var = jnp.mean(yc * yc, axis=-1, keepdims=True)
        y = yc * lax.rsqrt(var + EPS) * g_ref[...] + beta_ref[...]
        o_ref[sub * rows:(sub + 1) * rows, :] = y * jax.nn.sigmoid(y)


def _conv_module(xp, w_dw, b_dw, ln_g, ln_b, *, t, tt):
    b, t_p, _ = xp.shape
    kern = functools.partial(_conv_kernel, tt=tt)
    return pl.pallas_call(
        kern, grid=(b, t // tt),
        in_specs=[pl.BlockSpec((None, t_p, C_C), lambda bi, ti: (bi, 0, 0)),
                  _const_spec((CONV_W, C_C)), _const_spec((1, C_C)), _const_spec((1, C_C)), _const_spec((1, C_C))],
        out_specs=pl.BlockSpec((None, tt, C_C), lambda bi, ti: (bi, ti, 0)),
        out_shape=jax.ShapeDtypeStruct((b, t, C_C), F32),
        compiler_params=_cparams("parallel", "arbitrary"), name="conv_module",
    )(xp, w_dw, b_dw, ln_g, ln_b)


def _out_proj_kernel(oa_ref, ob_ref, oc_ref, x_ref, gout_ref, wout_ref, gffn_ref, wr_ref, br_ref,
                     xn_ref, h2_ref, rl_ref):
    g = gout_ref[...]
    na = _rms(oa_ref[...], g[:, :W_A]).astype(BF16)
    nb = _rms(ob_ref[...], g[:, W_A:W_A + W_B]).astype(BF16)
    nc = _rms(oc_ref[...], g[:, W_A + W_B:]).astype(BF16)
    mix = jnp.dot(na, wout_ref[:W_A, :], preferred_element_type=F32)
    mix = mix + jnp.dot(nb, wout_ref[W_A:W_A + W_B, :], preferred_element_type=F32)
    mix = mix + jnp.dot(nc, wout_ref[W_A + W_B:, :], preferred_element_type=F32)
    xn = x_ref[...] + mix
    xn_ref[...] = xn
    h2 = _rms(xn, gffn_ref[...]).astype(BF16)
    h2_ref[...] = h2
    rl_ref[...] = jnp.dot(h2, wr_ref[...], preferred_element_type=F32) + br_ref[...]


def _out_proj(o_a, o_b, o_c, x, g_out, w_out, g_ffn, w_r, b_r):
    n = x.shape[0]
    tm = ROW_TILE
    row = lambda w: pl.BlockSpec((tm, w), lambda i: (i, 0))
    return pl.pallas_call(
        _out_proj_kernel, grid=(n // tm,),
        in_specs=[row(W_A), row(W_B), row(C_C), row(D_MODEL), _const_spec((1, D_MODEL)),
                  _const_spec((D_MODEL, D_MODEL)), _const_spec((1, D_MODEL)),
                  _const_spec((D_MODEL, LANES)), _const_spec((1, LANES))],
        out_specs=[row(D_MODEL), row(D_MODEL), row(LANES)],
        out_shape=(jax.ShapeDtypeStruct((n, D_MODEL), F32), jax.ShapeDtypeStruct((n, D_MODEL), BF16),
                   jax.ShapeDtypeStruct((n, LANES), F32)),
        compiler_params=_cparams("parallel"), name="out_proj_router",
    )(o_a, o_b, o_c, x, g_out, w_out, g_ffn, w_r, b_r)


def _expert_kernel(be_ref, nv_ref, xs_ref, wg_ref, wu_ref, wd_ref, y_ref, wgb_ref, wub_ref, wdb_ref):
    i = pl.program_id(0)
    prev = be_ref[jnp.maximum(i - 1, 0)]

    @pl.when((i == 0) | (be_ref[i] != prev))
    def _():
        wgb_ref[...] = wg_ref[...].astype(BF16)
        wub_ref[...] = wu_ref[...].astype(BF16)
        wdb_ref[...] = wd_ref[...].astype(BF16)

    @pl.when(i < nv_ref[0])
    def _():
        x = xs_ref[...]
        gate = jnp.dot(x, wgb_ref[...], preferred_element_type=F32)
        up = jnp.dot(x, wub_ref[...], preferred_element_type=F32)
        hid = (gate * jax.nn.sigmoid(gate) * up).astype(BF16)
        y_ref[...] = jnp.dot(hid, wdb_ref[...], preferred_element_type=F32)

    @pl.when(i >= nv_ref[0])
    def _():
        y_ref[...] = jnp.zeros(y_ref.shape, F32)


def _expert_blocks(block_e, n_valid, xs, w_gate, w_up, w_down, *, layer):
    rows = xs.shape[0]
    te = EXPERT_TILE
    grid_spec = pltpu.PrefetchScalarGridSpec(
        num_scalar_prefetch=2, grid=(rows // te,),
        in_specs=[pl.BlockSpec((te, D_MODEL), lambda i, be, nv: (i, 0)),
                  pl.BlockSpec((None, None, D_MODEL, D_EXPERT), lambda i, be, nv: (layer, be[i], 0, 0)),
                  pl.BlockSpec((None, None, D_MODEL, D_EXPERT), lambda i, be, nv: (layer, be[i], 0, 0)),
                  pl.BlockSpec((None, None, D_EXPERT, D_MODEL), lambda i, be, nv: (layer, be[i], 0, 0))],
        out_specs=pl.BlockSpec((te, D_MODEL), lambda i, be, nv: (i, 0)),
        scratch_shapes=[pltpu.VMEM((D_MODEL, D_EXPERT), BF16), pltpu.VMEM((D_MODEL, D_EXPERT), BF16),
                        pltpu.VMEM((D_EXPERT, D_MODEL), BF16)])
    return pl.pallas_call(
        _expert_kernel, grid_spec=grid_spec,
        out_shape=jax.ShapeDtypeStruct((rows, D_MODEL), F32),
        compiler_params=_cparams("arbitrary"), name="expert_blocks",
    )(block_e, n_valid, xs, w_gate, w_up, w_down)


def _route(rlog):
    n = rlog.shape[0]
    idx = jnp.arange(n)
    g_logits = rlog[:, :N_GROUPS]
    g_sel = jnp.argmax(g_logits, axis=-1)
    g_w = jax.nn.softmax(g_logits, axis=-1)[idx, g_sel]
    e_logits = rlog[:, N_GROUPS:N_GROUPS + N_EXPERTS].reshape(n, N_GROUPS, EXPERTS_PER_GROUP)[idx, g_sel]
    top_v, top_i = lax.top_k(e_logits, TOP_K)
    gate = jax.nn.softmax(top_v, axis=-1) * g_w[:, None]
    expert_ids = (g_sel[:, None] * EXPERTS_PER_GROUP + top_i).astype(jnp.int32)
    return expert_ids, gate


def _dispatch_plan(expert_ids):
    n = expert_ids.shape[0]
    te = EXPERT_TILE
    a = n * TOP_K
    n_blocks = -(-a // te) + N_EXPERTS
    rows = n_blocks * te
    flat_e = expert_ids.reshape(-1)
    experts = jnp.arange(N_EXPERTS, dtype=jnp.int32)
    onehot = (flat_e[:, None] == experts[None, :]).astype(jnp.int32)
    csum = jnp.cumsum(onehot, axis=0)
    counts = csum[-1]
    rank = jnp.sum(csum * onehot, axis=-1) - 1
    padded = (counts + te - 1) // te * te
    pad_end = jnp.cumsum(padded)
    pad_start = pad_end - padded
    start = jnp.cumsum(counts) - counts
    pos = (jnp.sum(onehot * pad_start[None, :], axis=-1) + rank).astype(jnp.int32).reshape(n, TOP_K)

    tok_sorted = (jnp.argsort(flat_e, stable=True) // TOP_K).astype(jnp.int32)
    slot = jnp.arange(rows, dtype=jnp.int32)
    slot_e = jnp.minimum(jnp.sum((pad_end[None, :] <= slot[:, None]).astype(jnp.int32), axis=-1), N_EXPERTS - 1)
    slot_hot = (slot_e[:, None] == experts[None, :]).astype(jnp.int32)
    slot_rank = slot - jnp.sum(slot_hot * pad_start[None, :], axis=-1)
    filled = slot_rank < jnp.sum(slot_hot * counts[None, :], axis=-1)
    src = jnp.where(filled, jnp.sum(slot_hot * start[None, :], axis=-1) + slot_rank, 0)
    slot_tok = tok_sorted[src]
    block_e = slot_e.reshape(n_blocks, te)[:, 0]
    n_valid = (pad_end[-1] // te).astype(jnp.int32).reshape(1)
    return slot_tok, pos, block_e, n_valid


def _final_norm_kernel(x_ref, g_ref, o_ref):
    o_ref[...] = _rms(x_ref[...], g_ref[...])


def _final_norm(x, g):
    n = x.shape[0]
    row = pl.BlockSpec((ROW_TILE, D_MODEL), lambda i: (i, 0))
    return pl.pallas_call(
        _final_norm_kernel, grid=(n // ROW_TILE,), in_specs=[row, _const_spec((1, D_MODEL))], out_specs=row,
        out_shape=jax.ShapeDtypeStruct((n, D_MODEL), F32), compiler_params=_cparams("parallel"),
        name="final_norm",
    )(x, g)


def _rope_tables(pos):
    half = ROPE // 2
    inv = ROPE_THETA ** (-jnp.arange(half, dtype=F32) / half)
    ang = pos.astype(F32)[:, None] * inv[None, :]
    zeros = jnp.zeros((pos.shape[0], LANES - ROPE), F32)
    cos = jnp.concatenate([jnp.cos(ang), jnp.cos(ang), zeros], axis=-1)
    sin = jnp.concatenate([-jnp.sin(ang), jnp.sin(ang), zeros], axis=-1)
    return cos, sin


def _prep_w_in(w_in):
    qa, ka, va, fa, cq, ckv, kr, ua, ug = jnp.split(
        w_in, np.cumsum([W_A, W_A, W_A, H_A, Q_RANK, KV_RANK, ROPE, C_C])[:].tolist(), axis=1)
    pad = jnp.zeros((D_MODEL, LANES - ROPE - H_A), w_in.dtype)
    return jnp.concatenate([qa * (D_HA ** -0.5), ka, va, cq, ckv, ua, ug, kr, fa, pad], axis=1).astype(BF16)


def _prep_w_uq(w_uq):
    w = w_uq.reshape(Q_RANK, H_B, NOPE + ROPE)
    w = jnp.pad(w, ((0, 0), (0, 0), (0, MLA_SLAB - NOPE - ROPE)))
    return w.reshape(Q_RANK, H_B * MLA_SLAB).astype(BF16)


def _prep_w_ukv(w_ukv):
    w = w_ukv.reshape(KV_RANK, H_B, NOPE + V_HD)
    return (w[:, :, :NOPE].reshape(KV_RANK, H_B * NOPE).astype(BF16),
            w[:, :, NOPE:].reshape(KV_RANK, H_B * V_HD).astype(BF16))


def _pad_lanes(x, width):
    return jnp.pad(x, ((0, 0),) * (x.ndim - 1) + ((0, width - x.shape[-1]),))


def _round_up(x, m):
    return -(-x // m) * m


def kernel(x_prompt, x_sample, cache_fox_k, cache_fox_v, cache_fox_logf, cache_mla_ckv, cache_mla_krope, state_conv, g_mix, w_in, b_f, g_q, w_uq, g_kv, w_ukv, w_dw, b_dw, ln_g, ln_b, g_out, w_out, g_ffn, w_rg, b_rg, w_re, b_re, w_gate, w_up, w_down, g_final):
    bp, tp, _ = x_prompt.shape
    bs, ts, _ = x_sample.shape
    n_past = cache_fox_k.shape[2]
    n_p, n_s = bp * tp, bs * ts
    depth = g_mix.shape[0]

    halo = CONV_W - 1
    tkp_s = _round_up(n_past + ts, LANES)

    x_p = x_prompt.reshape(n_p, D_MODEL)
    x_s = x_sample.reshape(n_s, D_MODEL)
    cos_p, sin_p = _rope_tables(jnp.arange(tp))
    cos_s, sin_s = _rope_tables(n_past + jnp.arange(ts))
    rope_p = (jnp.tile(cos_p, (bp, 1)), jnp.tile(sin_p, (bp, 1)))
    rope_s = (jnp.tile(cos_s, (bs, 1)), jnp.tile(sin_s, (bs, 1)))

    def split_heads(c):
        return c.reshape(c.shape[0], H_A // 2, 2, c.shape[-1])

    states_p, states_s = [], []
    for l in range(depth):
        w_uk, w_uv = _prep_w_ukv(w_ukv[l])
        b_f128 = jnp.pad(b_f[l], (_FA_LANE, LANES - _FA_LANE - H_A))[None, :]
        in_w = (g_mix[l][None], _prep_w_in(w_in[l]), g_q[l][None], g_kv[l][None], _prep_w_uq(w_uq[l]), b_f128)
        qa_p, ka_p, va_p, kab_p, vab_p, lf_p, qm_p, ckv_p, kr_p, u_p = _in_proj(x_p, *in_w, *rope_p)
        qa_s, ka_s, va_s, kab_s, vab_s, lf_s, qm_s, ckv_s, kr_s, u_s = _in_proj(x_s, *in_w, *rope_s)

        lf_p = lf_p.reshape(bp, tp, H_A)
        lf_s = lf_s.reshape(bs, ts, H_A)
        c_p = _cumsum_time(lf_p.transpose(0, 2, 1))
        lf_all = jnp.concatenate([cache_fox_logf[l].astype(F32), lf_s], axis=1)
        c_s = _cumsum_time(_pad_lanes(lf_all.transpose(0, 2, 1), tkp_s))
        o_a_p = _fox_attention(
            qa_p.reshape(bp, tp, W_A), kab_p.reshape(bp, tp, W_A), vab_p.reshape(bp, tp, W_A),
            split_heads(c_p).transpose(0, 1, 3, 2), split_heads(c_p), n_past=0, tq=TQ_PROMPT, tk=TQ_PROMPT)
        k_all = jnp.concatenate([cache_fox_k[l].reshape(bs, n_past, W_A).astype(BF16),
                                 kab_s.reshape(bs, ts, W_A)], axis=1)
        v_all = jnp.concatenate([cache_fox_v[l].reshape(bs, n_past, W_A).astype(BF16),
                                 vab_s.reshape(bs, ts, W_A)], axis=1)
        o_a_s = _fox_attention(
            qa_s.reshape(bs, ts, W_A), k_all, v_all,
            split_heads(c_s[:, :, n_past:n_past + ts]).transpose(0, 1, 3, 2), split_heads(c_s),
            n_past=n_past, tq=ts, tk=TK_SAMPLE)

        ckv_s = ckv_s.reshape(bs, ts, KV_RANK)
        kr_s = kr_s.reshape(bs, ts, LANES)
        ckv_all = jnp.concatenate([cache_mla_ckv[l].astype(F32), ckv_s], axis=1)
        kr_all = jnp.concatenate([_pad_lanes(cache_mla_krope[l].astype(F32), LANES), kr_s], axis=1)
        km_p, vm_p = _kv_up(ckv_p, kr_p, w_uk, w_uv)
        km_s, vm_s = _kv_up(ckv_all.reshape(-1, KV_RANK), kr_all.reshape(-1, LANES), w_uk, w_uv)
        o_b_p = _mla_attention(
            qm_p.reshape(bp, tp, H_B * MLA_SLAB), km_p.reshape(bp, tp, H_B * MLA_SLAB), vm_p.reshape(bp, tp, W_B),
            n_past=0, tq=TQ_PROMPT, tk=TQ_PROMPT)
        o_b_s = _mla_attention(
            qm_s.reshape(bs, ts, H_B * MLA_SLAB), km_s.reshape(bs, n_past + ts, H_B * MLA_SLAB),
            vm_s.reshape(bs, n_past + ts, W_B), n_past=n_past, tq=ts, tk=TK_SAMPLE)

        xp_p = jnp.pad(u_p.reshape(bp, tp, C_C), ((0, 0), (halo, _CONV_HALO - halo), (0, 0)))
        xp_s = jnp.concatenate([state_conv[l].astype(F32), u_s.reshape(bs, ts, C_C)], axis=1)
        conv_p = xp_p[:, tp:tp + halo]
        conv_s = xp_s[:, ts:ts + halo]
        xp_s = jnp.pad(xp_s, ((0, 0), (0, _CONV_HALO - halo), (0, 0)))
        conv_w = (w_dw[l], b_dw[l][None], ln_g[l][None], ln_b[l][None])
        o_c_p = _conv_module(xp_p, *conv_w, t=tp, tt=CONV_TILE)
        o_c_s = _conv_module(xp_s, *conv_w, t=ts, tt=ts)

        w_r = _pad_lanes(jnp.concatenate([w_rg[l], w_re[l]], axis=1), LANES).astype(BF16)
        b_r = _pad_lanes(jnp.concatenate([b_rg[l], b_re[l].reshape(-1)])[None, :].astype(F32), LANES)
        out_w = (g_out[l][None], w_out[l].astype(BF16), g_ffn[l][None], w_r, b_r)
        x_p, h2_p, rlog_p = _out_proj(o_a_p.reshape(n_p, W_A), o_b_p.reshape(n_p, W_B), o_c_p.reshape(n_p, C_C),
                                      x_p, *out_w)
        x_s, h2_s, rlog_s = _out_proj(o_a_s.reshape(n_s, W_A), o_b_s.reshape(n_s, W_B), o_c_s.reshape(n_s, C_C),
                                      x_s, *out_w)

        expert_ids, gate = _route(jnp.concatenate([rlog_p, rlog_s], axis=0))
        slot_tok, pos, block_e, n_valid = _dispatch_plan(expert_ids)
        xs = jnp.concatenate([h2_p, h2_s], axis=0)[slot_tok]
        ys = _expert_blocks(block_e, n_valid, xs, w_gate, w_up, w_down, layer=l)

        def combine(x, pos, gate):
            return x + ys[pos[:, 0]] * gate[:, 0:1] + ys[pos[:, 1]] * gate[:, 1:2]

        x_p = combine(x_p, pos[:n_p], gate[:n_p])
        x_s = combine(x_s, pos[n_p:], gate[n_p:])

        states_p.append((ka_p.reshape(bp, tp, H_A, D_HA), va_p.reshape(bp, tp, H_A, D_HA), lf_p,
                         ckv_p.reshape(bp, tp, KV_RANK), kr_p[:, :ROPE].reshape(bp, tp, ROPE), conv_p))
        states_s.append((ka_s.reshape(bs, ts, H_A, D_HA), va_s.reshape(bs, ts, H_A, D_HA), lf_s,
                         ckv_s, kr_s[:, :, :ROPE], conv_s))

    y_p = _final_norm(x_p, g_final[None]).reshape(bp, tp, D_MODEL)
    y_s = _final_norm(x_s, g_final[None]).reshape(bs, ts, D_MODEL)
    p_out = [jnp.stack(a) for a in zip(*states_p)]
    s_out = [jnp.stack(a) for a in zip(*states_s)]
    return (y_p, y_s, *p_out, *s_out)
```

```python
import functools

import numpy as np
import jax
import jax.numpy as jnp
from jax import lax
from jax.experimental import pallas as pl
from jax.experimental.pallas import tpu as pltpu

F32 = jnp.float32
BF16 = jnp.bfloat16

D_MODEL = 2048
DEPTH = 4
CHUNK = 64
H_A, D_HA = 8, 64
W_A = H_A * D_HA
H_B, NOPE, ROPE, V_HD = 8, 128, 64, 128
Q_RANK, KV_RANK = 512, 256
W_B = H_B * V_HD
C_C = D_MODEL - W_A - W_B
CONV_W = 31
_CHUNK_SHIFT = CHUNK.bit_length() - 1
assert 1 << _CHUNK_SHIFT == CHUNK
ROPE_THETA = 10000.0
N_GROUPS, EXPERTS_PER_GROUP = 4, 8
N_EXPERTS = N_GROUPS * EXPERTS_PER_GROUP
TOP_K = 2
D_EXPERT = 512
EPS = 1e-6
NEG_INF = -1e30

LANES = 128
SUBLANES = 8
TQ_PROMPT = 512
TK_SAMPLE = 512
MLA_HEADS_PER_STEP = 4
FOX_PAIRS_PER_STEP = 2
CONV_TILE = 256
MLA_SLAB = NOPE + LANES
ROW_TILE = 256
EXPERT_TILE = 256
EXPERT_DUMP_ROWS = 2 * EXPERT_TILE
VMEM_LIMIT = 56 * 1024 * 1024

_C_QA, _C_KA, _C_VA, _C_CQ, _C_CKV, _C_UA, _C_UG, _C_KRF = 0, 512, 1024, 1536, 2048, 2304, 2816, 3328
IN_COLS = _C_KRF + LANES
_FA_LANE = ROPE


def _cparams(*sem):
    return pltpu.CompilerParams(dimension_semantics=sem, vmem_limit_bytes=VMEM_LIMIT)


def _const_spec(shape):
    nd = len(shape)
    return pl.BlockSpec(shape, lambda *_: (0,) * nd)


def _rms(x, g):
    return x * lax.rsqrt(jnp.mean(x * x, axis=-1, keepdims=True) + EPS) * g


def _rope_block(x, cos, sin_signed):
    lane = lax.broadcasted_iota(jnp.int32, x.shape, 1)
    partner = jnp.where(lane < ROPE // 2, pltpu.roll(x, LANES - ROPE // 2, 1), pltpu.roll(x, ROPE // 2, 1))
    return x * cos + partner * sin_signed


def _in_proj_kernel(x_ref, gmix_ref, win_ref, gq_ref, gkv_ref, wuq_ref, bf_ref, cos_ref, sin_ref,
                    qa_ref, ka_ref, va_ref, kab_ref, vab_ref, logf_ref, qmla_ref, ckv_ref, kr_ref, u_ref):
    h = _rms(x_ref[...], gmix_ref[...]).astype(BF16)

    def proj(c0, width):
        return jnp.dot(h, win_ref[:, c0:c0 + width], preferred_element_type=F32)

    qa_ref[...] = proj(_C_QA, W_A).astype(BF16)
    ka = proj(_C_KA, W_A)
    ka_ref[...] = ka
    kab_ref[...] = ka.astype(BF16)
    va = proj(_C_VA, W_A)
    va_ref[...] = va
    vab_ref[...] = va.astype(BF16)

    cos = cos_ref[...]
    sin = sin_ref[...]
    krf = proj(_C_KRF, LANES)
    kr_ref[...] = _rope_block(krf, cos, sin)
    z = krf + bf_ref[...]
    logsig = jnp.minimum(z, 0.0) - jnp.log(1.0 + jnp.exp(-jnp.abs(z)))
    logf_ref[...] = logsig[:, _FA_LANE:_FA_LANE + H_A]

    ckv_ref[...] = _rms(proj(_C_CKV, KV_RANK), gkv_ref[...])

    u_ref[...] = proj(_C_UA, C_C) * jax.nn.sigmoid(proj(_C_UG, C_C))

    cqn = _rms(proj(_C_CQ, Q_RANK), gq_ref[...]).astype(BF16)
    scale = (NOPE + ROPE) ** -0.5
    for hd in range(H_B):
        q = jnp.dot(cqn, wuq_ref[:, hd * MLA_SLAB:(hd + 1) * MLA_SLAB], preferred_element_type=F32)
        qmla_ref[:, hd * MLA_SLAB:hd * MLA_SLAB + NOPE] = (q[:, :NOPE] * scale).astype(BF16)
        qmla_ref[:, hd * MLA_SLAB + NOPE:(hd + 1) * MLA_SLAB] = (
            _rope_block(q[:, NOPE:], cos, sin) * scale).astype(BF16)


def _in_proj(x, g_mix, w_in_p, g_q, g_kv, w_uq_p, b_f128, cos, sin):
    n = x.shape[0]
    tm = ROW_TILE
    row = lambda w: pl.BlockSpec((tm, w), lambda i: (i, 0))
    out_shapes = (
        jax.ShapeDtypeStruct((n, W_A), BF16),
        jax.ShapeDtypeStruct((n, W_A), F32),
        jax.ShapeDtypeStruct((n, W_A), F32),
        jax.ShapeDtypeStruct((n, W_A), BF16),
        jax.ShapeDtypeStruct((n, W_A), BF16),
        jax.ShapeDtypeStruct((n, H_A), F32),
        jax.ShapeDtypeStruct((n, H_B * MLA_SLAB), BF16),
        jax.ShapeDtypeStruct((n, KV_RANK), F32),
        jax.ShapeDtypeStruct((n, LANES), F32),
        jax.ShapeDtypeStruct((n, C_C), F32),
    )
    return pl.pallas_call(
        _in_proj_kernel,
        grid=(n // tm,),
        in_specs=[row(D_MODEL), _const_spec((1, D_MODEL)), _const_spec((D_MODEL, IN_COLS)),
                  _const_spec((1, Q_RANK)), _const_spec((1, KV_RANK)),
                  _const_spec((Q_RANK, H_B * MLA_SLAB)), _const_spec((1, LANES)), row(LANES), row(LANES)],
        out_specs=[row(W_A), row(W_A), row(W_A), row(W_A), row(W_A), row(H_A), row(H_B * MLA_SLAB),
                   row(KV_RANK), row(LANES), row(C_C)],
        out_shape=out_shapes,
        compiler_params=_cparams("parallel"),
        name="in_proj",
    )(x, g_mix, w_in_p, g_q, g_kv, w_uq_p, b_f128, cos, sin)


def _cumsum_kernel(x_ref, o_ref):
    c = x_ref[...]
    t = c.shape[-1]
    lane = lax.broadcasted_iota(jnp.int32, c.shape, 1)
    s = 1
    while s < t:
        c = c + jnp.where(lane >= s, pltpu.roll(c, s, 1), 0.0)
        s *= 2
    o_ref[...] = c


def _cumsum_time(logf_t):
    b, hh, t = logf_t.shape
    spec = pl.BlockSpec((None, hh, t), lambda i: (i, 0, 0))
    return pl.pallas_call(
        _cumsum_kernel, grid=(b,), in_specs=[spec], out_specs=spec,
        out_shape=jax.ShapeDtypeStruct((b, hh, t), F32),
        compiler_params=_cparams("parallel"), name="cumsum_logf",
    )(logf_t)


def _softmax_step(s, v, row_bias, m_ref, l_ref, acc_ref):
    width = s.shape[1]
    s_max = jnp.max(s, axis=-1, keepdims=True)
    m_old = m_ref[...]
    m_new = jnp.maximum(m_old, s_max if row_bias is None else s_max + row_bias)
    alpha = jnp.exp(m_old - m_new)
    offset = m_new if row_bias is None else m_new - row_bias
    if width % LANES == 0:
        p = [jnp.exp(s[:, c * LANES:(c + 1) * LANES] - offset) for c in range(width // LANES)]
        l_add = functools.reduce(lambda a, b: a + b, p)
        p = p[0].astype(BF16) if len(p) == 1 else jnp.concatenate([x.astype(BF16) for x in p], axis=1)
    else:
        p = jnp.exp(s - offset[:, :1])
        lane = lax.broadcasted_iota(jnp.int32, m_old.shape, 1)
        l_add = jnp.where(lane == 0, jnp.sum(p, axis=-1, keepdims=True), 0.0)
        p = p.astype(BF16)
    l_ref[...] = alpha * l_ref[...] + l_add
    acc_ref[...] = alpha * acc_ref[...] + jnp.dot(p, v, preferred_element_type=F32)
    m_ref[...] = m_new


def _flash_sweep(heads, n_open, d0, tq, tk, diag_mask, m_ref, l_ref, acc_ref):
    for h in range(len(heads)):
        m_ref[h] = jnp.full(m_ref.shape[1:], NEG_INF, F32)
        l_ref[h] = jnp.zeros(l_ref.shape[1:], F32)
        acc_ref[h] = jnp.zeros(acc_ref.shape[1:], F32)

    def chunk(k0, width, mask):
        for h, (q, key_chunk, value_chunk, col_bias, row_bias) in enumerate(heads):
            s = _qk(q, key_chunk(k0, width))
            if col_bias is not None:
                s = s + col_bias(k0, width)
            if mask is not None:
                s = jnp.where(mask, s, NEG_INF)
            _softmax_step(s, value_chunk(k0, width), row_bias, m_ref.at[h], l_ref.at[h], acc_ref.at[h])

    def open_chunk(i, carry):
        chunk(pl.multiple_of(i * tk, tk), tk, None)
        return carry

    lax.fori_loop(0, n_open, open_chunk, 0)
    chunk(d0, tq, diag_mask)
    return [acc_ref[h] / jnp.sum(l_ref[h], axis=-1, keepdims=True) for h in range(len(heads))]


def _qk(q, k):
    return lax.dot_general(q, k, (((1,), (1,)), ((), ())), preferred_element_type=F32)


def _tile_extent(n_past, tq, tk, single_tile):
    if single_tile:
        return n_past // tk, n_past
    start = n_past + pl.program_id(2) * tq
    return start // tk, pl.multiple_of(start, tq)


def _fox_kernel(q_ref, k_ref, v_ref, cq_ref, ck_ref, o_ref, m_ref, l_ref, acc_ref, *, tq, tk, n_past, single_tile):
    lane = lax.broadcasted_iota(jnp.int32, (tq, LANES), 1)
    n_open, d0 = _tile_extent(n_past, tq, tk, single_tile)
    row = lax.broadcasted_iota(jnp.int32, (tq, tq), 0)
    col = lax.broadcasted_iota(jnp.int32, (tq, tq), 1)
    heads = []
    for pair in range(FOX_PAIRS_PER_STEP):
        lanes = slice(pair * LANES, (pair + 1) * LANES)
        q = q_ref[:, lanes]
        key_chunk = functools.partial(lambda k0, width, lanes: k_ref[pl.ds(k0, width), lanes], lanes=lanes)
        value_chunk = functools.partial(lambda k0, width, lanes: v_ref[pl.ds(k0, width), lanes], lanes=lanes)
        for j in range(2):
            qj = jnp.where((lane >= j * D_HA) & (lane < (j + 1) * D_HA), q, jnp.zeros_like(q))
            col_bias = functools.partial(
                lambda k0, width, pair, j: -ck_ref[pair, j:j + 1, pl.ds(k0, width)], pair=pair, j=j)
            row_bias = jnp.broadcast_to(cq_ref[pair, :, j:j + 1], (tq, LANES))
            heads.append((qj, key_chunk, value_chunk, col_bias, row_bias))
    outs = _flash_sweep(heads, n_open, d0, tq, tk, col <= row, m_ref, l_ref, acc_ref)
    for pair in range(FOX_PAIRS_PER_STEP):
        o_ref[:, pair * LANES:(pair + 1) * LANES] = jnp.where(lane < D_HA, outs[2 * pair], outs[2 * pair + 1])


def _fox_attention(q, k, v, c_q, c_k, *, n_past, tq, tk):
    b, t_q, _ = q.shape
    t_k = k.shape[1]
    t_kp = c_k.shape[-1]
    assert t_q % tq == 0 and n_past % tk == 0 and (t_q == tq or tq % tk == 0)
    npair = FOX_PAIRS_PER_STEP
    kern = functools.partial(_fox_kernel, tq=tq, tk=tk, n_past=n_past, single_tile=t_q == tq)
    return pl.pallas_call(
        kern,
        grid=(b, H_A // (2 * npair), t_q // tq),
        in_specs=[pl.BlockSpec((None, tq, npair * LANES), lambda bi, hp, qi: (bi, qi, hp)),
                  pl.BlockSpec((None, t_k, npair * LANES), lambda bi, hp, qi: (bi, 0, hp)),
                  pl.BlockSpec((None, t_k, npair * LANES), lambda bi, hp, qi: (bi, 0, hp)),
                  pl.BlockSpec((None, npair, tq, 2), lambda bi, hp, qi: (bi, hp, qi, 0)),
                  pl.BlockSpec((None, npair, 2, t_kp), lambda bi, hp, qi: (bi, hp, 0, 0))],
        out_specs=pl.BlockSpec((None, tq, npair * LANES), lambda bi, hp, qi: (bi, qi, hp)),
        out_shape=jax.ShapeDtypeStruct((b, t_q, W_A), F32),
        scratch_shapes=[pltpu.VMEM((2 * npair, tq, LANES), F32)] * 3,
        compiler_params=_cparams("parallel", "parallel", "arbitrary"),
        name="fox_attention",
    )(q, k, v, c_q, c_k)


def _mla_kernel(q_ref, k_ref, v_ref, o_ref, m_ref, l_ref, acc_ref, *, tq, tk, n_past, single_tile):
    n_open, d0 = _tile_extent(n_past, tq, tk, single_tile)
    row = lax.broadcasted_iota(jnp.int32, (tq, tq), 0)
    col = lax.broadcasted_iota(jnp.int32, (tq, tq), 1)
    heads = []
    for j in range(MLA_HEADS_PER_STEP):
        key_chunk = functools.partial(
            lambda k0, width, j: k_ref[pl.ds(k0, width), j * MLA_SLAB:(j + 1) * MLA_SLAB], j=j)
        value_chunk = functools.partial(lambda k0, width, j: v_ref[pl.ds(k0, width), j * V_HD:(j + 1) * V_HD], j=j)
        heads.append((q_ref[:, j * MLA_SLAB:(j + 1) * MLA_SLAB], key_chunk, value_chunk, None, None))
    mask = (d0 + col) >> _CHUNK_SHIFT <= (d0 + row) >> _CHUNK_SHIFT
    outs = _flash_sweep(heads, n_open, d0, tq, tk, mask, m_ref, l_ref, acc_ref)
    for j in range(MLA_HEADS_PER_STEP):
        o_ref[:, j * V_HD:(j + 1) * V_HD] = outs[j]


def _mla_attention(q, k, v, *, n_past, tq, tk):
    b, t_q, _ = q.shape
    t_k = k.shape[1]
    assert t_q % tq == 0 and n_past % tk == 0 and (t_q == tq or tq % tk == 0)
    nh = MLA_HEADS_PER_STEP
    kern = functools.partial(_mla_kernel, tq=tq, tk=tk, n_past=n_past, single_tile=t_q == tq)
    return pl.pallas_call(
        kern,
        grid=(b, H_B // nh, t_q // tq),
        in_specs=[pl.BlockSpec((None, tq, nh * MLA_SLAB), lambda bi, hd, qi: (bi, qi, hd)),
                  pl.BlockSpec((None, t_k, nh * MLA_SLAB), lambda bi, hd, qi: (bi, 0, hd)),
                  pl.BlockSpec((None, t_k, nh * V_HD), lambda bi, hd, qi: (bi, 0, hd))],
        out_specs=pl.BlockSpec((None, tq, nh * V_HD), lambda bi, hd, qi: (bi, qi, hd)),
        out_shape=jax.ShapeDtypeStruct((b, t_q, W_B), F32),
        scratch_shapes=[pltpu.VMEM((nh, tq, LANES), F32), pltpu.VMEM((nh, tq, LANES), F32),
                        pltpu.VMEM((nh, tq, V_HD), F32)],
        compiler_params=_cparams("parallel", "parallel", "arbitrary"),
        name="mla_attention",
    )(q, k, v)


def _kv_up_kernel(ckv_ref, kr_ref, wuk_ref, wuv_ref, k_ref, v_ref):
    c = ckv_ref[...].astype(BF16)
    kr = kr_ref[...].astype(BF16)
    kn = jnp.dot(c, wuk_ref[...], preferred_element_type=F32).astype(BF16)
    for hd in range(H_B):
        k_ref[:, hd * MLA_SLAB:hd * MLA_SLAB + NOPE] = kn[:, hd * NOPE:(hd + 1) * NOPE]
        k_ref[:, hd * MLA_SLAB + NOPE:(hd + 1) * MLA_SLAB] = kr
    v_ref[...] = jnp.dot(c, wuv_ref[...], preferred_element_type=F32).astype(BF16)


def _kv_up(ckv_n, krope128, w_uk, w_uv):
    r = ckv_n.shape[0]
    tm = ROW_TILE
    row = lambda w: pl.BlockSpec((tm, w), lambda i: (i, 0))
    return pl.pallas_call(
        _kv_up_kernel, grid=(r // tm,),
        in_specs=[row(KV_RANK), row(LANES), _const_spec((KV_RANK, H_B * NOPE)),
                  _const_spec((KV_RANK, H_B * V_HD))],
        out_specs=[row(H_B * MLA_SLAB), row(H_B * V_HD)],
        out_shape=(jax.ShapeDtypeStruct((r, H_B * MLA_SLAB), BF16), jax.ShapeDtypeStruct((r, H_B * V_HD), BF16)),
        compiler_params=_cparams("parallel"), name="mla_kv_up",
    )(ckv_n, krope128, w_uk, w_uv)


_CONV_SUB = 64
_CONV_HALO = -(-CONV_W // SUBLANES) * SUBLANES


def _conv_kernel(xp_ref, w_ref, b_ref, g_ref, beta_ref, o_ref, *, tt):
    t0 = pl.program_id(1) * tt
    w = w_ref[...]
    rows = min(_CONV_SUB, tt)
    for sub in range(tt // rows):
        base = pl.multiple_of(t0 + sub * rows, rows)
        xa = xp_ref[pl.ds(base, rows + _CONV_HALO), :]
        acc = None
        for b in range(SUBLANES):
            z = None
            for a in range(-(-CONV_W // SUBLANES)):
                tap = SUBLANES * a + b
                if tap < CONV_W:
                    term = xa[SUBLANES * a:SUBLANES * a + rows + SUBLANES, :] * w[tap:tap + 1, :]
                    z = term if z is None else z + term
            acc = z[b:b + rows, :] if acc is None else acc + z[b:b + rows, :]
        y = acc + b_ref[...]
        mu = jnp.mean(y, axis=-1, keepdims=True)
        yc = y - mu
        var = jnp.mean(yc * yc, axis=-1, keepdims=True)
        y = yc * lax.rsqrt(var + EPS) * g_ref[...] + beta_ref[...]
        o_ref[sub * rows:(sub + 1) * rows, :] = y * jax.nn.sigmoid(y)


def _conv_module(xp, w_dw, b_dw, ln_g, ln_b, *, t, tt):
    b, t_p, _ = xp.shape
    kern = functools.partial(_conv_kernel, tt=tt)
    return pl.pallas_call(
        kern, grid=(b, t // tt),
        in_specs=[pl.BlockSpec((None, t_p, C_C), lambda bi, ti: (bi, 0, 0)),
                  _const_spec((CONV_W, C_C)), _const_spec((1, C_C)), _const_spec((1, C_C)), _const_spec((1, C_C))],
        out_specs=pl.BlockSpec((None, tt, C_C), lambda bi, ti: (bi, ti, 0)),
        out_shape=jax.ShapeDtypeStruct((b, t, C_C), F32),
        compiler_params=_cparams("parallel", "arbitrary"), name="conv_module",
    )(xp, w_dw, b_dw, ln_g, ln_b)


def _out_proj_kernel(oa_ref, ob_ref, oc_ref, x_ref, gout_ref, wout_ref, gffn_ref, wr_ref, br_ref,
                     xn_ref, h2_ref, rl_ref):
    g = gout_ref[...]
    na = _rms(oa_ref[...], g[:, :W_A]).astype(BF16)
    nb = _rms(ob_ref[...], g[:, W_A:W_A + W_B]).astype(BF16)
    nc = _rms(oc_ref[...], g[:, W_A + W_B:]).astype(BF16)
    mix = jnp.dot(na, wout_ref[:W_A, :], preferred_element_type=F32)
    mix = mix + jnp.dot(nb, wout_ref[W_A:W_A + W_B, :], preferred_element_type=F32)
    mix = mix + jnp.dot(nc, wout_ref[W_A + W_B:, :], preferred_element_type=F32)
    xn = x_ref[...] + mix
    xn_ref[...] = xn
    h2 = _rms(xn, gffn_ref[...]).astype(BF16)
    rl_ref[...] = jnp.dot(h2, wr_ref[...], preferred_element_type=F32) + br_ref[...]
    bits = pltpu.bitcast(h2.astype(F32), jnp.uint32)
    h2_ref[...] = (bits[:, :D_MODEL // 2] >> 16) | (bits[:, D_MODEL // 2:] & jnp.uint32(0xFFFF0000))


def _out_proj(o_a, o_b, o_c, x, g_out, w_out, g_ffn, w_r, b_r):
    n = x.shape[0]
    tm = ROW_TILE
    row = lambda w: pl.BlockSpec((tm, w), lambda i: (i, 0))
    return pl.pallas_call(
        _out_proj_kernel, grid=(n // tm,),
        in_specs=[row(W_A), row(W_B), row(C_C), row(D_MODEL), _const_spec((1, D_MODEL)),
                  _const_spec((D_MODEL, D_MODEL)), _const_spec((1, D_MODEL)),
                  _const_spec((D_MODEL, LANES)), _const_spec((1, LANES))],
        out_specs=[row(D_MODEL), row(D_MODEL // 2), row(LANES)],
        out_shape=(jax.ShapeDtypeStruct((n, D_MODEL), F32), jax.ShapeDtypeStruct((n, D_MODEL // 2), jnp.uint32),
                   jax.ShapeDtypeStruct((n, LANES), F32)),
        compiler_params=_cparams("parallel"), name="out_proj_router",
    )(o_a, o_b, o_c, x, g_out, w_out, g_ffn, w_r, b_r)


def _expert_kernel(be_ref, nv_ref, sa_ref, h_hbm, wg_ref, wu_ref, wd_ref, y_hbm,
                   xbuf, ybuf, wgb_ref, wub_ref, wdb_ref, gsem, ssem, *, n_tok, n_blocks):
    te = EXPERT_TILE
    half = D_MODEL // 2
    i = pl.program_id(0)
    nv = nv_ref[0]
    cur = i % 2
    nxt = 1 - cur
    valid = i < nv

    def gather_block(block, buf):
        for r in range(te):
            tok = jnp.minimum(sa_ref[block * te + r] >> 1, n_tok - 1)
            pltpu.make_async_copy(h_hbm.at[pl.ds(tok, 1)], xbuf.at[buf, pl.ds(r, 1)], gsem.at[buf]).start()

    def scatter_block(block, buf):
        for r in range(te):
            dst = sa_ref[block * te + r]
            pltpu.make_async_copy(ybuf.at[buf, pl.ds(r, 1)], y_hbm.at[pl.ds(dst, 1)], ssem.at[buf]).start()

    def wait_gather(buf):
        pltpu.make_async_copy(h_hbm.at[pl.ds(0, te)], xbuf.at[buf], gsem.at[buf]).wait()

    def wait_scatter(buf):
        pltpu.make_async_copy(ybuf.at[buf], y_hbm.at[pl.ds(0, te)], ssem.at[buf]).wait()

    @pl.when(i == 0)
    def _():
        gather_block(0, 0)
        ybuf[1] = jnp.zeros(ybuf.shape[1:], F32)
        fills = [pltpu.make_async_copy(ybuf.at[1], y_hbm.at[pl.ds(TOP_K * n_tok + k * te, te)], ssem.at[1])
                 for k in range(EXPERT_DUMP_ROWS // te)]
        for fill in fills:
            fill.start()
        for fill in fills:
            fill.wait()

    prev_e = be_ref[jnp.maximum(i - 1, 0)]

    @pl.when(valid & ((i == 0) | (be_ref[i] != prev_e)))
    def _():
        wgb_ref[...] = wg_ref[...].astype(BF16)
        wub_ref[...] = wu_ref[...].astype(BF16)
        wdb_ref[...] = wd_ref[...].astype(BF16)

    @pl.when(valid)
    def _():
        wait_gather(cur)

    @pl.when(valid & (i >= 2))
    def _():
        wait_scatter(cur)

    def step(scatter_previous):
        gather_block(jnp.minimum(i + 1, n_blocks - 1), nxt)
        if scatter_previous:
            scatter_block(i - 1, nxt)
        u = xbuf[cur]
        x_lo = pltpu.bitcast(u << 16, F32).astype(BF16)
        x_hi = pltpu.bitcast(u & jnp.uint32(0xFFFF0000), F32).astype(BF16)
        gate = (jnp.dot(x_lo, wgb_ref[:half, :], preferred_element_type=F32)
                + jnp.dot(x_hi, wgb_ref[half:, :], preferred_element_type=F32))
        up = (jnp.dot(x_lo, wub_ref[:half, :], preferred_element_type=F32)
              + jnp.dot(x_hi, wub_ref[half:, :], preferred_element_type=F32))
        hid = (gate * jax.nn.sigmoid(gate) * up).astype(BF16)
        ybuf[cur] = jnp.dot(hid, wdb_ref[...], preferred_element_type=F32)

    @pl.when(valid & (i == 0))
    def _():
        step(False)

    @pl.when(valid & (i > 0))
    def _():
        step(True)

    @pl.when(i == nv - 1)
    def _():
        scatter_block(i, cur)
        wait_scatter(cur)
        wait_gather(nxt)

    @pl.when((i == nv - 1) & (i > 0))
    def _():
        wait_scatter(nxt)


def _expert_blocks(block_e, n_valid, slot_a, h2_bits, w_gate, w_up, w_down, *, layer):
    n_tok = h2_bits.shape[0]
    te = EXPERT_TILE
    n_blocks = block_e.shape[0]
    weights = lambda r, c: pl.BlockSpec((None, None, r, c), lambda i, be, nv, sa: (layer, be[i], 0, 0))
    grid_spec = pltpu.PrefetchScalarGridSpec(
        num_scalar_prefetch=3, grid=(n_blocks,),
        in_specs=[pl.BlockSpec(memory_space=pl.ANY), weights(D_MODEL, D_EXPERT), weights(D_MODEL, D_EXPERT),
                  weights(D_EXPERT, D_MODEL)],
        out_specs=pl.BlockSpec(memory_space=pl.ANY),
        scratch_shapes=[pltpu.VMEM((2, te, D_MODEL // 2), jnp.uint32), pltpu.VMEM((2, te, D_MODEL), F32),
                        pltpu.VMEM((D_MODEL, D_EXPERT), BF16), pltpu.VMEM((D_MODEL, D_EXPERT), BF16),
                        pltpu.VMEM((D_EXPERT, D_MODEL), BF16),
                        pltpu.SemaphoreType.DMA((2,)), pltpu.SemaphoreType.DMA((2,))])
    kern = functools.partial(_expert_kernel, n_tok=n_tok, n_blocks=n_blocks)
    return pl.pallas_call(
        kern, grid_spec=grid_spec,
        out_shape=jax.ShapeDtypeStruct((TOP_K * n_tok + EXPERT_DUMP_ROWS, D_MODEL), F32),
        compiler_params=_cparams("arbitrary"), name="expert_blocks",
    )(block_e, n_valid, slot_a, h2_bits, w_gate, w_up, w_down)


def _route(rlog):
    n = rlog.shape[0]
    idx = jnp.arange(n)
    g_logits = rlog[:, :N_GROUPS]
    g_sel = jnp.argmax(g_logits, axis=-1)
    g_w = jax.nn.softmax(g_logits, axis=-1)[idx, g_sel]
    e_logits = rlog[:, N_GROUPS:N_GROUPS + N_EXPERTS].reshape(n, N_GROUPS, EXPERTS_PER_GROUP)[idx, g_sel]
    top_v, top_i = lax.top_k(e_logits, TOP_K)
    gate = jax.nn.softmax(top_v, axis=-1) * g_w[:, None]
    expert_ids = (g_sel[:, None] * EXPERTS_PER_GROUP + top_i).astype(jnp.int32)
    return expert_ids, gate


def _dispatch_plan(expert_ids):
    n = expert_ids.shape[0]
    te = EXPERT_TILE
    a = n * TOP_K
    n_blocks = -(-a // te) + N_EXPERTS
    rows = n_blocks * te
    flat_e = expert_ids.reshape(-1)
    experts = jnp.arange(N_EXPERTS, dtype=jnp.int32)
    counts = jnp.sum((flat_e[:, None] == experts[None, :]).astype(jnp.int32), axis=0)
    padded = (counts + te - 1) // te * te
    pad_end = jnp.cumsum(padded)
    pad_start = pad_end - padded
    start = jnp.cumsum(counts) - counts
    order = jnp.argsort(flat_e).astype(jnp.int32)
    slot = jnp.arange(rows, dtype=jnp.int32)
    slot_e = jnp.minimum(jnp.sum((pad_end[None, :] <= slot[:, None]).astype(jnp.int32), axis=-1), N_EXPERTS - 1)
    slot_hot = (slot_e[:, None] == experts[None, :]).astype(jnp.int32)
    slot_rank = slot - jnp.sum(slot_hot * pad_start[None, :], axis=-1)
    filled = slot_rank < jnp.sum(slot_hot * counts[None, :], axis=-1)
    src = jnp.where(filled, jnp.sum(slot_hot * start[None, :], axis=-1) + slot_rank, 0)
    slot_a = jnp.where(filled, order[src], a + slot % EXPERT_DUMP_ROWS)
    block_e = slot_e.reshape(n_blocks, te)[:, 0]
    n_valid = (pad_end[-1] // te).astype(jnp.int32).reshape(1)
    return slot_a, block_e, n_valid


def _combine_kernel(x_ref, y_ref, gate_ref, gfin_ref, o_ref, *, final):
    y = y_ref[...]
    gate = gate_ref[...]
    x = x_ref[...] + (y[:, :D_MODEL] * gate[:, 0:1] + y[:, D_MODEL:] * gate[:, 1:2])
    o_ref[...] = _rms(x, gfin_ref[...]) if final else x


def _combine(x, y_pairs, gate, g_final, *, row0, final):
    n = x.shape[0]
    tm = ROW_TILE
    off = row0 // tm
    assert row0 % tm == 0
    row = pl.BlockSpec((tm, D_MODEL), lambda i: (i, 0))
    return pl.pallas_call(
        functools.partial(_combine_kernel, final=final), grid=(n // tm,),
        in_specs=[row, pl.BlockSpec((tm, TOP_K * D_MODEL), lambda i: (i + off, 0)),
                  pl.BlockSpec((tm, TOP_K), lambda i: (i, 0)), _const_spec((1, D_MODEL))],
        out_specs=row, out_shape=jax.ShapeDtypeStruct((n, D_MODEL), F32),
        compiler_params=_cparams("parallel"), name="moe_combine",
    )(x, y_pairs, gate, g_final)


def _rope_tables(pos):
    half = ROPE // 2
    inv = ROPE_THETA ** (-jnp.arange(half, dtype=F32) / half)
    ang = pos.astype(F32)[:, None] * inv[None, :]
    zeros = jnp.zeros((pos.shape[0], LANES - ROPE), F32)
    cos = jnp.concatenate([jnp.cos(ang), jnp.cos(ang), zeros], axis=-1)
    sin = jnp.concatenate([-jnp.sin(ang), jnp.sin(ang), zeros], axis=-1)
    return cos, sin


def _prep_w_in(w_in):
    qa, ka, va, fa, cq, ckv, kr, ua, ug = jnp.split(
        w_in, np.cumsum([W_A, W_A, W_A, H_A, Q_RANK, KV_RANK, ROPE, C_C])[:].tolist(), axis=1)
    pad = jnp.zeros((D_MODEL, LANES - ROPE - H_A), w_in.dtype)
    return jnp.concatenate([qa * (D_HA ** -0.5), ka, va, cq, ckv, ua, ug, kr, fa, pad], axis=1).astype(BF16)


def _prep_w_uq(w_uq):
    w = w_uq.reshape(Q_RANK, H_B, NOPE + ROPE)
    w = jnp.pad(w, ((0, 0), (0, 0), (0, MLA_SLAB - NOPE - ROPE)))
    return w.reshape(Q_RANK, H_B * MLA_SLAB).astype(BF16)


def _prep_w_ukv(w_ukv):
    w = w_ukv.reshape(KV_RANK, H_B, NOPE + V_HD)
    return (w[:, :, :NOPE].reshape(KV_RANK, H_B * NOPE).astype(BF16),
            w[:, :, NOPE:].reshape(KV_RANK, H_B * V_HD).astype(BF16))


def _pad_lanes(x, width):
    return jnp.pad(x, ((0, 0),) * (x.ndim - 1) + ((0, width - x.shape[-1]),))


def _round_up(x, m):
    return -(-x // m) * m


def kernel(x_prompt, x_sample, cache_fox_k, cache_fox_v, cache_fox_logf, cache_mla_ckv, cache_mla_krope, state_conv, g_mix, w_in, b_f, g_q, w_uq, g_kv, w_ukv, w_dw, b_dw, ln_g, ln_b, g_out, w_out, g_ffn, w_rg, b_rg, w_re, b_re, w_gate, w_up, w_down, g_final):
    bp, tp, _ = x_prompt.shape
    bs, ts, _ = x_sample.shape
    n_past = cache_fox_k.shape[2]
    n_p, n_s = bp * tp, bs * ts
    depth = g_mix.shape[0]

    halo = CONV_W - 1
    tkp_s = _round_up(n_past + ts, LANES)

    x_p = x_prompt.reshape(n_p, D_MODEL)
    x_s = x_sample.reshape(n_s, D_MODEL)
    cos_p, sin_p = _rope_tables(jnp.arange(tp))
    cos_s, sin_s = _rope_tables(n_past + jnp.arange(ts))
    rope_p = (jnp.tile(cos_p, (bp, 1)), jnp.tile(sin_p, (bp, 1)))
    rope_s = (jnp.tile(cos_s, (bs, 1)), jnp.tile(sin_s, (bs, 1)))

    def split_heads(c):
        return c.reshape(c.shape[0], H_A // 2, 2, c.shape[-1])

    states_p, states_s = [], []
    for l in range(depth):
        w_uk, w_uv = _prep_w_ukv(w_ukv[l])
        b_f128 = jnp.pad(b_f[l], (_FA_LANE, LANES - _FA_LANE - H_A))[None, :]
        in_w = (g_mix[l][None], _prep_w_in(w_in[l]), g_q[l][None], g_kv[l][None], _prep_w_uq(w_uq[l]), b_f128)
        qa_p, ka_p, va_p, kab_p, vab_p, lf_p, qm_p, ckv_p, kr_p, u_p = _in_proj(x_p, *in_w, *rope_p)
        qa_s, ka_s, va_s, kab_s, vab_s, lf_s, qm_s, ckv_s, kr_s, u_s = _in_proj(x_s, *in_w, *rope_s)

        lf_p = lf_p.reshape(bp, tp, H_A)
        lf_s = lf_s.reshape(bs, ts, H_A)
        c_p = _cumsum_time(lf_p.transpose(0, 2, 1))
        lf_all = jnp.concatenate([cache_fox_logf[l].astype(F32), lf_s], axis=1)
        c_s = _cumsum_time(_pad_lanes(lf_all.transpose(0, 2, 1), tkp_s))
        o_a_p = _fox_attention(
            qa_p.reshape(bp, tp, W_A), kab_p.reshape(bp, tp, W_A), vab_p.reshape(bp, tp, W_A),
            split_heads(c_p).transpose(0, 1, 3, 2), split_heads(c_p), n_past=0, tq=TQ_PROMPT, tk=TQ_PROMPT)
        k_all = jnp.concatenate([cache_fox_k[l].reshape(bs, n_past, W_A).astype(BF16),
                                 kab_s.reshape(bs, ts, W_A)], axis=1)
        v_all = jnp.concatenate([cache_fox_v[l].reshape(bs, n_past, W_A).astype(BF16),
                                 vab_s.reshape(bs, ts, W_A)], axis=1)
        o_a_s = _fox_attention(
            qa_s.reshape(bs, ts, W_A), k_all, v_all,
            split_heads(c_s[:, :, n_past:n_past + ts]).transpose(0, 1, 3, 2), split_heads(c_s),
            n_past=n_past, tq=ts, tk=TK_SAMPLE)

        ckv_s = ckv_s.reshape(bs, ts, KV_RANK)
        kr_s = kr_s.reshape(bs, ts, LANES)
        ckv_all = jnp.concatenate([cache_mla_ckv[l].astype(F32), ckv_s], axis=1)
        kr_all = jnp.concatenate([_pad_lanes(cache_mla_krope[l].astype(F32), LANES), kr_s], axis=1)
        km_p, vm_p = _kv_up(ckv_p, kr_p, w_uk, w_uv)
        km_s, vm_s = _kv_up(ckv_all.reshape(-1, KV_RANK), kr_all.reshape(-1, LANES), w_uk, w_uv)
        o_b_p = _mla_attention(
            qm_p.reshape(bp, tp, H_B * MLA_SLAB), km_p.reshape(bp, tp, H_B * MLA_SLAB), vm_p.reshape(bp, tp, W_B),
            n_past=0, tq=TQ_PROMPT, tk=TQ_PROMPT)
        o_b_s = _mla_attention(
            qm_s.reshape(bs, ts, H_B * MLA_SLAB), km_s.reshape(bs, n_past + ts, H_B * MLA_SLAB),
            vm_s.reshape(bs, n_past + ts, W_B), n_past=n_past, tq=ts, tk=TK_SAMPLE)

        xp_p = jnp.pad(u_p.reshape(bp, tp, C_C), ((0, 0), (halo, _CONV_HALO - halo), (0, 0)))
        xp_s = jnp.concatenate([state_conv[l].astype(F32), u_s.reshape(bs, ts, C_C)], axis=1)
        conv_p = xp_p[:, tp:tp + halo]
        conv_s = xp_s[:, ts:ts + halo]
        xp_s = jnp.pad(xp_s, ((0, 0), (0, _CONV_HALO - halo), (0, 0)))
        conv_w = (w_dw[l], b_dw[l][None], ln_g[l][None], ln_b[l][None])
        o_c_p = _conv_module(xp_p, *conv_w, t=tp, tt=CONV_TILE)
        o_c_s = _conv_module(xp_s, *conv_w, t=ts, tt=ts)

        w_r = _pad_lanes(jnp.concatenate([w_rg[l], w_re[l]], axis=1), LANES).astype(BF16)
        b_r = _pad_lanes(jnp.concatenate([b_rg[l], b_re[l].reshape(-1)])[None, :].astype(F32), LANES)
        out_w = (g_out[l][None], w_out[l].astype(BF16), g_ffn[l][None], w_r, b_r)
        x_p, h2_p, rlog_p = _out_proj(o_a_p.reshape(n_p, W_A), o_b_p.reshape(n_p, W_B), o_c_p.reshape(n_p, C_C),
                                      x_p, *out_w)
        x_s, h2_s, rlog_s = _out_proj(o_a_s.reshape(n_s, W_A), o_b_s.reshape(n_s, W_B), o_c_s.reshape(n_s, C_C),
                                      x_s, *out_w)

        expert_ids, gate = _route(jnp.concatenate([rlog_p, rlog_s], axis=0))
        slot_a, block_e, n_valid = _dispatch_plan(expert_ids)
        ys = _expert_blocks(block_e, n_valid, slot_a, jnp.concatenate([h2_p, h2_s], axis=0),
                            w_gate, w_up, w_down, layer=l)
        y_pairs = ys.reshape(-1, TOP_K * D_MODEL)
        final = l == depth - 1
        x_p = _combine(x_p, y_pairs, gate[:n_p], g_final[None], row0=0, final=final)
        x_s = _combine(x_s, y_pairs, gate[n_p:], g_final[None], row0=n_p, final=final)

        states_p.append((ka_p.reshape(bp, tp, H_A, D_HA), va_p.reshape(bp, tp, H_A, D_HA), lf_p,
                         ckv_p.reshape(bp, tp, KV_RANK), kr_p[:, :ROPE].reshape(bp, tp, ROPE), conv_p))
        states_s.append((ka_s.reshape(bs, ts, H_A, D_HA), va_s.reshape(bs, ts, H_A, D_HA), lf_s,
                         ckv_s, kr_s[:, :, :ROPE], conv_s))

    y_p = x_p.reshape(bp, tp, D_MODEL)
    y_s = x_s.reshape(bs, ts, D_MODEL)
    p_out = [jnp.stack(a) for a in zip(*states_p)]
    s_out = [jnp.stack(a) for a in zip(*states_s)]
    return (y_p, y_s, *p_out, *s_out)
```

```python
import functools

import numpy as np
import jax
import jax.numpy as jnp
from jax import lax
from jax.experimental import pallas as pl
from jax.experimental.pallas import tpu as pltpu

F32 = jnp.float32
BF16 = jnp.bfloat16

D_MODEL = 2048
DEPTH = 4
CHUNK = 64
H_A, D_HA = 8, 64
W_A = H_A * D_HA
H_B, NOPE, ROPE, V_HD = 8, 128, 64, 128
Q_RANK, KV_RANK = 512, 256
W_B = H_B * V_HD
C_C = D_MODEL - W_A - W_B
CONV_W = 31
_CHUNK_SHIFT = CHUNK.bit_length() - 1
assert 1 << _CHUNK_SHIFT == CHUNK
ROPE_THETA = 10000.0
N_GROUPS, EXPERTS_PER_GROUP = 4, 8
N_EXPERTS = N_GROUPS * EXPERTS_PER_GROUP
TOP_K = 2
D_EXPERT = 512
EPS = 1e-6
NEG_INF = -1e30

LANES = 128
SUBLANES = 8
TQ_PROMPT = 512
TK_SAMPLE = 512
MLA_HEADS_PER_STEP = 4
FOX_PAIRS_PER_STEP = 2
CONV_TILE = 256
MLA_SLAB = NOPE + LANES
ROW_TILE = 256
EXPERT_TILE = 256
EXPERT_DUMP_ROWS = 2 * EXPERT_TILE
VMEM_LIMIT = 56 * 1024 * 1024

_C_QA, _C_KA, _C_VA, _C_CQ, _C_CKV, _C_UA, _C_UG, _C_KRF = 0, 512, 1024, 1536, 2048, 2304, 2816, 3328
IN_COLS = _C_KRF + LANES
_FA_LANE = ROPE


def _cparams(*sem):
    return pltpu.CompilerParams(dimension_semantics=sem, vmem_limit_bytes=VMEM_LIMIT)


def _const_spec(shape):
    nd = len(shape)
    return pl.BlockSpec(shape, lambda *_: (0,) * nd)


def _rms(x, g):
    return x * lax.rsqrt(jnp.mean(x * x, axis=-1, keepdims=True) + EPS) * g


def _rope_block(x, cos, sin_signed):
    lane = lax.broadcasted_iota(jnp.int32, x.shape, 1)
    partner = jnp.where(lane < ROPE // 2, pltpu.roll(x, LANES - ROPE // 2, 1), pltpu.roll(x, ROPE // 2, 1))
    return x * cos + partner * sin_signed


def _in_proj_kernel(x_ref, gmix_ref, win_ref, gq_ref, gkv_ref, wuq_ref, bf_ref, cos_ref, sin_ref,
                    qa_ref, ka_ref, va_ref, kab_ref, vab_ref, logf_ref, qmla_ref, ckv_ref, kr_ref, u_ref):
    h = _rms(x_ref[...], gmix_ref[...]).astype(BF16)

    def proj(c0, width):
        return jnp.dot(h, win_ref[:, c0:c0 + width], preferred_element_type=F32)

    qa_ref[...] = proj(_C_QA, W_A).astype(BF16)
    ka = proj(_C_KA, W_A)
    ka_ref[...] = ka
    kab_ref[...] = ka.astype(BF16)
    va = proj(_C_VA, W_A)
    va_ref[...] = va
    vab_ref[...] = va.astype(BF16)

    cos = cos_ref[...]
    sin = sin_ref[...]
    krf = proj(_C_KRF, LANES)
    kr_ref[...] = _rope_block(krf, cos, sin)
    z = krf + bf_ref[...]
    logsig = jnp.minimum(z, 0.0) - jnp.log(1.0 + jnp.exp(-jnp.abs(z)))
    logf_ref[...] = logsig[:, _FA_LANE:_FA_LANE + H_A]

    ckv_ref[...] = _rms(proj(_C_CKV, KV_RANK), gkv_ref[...])

    u_ref[...] = proj(_C_UA, C_C) * jax.nn.sigmoid(proj(_C_UG, C_C))

    cqn = _rms(proj(_C_CQ, Q_RANK), gq_ref[...]).astype(BF16)
    scale = (NOPE + ROPE) ** -0.5
    for hd in range(H_B):
        q = jnp.dot(cqn, wuq_ref[:, hd * MLA_SLAB:(hd + 1) * MLA_SLAB], preferred_element_type=F32)
        qmla_ref[:, hd * MLA_SLAB:hd * MLA_SLAB + NOPE] = (q[:, :NOPE] * scale).astype(BF16)
        qmla_ref[:, hd * MLA_SLAB + NOPE:(hd + 1) * MLA_SLAB] = (
            _rope_block(q[:, NOPE:], cos, sin) * scale).astype(BF16)


def _in_proj(x, g_mix, w_in_p, g_q, g_kv, w_uq_p, b_f128, cos, sin):
    n = x.shape[0]
    tm = ROW_TILE
    row = lambda w: pl.BlockSpec((tm, w), lambda i: (i, 0))
    out_shapes = (
        jax.ShapeDtypeStruct((n, W_A), BF16),
        jax.ShapeDtypeStruct((n, W_A), F32),
        jax.ShapeDtypeStruct((n, W_A), F32),
        jax.ShapeDtypeStruct((n, W_A), BF16),
        jax.ShapeDtypeStruct((n, W_A), BF16),
        jax.ShapeDtypeStruct((n, H_A), F32),
        jax.ShapeDtypeStruct((n, H_B * MLA_SLAB), BF16),
        jax.ShapeDtypeStruct((n, KV_RANK), F32),
        jax.ShapeDtypeStruct((n, LANES), F32),
        jax.ShapeDtypeStruct((n, C_C), F32),
    )
    return pl.pallas_call(
        _in_proj_kernel,
        grid=(n // tm,),
        in_specs=[row(D_MODEL), _const_spec((1, D_MODEL)), _const_spec((D_MODEL, IN_COLS)),
                  _const_spec((1, Q_RANK)), _const_spec((1, KV_RANK)),
                  _const_spec((Q_RANK, H_B * MLA_SLAB)), _const_spec((1, LANES)), row(LANES), row(LANES)],
        out_specs=[row(W_A), row(W_A), row(W_A), row(W_A), row(W_A), row(H_A), row(H_B * MLA_SLAB),
                   row(KV_RANK), row(LANES), row(C_C)],
        out_shape=out_shapes,
        compiler_params=_cparams("parallel"),
        name="in_proj",
    )(x, g_mix, w_in_p, g_q, g_kv, w_uq_p, b_f128, cos, sin)


def _cumsum_kernel(x_ref, o_ref):
    c = x_ref[...]
    t = c.shape[-1]
    lane = lax.broadcasted_iota(jnp.int32, c.shape, 1)
    s = 1
    while s < t:
        c = c + jnp.where(lane >= s, pltpu.roll(c, s, 1), 0.0)
        s *= 2
    o_ref[...] = c


def _cumsum_time(logf_t):
    b, hh, t = logf_t.shape
    spec = pl.BlockSpec((None, hh, t), lambda i: (i, 0, 0))
    return pl.pallas_call(
        _cumsum_kernel, grid=(b,), in_specs=[spec], out_specs=spec,
        out_shape=jax.ShapeDtypeStruct((b, hh, t), F32),
        compiler_params=_cparams("parallel"), name="cumsum_logf",
    )(logf_t)


def _softmax_step(s, v, row_bias, m_ref, l_ref, acc_ref):
    width = s.shape[1]
    s_max = jnp.max(s, axis=-1, keepdims=True)
    m_old = m_ref[...]
    m_new = jnp.maximum(m_old, s_max if row_bias is None else s_max + row_bias)
    alpha = jnp.exp(m_old - m_new)
    offset = m_new if row_bias is None else m_new - row_bias
    if width % LANES == 0:
        p = [jnp.exp(s[:, c * LANES:(c + 1) * LANES] - offset) for c in range(width // LANES)]
        l_add = functools.reduce(lambda a, b: a + b, p)
        p = p[0].astype(BF16) if len(p) == 1 else jnp.concatenate([x.astype(BF16) for x in p], axis=1)
    else:
        p = jnp.exp(s - offset[:, :1])
        lane = lax.broadcasted_iota(jnp.int32, m_old.shape, 1)
        l_add = jnp.where(lane == 0, jnp.sum(p, axis=-1, keepdims=True), 0.0)
        p = p.astype(BF16)
    l_ref[...] = alpha * l_ref[...] + l_add
    acc_ref[...] = alpha * acc_ref[...] + jnp.dot(p, v, preferred_element_type=F32)
    m_ref[...] = m_new


def _flash_sweep(heads, n_open, d0, tq, tk, diag_mask, m_ref, l_ref, acc_ref):
    for h in range(len(heads)):
        m_ref[h] = jnp.full(m_ref.shape[1:], NEG_INF, F32)
        l_ref[h] = jnp.zeros(l_ref.shape[1:], F32)
        acc_ref[h] = jnp.zeros(acc_ref.shape[1:], F32)

    def chunk(k0, width, mask):
        for h, (q, key_chunk, value_chunk, col_bias, row_bias) in enumerate(heads):
            s = _qk(q, key_chunk(k0, width))
            if col_bias is not None:
                s = s + col_bias(k0, width)
            if mask is not None:
                s = jnp.where(mask, s, NEG_INF)
            _softmax_step(s, value_chunk(k0, width), row_bias, m_ref.at[h], l_ref.at[h], acc_ref.at[h])

    def open_chunk(i, carry):
        chunk(pl.multiple_of(i * tk, tk), tk, None)
        return carry

    lax.fori_loop(0, n_open, open_chunk, 0)
    chunk(d0, tq, diag_mask)
    return [acc_ref[h] / jnp.sum(l_ref[h], axis=-1, keepdims=True) for h in range(len(heads))]


def _qk(q, k):
    return lax.dot_general(q, k, (((1,), (1,)), ((), ())), preferred_element_type=F32)


def _tile_extent(n_past, tq, tk, single_tile):
    if single_tile:
        return n_past // tk, n_past
    start = n_past + pl.program_id(2) * tq
    return start // tk, pl.multiple_of(start, tq)


def _fox_kernel(q_ref, k_ref, v_ref, cq_ref, ck_ref, o_ref, m_ref, l_ref, acc_ref, *, tq, tk, n_past, single_tile):
    lane = lax.broadcasted_iota(jnp.int32, (tq, LANES), 1)
    n_open, d0 = _tile_extent(n_past, tq, tk, single_tile)
    row = lax.broadcasted_iota(jnp.int32, (tq, tq), 0)
    col = lax.broadcasted_iota(jnp.int32, (tq, tq), 1)
    heads = []
    for pair in range(FOX_PAIRS_PER_STEP):
        lanes = slice(pair * LANES, (pair + 1) * LANES)
        q = q_ref[:, lanes]
        key_chunk = functools.partial(lambda k0, width, lanes: k_ref[pl.ds(k0, width), lanes], lanes=lanes)
        value_chunk = functools.partial(lambda k0, width, lanes: v_ref[pl.ds(k0, width), lanes], lanes=lanes)
        for j in range(2):
            qj = jnp.where((lane >= j * D_HA) & (lane < (j + 1) * D_HA), q, jnp.zeros_like(q))
            col_bias = functools.partial(
                lambda k0, width, pair, j: -ck_ref[pair, j:j + 1, pl.ds(k0, width)], pair=pair, j=j)
            row_bias = jnp.broadcast_to(cq_ref[pair, :, j:j + 1], (tq, LANES))
            heads.append((qj, key_chunk, value_chunk, col_bias, row_bias))
    outs = _flash_sweep(heads, n_open, d0, tq, tk, col <= row, m_ref, l_ref, acc_ref)
    for pair in range(FOX_PAIRS_PER_STEP):
        o_ref[:, pair * LANES:(pair + 1) * LANES] = jnp.where(lane < D_HA, outs[2 * pair], outs[2 * pair + 1])


def _fox_attention(q, k, v, c_q, c_k, *, n_past, tq, tk):
    b, t_q, _ = q.shape
    t_k = k.shape[1]
    t_kp = c_k.shape[-1]
    assert t_q % tq == 0 and n_past % tk == 0 and (t_q == tq or tq % tk == 0)
    npair = FOX_PAIRS_PER_STEP
    kern = functools.partial(_fox_kernel, tq=tq, tk=tk, n_past=n_past, single_tile=t_q == tq)
    return pl.pallas_call(
        kern,
        grid=(b, H_A // (2 * npair), t_q // tq),
        in_specs=[pl.BlockSpec((None, tq, npair * LANES), lambda bi, hp, qi: (bi, qi, hp)),
                  pl.BlockSpec((None, t_k, npair * LANES), lambda bi, hp, qi: (bi, 0, hp)),
                  pl.BlockSpec((None, t_k, npair * LANES), lambda bi, hp, qi: (bi, 0, hp)),
                  pl.BlockSpec((None, npair, tq, 2), lambda bi, hp, qi: (bi, hp, qi, 0)),
                  pl.BlockSpec((None, npair, 2, t_kp), lambda bi, hp, qi: (bi, hp, 0, 0))],
        out_specs=pl.BlockSpec((None, tq, npair * LANES), lambda bi, hp, qi: (bi, qi, hp)),
        out_shape=jax.ShapeDtypeStruct((b, t_q, W_A), F32),
        scratch_shapes=[pltpu.VMEM((2 * npair, tq, LANES), F32)] * 3,
        compiler_params=_cparams("parallel", "parallel", "arbitrary"),
        name="fox_attention",
    )(q, k, v, c_q, c_k)


def _mla_kernel(q_ref, k_ref, v_ref, o_ref, m_ref, l_ref, acc_ref, *, tq, tk, n_past, single_tile):
    n_open, d0 = _tile_extent(n_past, tq, tk, single_tile)
    row = lax.broadcasted_iota(jnp.int32, (tq, tq), 0)
    col = lax.broadcasted_iota(jnp.int32, (tq, tq), 1)
    heads = []
    for j in range(MLA_HEADS_PER_STEP):
        key_chunk = functools.partial(
            lambda k0, width, j: k_ref[pl.ds(k0, width), j * MLA_SLAB:(j + 1) * MLA_SLAB], j=j)
        value_chunk = functools.partial(lambda k0, width, j: v_ref[pl.ds(k0, width), j * V_HD:(j + 1) * V_HD], j=j)
        heads.append((q_ref[:, j * MLA_SLAB:(j + 1) * MLA_SLAB], key_chunk, value_chunk, None, None))
    mask = (d0 + col) >> _CHUNK_SHIFT <= (d0 + row) >> _CHUNK_SHIFT
    outs = _flash_sweep(heads, n_open, d0, tq, tk, mask, m_ref, l_ref, acc_ref)
    for j in range(MLA_HEADS_PER_STEP):
        o_ref[:, j * V_HD:(j + 1) * V_HD] = outs[j]


def _mla_attention(q, k, v, *, n_past, tq, tk):
    b, t_q, _ = q.shape
    t_k = k.shape[1]
    assert t_q % tq == 0 and n_past % tk == 0 and (t_q == tq or tq % tk == 0)
    nh = MLA_HEADS_PER_STEP
    kern = functools.partial(_mla_kernel, tq=tq, tk=tk, n_past=n_past, single_tile=t_q == tq)
    return pl.pallas_call(
        kern,
        grid=(b, H_B // nh, t_q // tq),
        in_specs=[pl.BlockSpec((None, tq, nh * MLA_SLAB), lambda bi, hd, qi: (bi, qi, hd)),
                  pl.BlockSpec((None, t_k, nh * MLA_SLAB), lambda bi, hd, qi: (bi, 0, hd)),
                  pl.BlockSpec((None, t_k, nh * V_HD), lambda bi, hd, qi: (bi, 0, hd))],
        out_specs=pl.BlockSpec((None, tq, nh * V_HD), lambda bi, hd, qi: (bi, qi, hd)),
        out_shape=jax.ShapeDtypeStruct((b, t_q, W_B), F32),
        scratch_shapes=[pltpu.VMEM((nh, tq, LANES), F32), pltpu.VMEM((nh, tq, LANES), F32),
                        pltpu.VMEM((nh, tq, V_HD), F32)],
        compiler_params=_cparams("parallel", "parallel", "arbitrary"),
        name="mla_attention",
    )(q, k, v)


def _kv_up_kernel(ckv_ref, kr_ref, wuk_ref, wuv_ref, k_ref, v_ref):
    c = ckv_ref[...].astype(BF16)
    kr = kr_ref[...].astype(BF16)
    kn = jnp.dot(c, wuk_ref[...], preferred_element_type=F32).astype(BF16)
    for hd in range(H_B):
        k_ref[:, hd * MLA_SLAB:hd * MLA_SLAB + NOPE] = kn[:, hd * NOPE:(hd + 1) * NOPE]
        k_ref[:, hd * MLA_SLAB + NOPE:(hd + 1) * MLA_SLAB] = kr
    v_ref[...] = jnp.dot(c, wuv_ref[...], preferred_element_type=F32).astype(BF16)


def _kv_up(ckv_n, krope128, w_uk, w_uv):
    r = ckv_n.shape[0]
    tm = ROW_TILE
    row = lambda w: pl.BlockSpec((tm, w), lambda i: (i, 0))
    return pl.pallas_call(
        _kv_up_kernel, grid=(r // tm,),
        in_specs=[row(KV_RANK), row(LANES), _const_spec((KV_RANK, H_B * NOPE)),
                  _const_spec((KV_RANK, H_B * V_HD))],
        out_specs=[row(H_B * MLA_SLAB), row(H_B * V_HD)],
        out_shape=(jax.ShapeDtypeStruct((r, H_B * MLA_SLAB), BF16), jax.ShapeDtypeStruct((r, H_B * V_HD), BF16)),
        compiler_params=_cparams("parallel"), name="mla_kv_up",
    )(ckv_n, krope128, w_uk, w_uv)


_CONV_SUB = 64
_CONV_HALO = -(-CONV_W // SUBLANES) * SUBLANES


def _conv_kernel(xp_ref, w_ref, b_ref, g_ref, beta_ref, o_ref, *, tt):
    t0 = pl.program_id(1) * tt
    w = w_ref[...]
    rows = min(_CONV_SUB, tt)
    for sub in range(tt // rows):
        base = pl.multiple_of(t0 + sub * rows, rows)
        xa = xp_ref[pl.ds(base, rows + _CONV_HALO), :]
        acc = None
        for b in range(SUBLANES):
            z = None
            for a in range(-(-CONV_W // SUBLANES)):
                tap = SUBLANES * a + b
                if tap < CONV_W:
                    term = xa[SUBLANES * a:SUBLANES * a + rows + SUBLANES, :] * w[tap:tap + 1, :]
                    z = term if z is None else z + term
            acc = z[b:b + rows, :] if acc is None else acc + z[b:b + rows, :]
        y = acc + b_ref[...]
        mu = jnp.mean(y, axis=-1, keepdims=True)
        yc = y - mu
        var = jnp.mean(yc * yc, axis=-1, keepdims=True)
        y = yc * lax.rsqrt(var + EPS) * g_ref[...] + beta_ref[...]
        o_ref[sub * rows:(sub + 1) * rows, :] = y * jax.nn.sigmoid(y)


def _conv_module(xp, w_dw, b_dw, ln_g, ln_b, *, t, tt):
    b, t_p, _ = xp.shape
    kern = functools.partial(_conv_kernel, tt=tt)
    return pl.pallas_call(
        kern, grid=(b, t // tt),
        in_specs=[pl.BlockSpec((None, t_p, C_C), lambda bi, ti: (bi, 0, 0)),
                  _const_spec((CONV_W, C_C)), _const_spec((1, C_C)), _const_spec((1, C_C)), _const_spec((1, C_C))],
        out_specs=pl.BlockSpec((None, tt, C_C), lambda bi, ti: (bi, ti, 0)),
        out_shape=jax.ShapeDtypeStruct((b, t, C_C), F32),
        compiler_params=_cparams("parallel", "arbitrary"), name="conv_module",
    )(xp, w_dw, b_dw, ln_g, ln_b)


def _out_proj_kernel(oa_ref, ob_ref, oc_ref, x_ref, gout_ref, wout_ref, gffn_ref, wr_ref, br_ref,
                     xn_ref, h2_ref, rl_ref):
    g = gout_ref[...]
    na = _rms(oa_ref[...], g[:, :W_A]).astype(BF16)
    nb = _rms(ob_ref[...], g[:, W_A:W_A + W_B]).astype(BF16)
    nc = _rms(oc_ref[...], g[:, W_A + W_B:]).astype(BF16)
    mix = jnp.dot(na, wout_ref[:W_A, :], preferred_element_type=F32)
    mix = mix + jnp.dot(nb, wout_ref[W_A:W_A + W_B, :], preferred_element_type=F32)
    mix = mix + jnp.dot(nc, wout_ref[W_A + W_B:, :], preferred_element_type=F32)
    xn = x_ref[...] + mix
    xn_ref[...] = xn
    h2 = _rms(xn, gffn_ref[...]).astype(BF16)
    rl_ref[...] = jnp.dot(h2, wr_ref[...], preferred_element_type=F32) + br_ref[...]
    bits = pltpu.bitcast(h2.astype(F32), jnp.uint32)
    h2_ref[...] = (bits[:, :D_MODEL // 2] >> 16) | (bits[:, D_MODEL // 2:] & jnp.uint32(0xFFFF0000))


def _out_proj(o_a, o_b, o_c, x, g_out, w_out, g_ffn, w_r, b_r):
    n = x.shape[0]
    tm = ROW_TILE
    row = lambda w: pl.BlockSpec((tm, w), lambda i: (i, 0))
    return pl.pallas_call(
        _out_proj_kernel, grid=(n // tm,),
        in_specs=[row(W_A), row(W_B), row(C_C), row(D_MODEL), _const_spec((1, D_MODEL)),
                  _const_spec((D_MODEL, D_MODEL)), _const_spec((1, D_MODEL)),
                  _const_spec((D_MODEL, LANES)), _const_spec((1, LANES))],
        out_specs=[row(D_MODEL), row(D_MODEL // 2), row(LANES)],
        out_shape=(jax.ShapeDtypeStruct((n, D_MODEL), F32), jax.ShapeDtypeStruct((n, D_MODEL // 2), jnp.uint32),
                   jax.ShapeDtypeStruct((n, LANES), F32)),
        compiler_params=_cparams("parallel"), name="out_proj_router",
    )(o_a, o_b, o_c, x, g_out, w_out, g_ffn, w_r, b_r)


def _expert_kernel(be_ref, nv_ref, src_ref, dst_ref, h_hbm, wg_ref, wu_ref, wd_ref, y_hbm,
                   xbuf, ybuf, zbuf, wgb_ref, wub_ref, wdb_ref, gsem, ssem, zsem, *, n_blocks, dump_row0):
    te = EXPERT_TILE
    xt = D_MODEL // 2 // LANES
    yt = D_MODEL // LANES
    i = pl.program_id(0)
    nv = nv_ref[0]
    cur = i % 2
    nxt = 1 - cur
    valid = i < nv

    def gather_copy(tok, buf, r):
        return pltpu.make_async_copy(h_hbm.at[tok], xbuf.at[buf, r // SUBLANES, :, r % SUBLANES, :], gsem.at[buf])

    def scatter_copy(row, buf, r):
        return pltpu.make_async_copy(ybuf.at[buf, r // SUBLANES, :, r % SUBLANES, :], y_hbm.at[row], ssem.at[buf])

    def gather_block(block, buf):
        for r in range(te):
            gather_copy(src_ref[block * te + r], buf, r).start()

    def scatter_block(block, buf):
        for r in range(te):
            scatter_copy(dst_ref[block * te + r], buf, r).start()

    def wait_gather(buf):
        for r in range(te):
            gather_copy(0, buf, 0).wait()

    def wait_scatter(buf):
        for r in range(te):
            scatter_copy(0, buf, 0).wait()

    @pl.when(i == 0)
    def _():
        gather_block(0, 0)
        zbuf[...] = jnp.zeros(zbuf.shape, F32)
        fills = [pltpu.make_async_copy(zbuf, y_hbm.at[pl.ds(dump_row0 + k * SUBLANES, SUBLANES)], zsem)
                 for k in range(EXPERT_DUMP_ROWS // SUBLANES)]
        for fill in fills:
            fill.start()
        for fill in fills:
            fill.wait()

    prev_e = be_ref[jnp.maximum(i - 1, 0)]

    @pl.when(valid & ((i == 0) | (be_ref[i] != prev_e)))
    def _():
        wgb_ref[...] = wg_ref[...].astype(BF16)
        wub_ref[...] = wu_ref[...].astype(BF16)
        wdb_ref[...] = wd_ref[...].astype(BF16)

    @pl.when(valid)
    def _():
        wait_gather(cur)

    @pl.when(valid & (i >= 2))
    def _():
        wait_scatter(cur)

    def step(scatter_previous):
        gather_block(jnp.minimum(i + 1, n_blocks - 1), nxt)
        if scatter_previous:
            scatter_block(i - 1, nxt)
        u = jnp.concatenate([xbuf[cur, :, j].reshape(te, LANES) for j in range(xt)], axis=1)
        x_lo = pltpu.bitcast(u << 16, F32).astype(BF16)
        x_hi = pltpu.bitcast(u & jnp.uint32(0xFFFF0000), F32).astype(BF16)
        half = D_MODEL // 2
        gate = (jnp.dot(x_lo, wgb_ref[:half, :], preferred_element_type=F32)
                + jnp.dot(x_hi, wgb_ref[half:, :], preferred_element_type=F32))
        up = (jnp.dot(x_lo, wub_ref[:half, :], preferred_element_type=F32)
              + jnp.dot(x_hi, wub_ref[half:, :], preferred_element_type=F32))
        hid = (gate * jax.nn.sigmoid(gate) * up).astype(BF16)
        y = jnp.dot(hid, wdb_ref[...], preferred_element_type=F32)
        for j in range(yt):
            ybuf[cur, :, j] = y[:, j * LANES:(j + 1) * LANES].reshape(te // SUBLANES, SUBLANES, LANES)

    @pl.when(valid & (i == 0))
    def _():
        step(False)

    @pl.when(valid & (i > 0))
    def _():
        step(True)

    @pl.when(i == nv - 1)
    def _():
        scatter_block(i, cur)
        wait_scatter(cur)
        wait_gather(nxt)

    @pl.when((i == nv - 1) & (i > 0))
    def _():
        wait_scatter(nxt)


def _expert_blocks(block_e, n_valid, slot_src, slot_dst, h2_tiles, w_gate, w_up, w_down, *, layer):
    n_tok = h2_tiles.shape[0]
    te = EXPERT_TILE
    n_blocks = block_e.shape[0]
    xt, yt = D_MODEL // 2 // LANES, D_MODEL // LANES
    weights = lambda r, c: pl.BlockSpec((None, None, r, c), lambda i, be, nv, src, dst: (layer, be[i], 0, 0))
    grid_spec = pltpu.PrefetchScalarGridSpec(
        num_scalar_prefetch=4, grid=(n_blocks,),
        in_specs=[pl.BlockSpec(memory_space=pl.ANY), weights(D_MODEL, D_EXPERT), weights(D_MODEL, D_EXPERT),
                  weights(D_EXPERT, D_MODEL)],
        out_specs=pl.BlockSpec(memory_space=pl.ANY),
        scratch_shapes=[pltpu.VMEM((2, te // SUBLANES, xt, SUBLANES, LANES), jnp.uint32),
                        pltpu.VMEM((2, te // SUBLANES, yt, SUBLANES, LANES), F32),
                        pltpu.VMEM((SUBLANES, yt, LANES), F32),
                        pltpu.VMEM((D_MODEL, D_EXPERT), BF16), pltpu.VMEM((D_MODEL, D_EXPERT), BF16),
                        pltpu.VMEM((D_EXPERT, D_MODEL), BF16),
                        pltpu.SemaphoreType.DMA((2,)), pltpu.SemaphoreType.DMA((2,)), pltpu.SemaphoreType.DMA(())])
    kern = functools.partial(_expert_kernel, n_blocks=n_blocks, dump_row0=TOP_K * n_tok)
    return pl.pallas_call(
        kern, grid_spec=grid_spec,
        out_shape=jax.ShapeDtypeStruct((TOP_K * n_tok + EXPERT_DUMP_ROWS, yt, LANES), F32),
        compiler_params=_cparams("arbitrary"), name="expert_blocks",
    )(block_e, n_valid, slot_src, slot_dst, h2_tiles, w_gate, w_up, w_down)


def _route(rlog):
    n = rlog.shape[0]
    idx = jnp.arange(n)
    g_logits = rlog[:, :N_GROUPS]
    g_sel = jnp.argmax(g_logits, axis=-1)
    g_w = jax.nn.softmax(g_logits, axis=-1)[idx, g_sel]
    e_logits = rlog[:, N_GROUPS:N_GROUPS + N_EXPERTS].reshape(n, N_GROUPS, EXPERTS_PER_GROUP)[idx, g_sel]
    top_v, top_i = lax.top_k(e_logits, TOP_K)
    gate = jax.nn.softmax(top_v, axis=-1) * g_w[:, None]
    expert_ids = (g_sel[:, None] * EXPERTS_PER_GROUP + top_i).astype(jnp.int32)
    return expert_ids, gate


def _dispatch_plan(expert_ids):
    n = expert_ids.shape[0]
    te = EXPERT_TILE
    a = n * TOP_K
    n_blocks = -(-a // te) + N_EXPERTS
    rows = n_blocks * te
    flat_e = expert_ids.reshape(-1)
    experts = jnp.arange(N_EXPERTS, dtype=jnp.int32)
    counts = jnp.sum((flat_e[:, None] == experts[None, :]).astype(jnp.int32), axis=0)
    padded = (counts + te - 1) // te * te
    pad_end = jnp.cumsum(padded)
    pad_start = pad_end - padded
    start = jnp.cumsum(counts) - counts
    order = jnp.argsort(flat_e).astype(jnp.int32)
    slot = jnp.arange(rows, dtype=jnp.int32)
    slot_e = jnp.minimum(jnp.sum((pad_end[None, :] <= slot[:, None]).astype(jnp.int32), axis=-1), N_EXPERTS - 1)
    slot_hot = (slot_e[:, None] == experts[None, :]).astype(jnp.int32)
    slot_rank = slot - jnp.sum(slot_hot * pad_start[None, :], axis=-1)
    filled = slot_rank < jnp.sum(slot_hot * counts[None, :], axis=-1)
    src = jnp.where(filled, jnp.sum(slot_hot * start[None, :], axis=-1) + slot_rank, 0)
    slot_a = order[src]
    slot_src = jnp.where(filled, slot_a // TOP_K, 0)
    slot_dst = jnp.where(filled, (slot_a % TOP_K) * n + slot_a // TOP_K, a + slot % EXPERT_DUMP_ROWS)
    block_e = slot_e.reshape(n_blocks, te)[:, 0]
    n_valid = (pad_end[-1] // te).astype(jnp.int32).reshape(1)
    return slot_src, slot_dst, block_e, n_valid


def _combine_kernel(x_ref, y0_ref, y1_ref, gate_ref, gfin_ref, o_ref, z_ref, *, final):
    z_ref[...] = y0_ref[...] * gate_ref[:, 0:1, :] + y1_ref[...] * gate_ref[:, 1:2, :]
    x = jnp.concatenate([x_ref[:, j * LANES:(j + 1) * LANES] + z_ref[:, j, :] for j in range(D_MODEL // LANES)],
                        axis=1)
    o_ref[...] = _rms(x, gfin_ref[...]) if final else x


def _combine(x, ys, gate, g_final, *, row0, n_tok, final):
    n = x.shape[0]
    tm = ROW_TILE
    assert row0 % tm == 0 and n_tok % tm == 0
    row = pl.BlockSpec((tm, D_MODEL), lambda i: (i, 0))
    expert_rows = lambda k: pl.BlockSpec((tm, D_MODEL // LANES, LANES),
                                         lambda i: (i + (row0 + k * n_tok) // tm, 0, 0))
    return pl.pallas_call(
        functools.partial(_combine_kernel, final=final), grid=(n // tm,),
        in_specs=[row, expert_rows(0), expert_rows(1), pl.BlockSpec((tm, TOP_K, LANES), lambda i: (i, 0, 0)),
                  _const_spec((1, D_MODEL))],
        out_specs=row, out_shape=jax.ShapeDtypeStruct((n, D_MODEL), F32),
        scratch_shapes=[pltpu.VMEM((tm, D_MODEL // LANES, LANES), F32)],
        compiler_params=_cparams("parallel"), name="moe_combine",
    )(x, ys, ys, gate, g_final)


def _rope_tables(pos):
    half = ROPE // 2
    inv = ROPE_THETA ** (-jnp.arange(half, dtype=F32) / half)
    ang = pos.astype(F32)[:, None] * inv[None, :]
    zeros = jnp.zeros((pos.shape[0], LANES - ROPE), F32)
    cos = jnp.concatenate([jnp.cos(ang), jnp.cos(ang), zeros], axis=-1)
    sin = jnp.concatenate([-jnp.sin(ang), jnp.sin(ang), zeros], axis=-1)
    return cos, sin


def _prep_w_in(w_in):
    qa, ka, va, fa, cq, ckv, kr, ua, ug = jnp.split(
        w_in, np.cumsum([W_A, W_A, W_A, H_A, Q_RANK, KV_RANK, ROPE, C_C])[:].tolist(), axis=1)
    pad = jnp.zeros((D_MODEL, LANES - ROPE - H_A), w_in.dtype)
    return jnp.concatenate([qa * (D_HA ** -0.5), ka, va, cq, ckv, ua, ug, kr, fa, pad], axis=1).astype(BF16)


def _prep_w_uq(w_uq):
    w = w_uq.reshape(Q_RANK, H_B, NOPE + ROPE)
    w = jnp.pad(w, ((0, 0), (0, 0), (0, MLA_SLAB - NOPE - ROPE)))
    return w.reshape(Q_RANK, H_B * MLA_SLAB).astype(BF16)


def _prep_w_ukv(w_ukv):
    w = w_ukv.reshape(KV_RANK, H_B, NOPE + V_HD)
    return (w[:, :, :NOPE].reshape(KV_RANK, H_B * NOPE).astype(BF16),
            w[:, :, NOPE:].reshape(KV_RANK, H_B * V_HD).astype(BF16))


def _pad_lanes(x, width):
    return jnp.pad(x, ((0, 0),) * (x.ndim - 1) + ((0, width - x.shape[-1]),))


def _round_up(x, m):
    return -(-x // m) * m


def kernel(x_prompt, x_sample, cache_fox_k, cache_fox_v, cache_fox_logf, cache_mla_ckv, cache_mla_krope, state_conv, g_mix, w_in, b_f, g_q, w_uq, g_kv, w_ukv, w_dw, b_dw, ln_g, ln_b, g_out, w_out, g_ffn, w_rg, b_rg, w_re, b_re, w_gate, w_up, w_down, g_final):
    bp, tp, _ = x_prompt.shape
    bs, ts, _ = x_sample.shape
    n_past = cache_fox_k.shape[2]
    n_p, n_s = bp * tp, bs * ts
    depth = g_mix.shape[0]

    halo = CONV_W - 1
    tkp_s = _round_up(n_past + ts, LANES)

    x_p = x_prompt.reshape(n_p, D_MODEL)
    x_s = x_sample.reshape(n_s, D_MODEL)
    cos_p, sin_p = _rope_tables(jnp.arange(tp))
    cos_s, sin_s = _rope_tables(n_past + jnp.arange(ts))
    rope_p = (jnp.tile(cos_p, (bp, 1)), jnp.tile(sin_p, (bp, 1)))
    rope_s = (jnp.tile(cos_s, (bs, 1)), jnp.tile(sin_s, (bs, 1)))

    def split_heads(c):
        return c.reshape(c.shape[0], H_A // 2, 2, c.shape[-1])

    states_p, states_s = [], []
    for l in range(depth):
        w_uk, w_uv = _prep_w_ukv(w_ukv[l])
        b_f128 = jnp.pad(b_f[l], (_FA_LANE, LANES - _FA_LANE - H_A))[None, :]
        in_w = (g_mix[l][None], _prep_w_in(w_in[l]), g_q[l][None], g_kv[l][None], _prep_w_uq(w_uq[l]), b_f128)
        qa_p, ka_p, va_p, kab_p, vab_p, lf_p, qm_p, ckv_p, kr_p, u_p = _in_proj(x_p, *in_w, *rope_p)
        qa_s, ka_s, va_s, kab_s, vab_s, lf_s, qm_s, ckv_s, kr_s, u_s = _in_proj(x_s, *in_w, *rope_s)

        lf_p = lf_p.reshape(bp, tp, H_A)
        lf_s = lf_s.reshape(bs, ts, H_A)
        c_p = _cumsum_time(lf_p.transpose(0, 2, 1))
        lf_all = jnp.concatenate([cache_fox_logf[l].astype(F32), lf_s], axis=1)
        c_s = _cumsum_time(_pad_lanes(lf_all.transpose(0, 2, 1), tkp_s))
        o_a_p = _fox_attention(
            qa_p.reshape(bp, tp, W_A), kab_p.reshape(bp, tp, W_A), vab_p.reshape(bp, tp, W_A),
            split_heads(c_p).transpose(0, 1, 3, 2), split_heads(c_p), n_past=0, tq=TQ_PROMPT, tk=TQ_PROMPT)
        k_all = jnp.concatenate([cache_fox_k[l].reshape(bs, n_past, W_A).astype(BF16),
                                 kab_s.reshape(bs, ts, W_A)], axis=1)
        v_all = jnp.concatenate([cache_fox_v[l].reshape(bs, n_past, W_A).astype(BF16),
                                 vab_s.reshape(bs, ts, W_A)], axis=1)
        o_a_s = _fox_attention(
            qa_s.reshape(bs, ts, W_A), k_all, v_all,
            split_heads(c_s[:, :, n_past:n_past + ts]).transpose(0, 1, 3, 2), split_heads(c_s),
            n_past=n_past, tq=ts, tk=TK_SAMPLE)

        ckv_s = ckv_s.reshape(bs, ts, KV_RANK)
        kr_s = kr_s.reshape(bs, ts, LANES)
        ckv_all = jnp.concatenate([cache_mla_ckv[l].astype(F32), ckv_s], axis=1)
        kr_all = jnp.concatenate([_pad_lanes(cache_mla_krope[l].astype(F32), LANES), kr_s], axis=1)
        km_p, vm_p = _kv_up(ckv_p, kr_p, w_uk, w_uv)
        km_s, vm_s = _kv_up(ckv_all.reshape(-1, KV_RANK), kr_all.reshape(-1, LANES), w_uk, w_uv)
        o_b_p = _mla_attention(
            qm_p.reshape(bp, tp, H_B * MLA_SLAB), km_p.reshape(bp, tp, H_B * MLA_SLAB), vm_p.reshape(bp, tp, W_B),
            n_past=0, tq=TQ_PROMPT, tk=TQ_PROMPT)
        o_b_s = _mla_attention(
            qm_s.reshape(bs, ts, H_B * MLA_SLAB), km_s.reshape(bs, n_past + ts, H_B * MLA_SLAB),
            vm_s.reshape(bs, n_past + ts, W_B), n_past=n_past, tq=ts, tk=TK_SAMPLE)

        xp_p = jnp.pad(u_p.reshape(bp, tp, C_C), ((0, 0), (halo, _CONV_HALO - halo), (0, 0)))
        xp_s = jnp.concatenate([state_conv[l].astype(F32), u_s.reshape(bs, ts, C_C)], axis=1)
        conv_p = xp_p[:, tp:tp + halo]
        conv_s = xp_s[:, ts:ts + halo]
        xp_s = jnp.pad(xp_s, ((0, 0), (0, _CONV_HALO - halo), (0, 0)))
        conv_w = (w_dw[l], b_dw[l][None], ln_g[l][None], ln_b[l][None])
        o_c_p = _conv_module(xp_p, *conv_w, t=tp, tt=CONV_TILE)
        o_c_s = _conv_module(xp_s, *conv_w, t=ts, tt=ts)

        w_r = _pad_lanes(jnp.concatenate([w_rg[l], w_re[l]], axis=1), LANES).astype(BF16)
        b_r = _pad_lanes(jnp.concatenate([b_rg[l], b_re[l].reshape(-1)])[None, :].astype(F32), LANES)
        out_w = (g_out[l][None], w_out[l].astype(BF16), g_ffn[l][None], w_r, b_r)
        x_p, h2_p, rlog_p = _out_proj(o_a_p.reshape(n_p, W_A), o_b_p.reshape(n_p, W_B), o_c_p.reshape(n_p, C_C),
                                      x_p, *out_w)
        x_s, h2_s, rlog_s = _out_proj(o_a_s.reshape(n_s, W_A), o_b_s.reshape(n_s, W_B), o_c_s.reshape(n_s, C_C),
                                      x_s, *out_w)

        expert_ids, gate = _route(jnp.concatenate([rlog_p, rlog_s], axis=0))
        slot_src, slot_dst, block_e, n_valid = _dispatch_plan(expert_ids)
        h2_tiles = jnp.concatenate([h2_p, h2_s], axis=0).reshape(n_p + n_s, D_MODEL // 2 // LANES, LANES)
        ys = _expert_blocks(block_e, n_valid, slot_src, slot_dst, h2_tiles, w_gate, w_up, w_down, layer=l)
        final = l == depth - 1
        gate = jnp.broadcast_to(gate[:, :, None], (n_p + n_s, TOP_K, LANES))
        x_p = _combine(x_p, ys, gate[:n_p], g_final[None], row0=0, n_tok=n_p + n_s, final=final)
        x_s = _combine(x_s, ys, gate[n_p:], g_final[None], row0=n_p, n_tok=n_p + n_s, final=final)

        states_p.append((ka_p.reshape(bp, tp, H_A, D_HA), va_p.reshape(bp, tp, H_A, D_HA), lf_p,
                         ckv_p.reshape(bp, tp, KV_RANK), kr_p[:, :ROPE].reshape(bp, tp, ROPE), conv_p))
        states_s.append((ka_s.reshape(bs, ts, H_A, D_HA), va_s.reshape(bs, ts, H_A, D_HA), lf_s,
                         ckv_s, kr_s[:, :, :ROPE], conv_s))

    y_p = x_p.reshape(bp, tp, D_MODEL)
    y_s = x_s.reshape(bs, ts, D_MODEL)
    p_out = [jnp.stack(a) for a in zip(*states_p)]
    s_out = [jnp.stack(a) for a in zip(*states_s)]
    return (y_p, y_s, *p_out, *s_out)
```

```python
import functools

import numpy as np
import jax
import jax.numpy as jnp
from jax import lax
from jax.experimental import pallas as pl
from jax.experimental.pallas import tpu as pltpu

F32 = jnp.float32
BF16 = jnp.bfloat16

D_MODEL = 2048
DEPTH = 4
CHUNK = 64
H_A, D_HA = 8, 64
W_A = H_A * D_HA
H_B, NOPE, ROPE, V_HD = 8, 128, 64, 128
Q_RANK, KV_RANK = 512, 256
W_B = H_B * V_HD
C_C = D_MODEL - W_A - W_B
CONV_W = 31
_CHUNK_SHIFT = CHUNK.bit_length() - 1
assert 1 << _CHUNK_SHIFT == CHUNK
ROPE_THETA = 10000.0
N_GROUPS, EXPERTS_PER_GROUP = 4, 8
N_EXPERTS = N_GROUPS * EXPERTS_PER_GROUP
TOP_K = 2
D_EXPERT = 512
EPS = 1e-6
NEG_INF = -1e30
LOG2E = 1.4426950408889634

LANES = 128
SUBLANES = 8
TQ_PROMPT = 512
TK_SAMPLE = 512
MLA_HEADS_PER_STEP = 4
FOX_PAIRS_PER_STEP = 2
CONV_TILE = 256
MLA_SLAB = NOPE + LANES
ROW_TILE = 256
EXPERT_TILE = 256
EXPERT_DUMP_ROWS = 2 * EXPERT_TILE
VMEM_LIMIT = 56 * 1024 * 1024

_C_QA, _C_KA, _C_VA, _C_CQ, _C_CKV, _C_UA, _C_UG, _C_KRF = 0, 512, 1024, 1536, 2048, 2304, 2816, 3328
IN_COLS = _C_KRF + LANES
_FA_LANE = ROPE


def _cparams(*sem):
    return pltpu.CompilerParams(dimension_semantics=sem, vmem_limit_bytes=VMEM_LIMIT)


def _const_spec(shape):
    nd = len(shape)
    return pl.BlockSpec(shape, lambda *_: (0,) * nd)


def _rms(x, g):
    return x * lax.rsqrt(jnp.mean(x * x, axis=-1, keepdims=True) + EPS) * g


def _rope_block(x, cos, sin_signed):
    lane = lax.broadcasted_iota(jnp.int32, x.shape, 1)
    partner = jnp.where(lane < ROPE // 2, pltpu.roll(x, LANES - ROPE // 2, 1), pltpu.roll(x, ROPE // 2, 1))
    return x * cos + partner * sin_signed


def _in_proj_kernel(x_ref, gmix_ref, win_ref, gq_ref, gkv_ref, wuq_ref, bf_ref, cos_ref, sin_ref,
                    qa_ref, ka_ref, va_ref, kab_ref, vab_ref, logf_ref, qmla_ref, ckv_ref, kr_ref, u_ref):
    h = _rms(x_ref[...], gmix_ref[...]).astype(BF16)

    def proj(c0, width):
        return jnp.dot(h, win_ref[:, c0:c0 + width], preferred_element_type=F32)

    qa_ref[...] = (proj(_C_QA, W_A) * LOG2E).astype(BF16)
    ka = proj(_C_KA, W_A)
    ka_ref[...] = ka
    kab_ref[...] = ka.astype(BF16)
    va = proj(_C_VA, W_A)
    va_ref[...] = va
    vab_ref[...] = va.astype(BF16)

    cos = cos_ref[...]
    sin = sin_ref[...]
    krf = proj(_C_KRF, LANES)
    kr_ref[...] = _rope_block(krf, cos, sin)
    z = krf + bf_ref[...]
    logsig = jnp.minimum(z, 0.0) - jnp.log(1.0 + jnp.exp(-jnp.abs(z)))
    logf_ref[...] = logsig[:, _FA_LANE:_FA_LANE + H_A]

    ckv_ref[...] = _rms(proj(_C_CKV, KV_RANK), gkv_ref[...])

    u_ref[...] = proj(_C_UA, C_C) * jax.nn.sigmoid(proj(_C_UG, C_C))

    cqn = _rms(proj(_C_CQ, Q_RANK), gq_ref[...]).astype(BF16)
    scale = (NOPE + ROPE) ** -0.5 * LOG2E
    for hd in range(H_B):
        q = jnp.dot(cqn, wuq_ref[:, hd * MLA_SLAB:(hd + 1) * MLA_SLAB], preferred_element_type=F32)
        qmla_ref[:, hd * MLA_SLAB:hd * MLA_SLAB + NOPE] = (q[:, :NOPE] * scale).astype(BF16)
        qmla_ref[:, hd * MLA_SLAB + NOPE:(hd + 1) * MLA_SLAB] = (
            _rope_block(q[:, NOPE:], cos, sin) * scale).astype(BF16)


def _in_proj(x, g_mix, w_in_p, g_q, g_kv, w_uq_p, b_f128, cos, sin):
    n = x.shape[0]
    tm = ROW_TILE
    row = lambda w: pl.BlockSpec((tm, w), lambda i: (i, 0))
    out_shapes = (
        jax.ShapeDtypeStruct((n, W_A), BF16),
        jax.ShapeDtypeStruct((n, W_A), F32),
        jax.ShapeDtypeStruct((n, W_A), F32),
        jax.ShapeDtypeStruct((n, W_A), BF16),
        jax.ShapeDtypeStruct((n, W_A), BF16),
        jax.ShapeDtypeStruct((n, H_A), F32),
        jax.ShapeDtypeStruct((n, H_B * MLA_SLAB), BF16),
        jax.ShapeDtypeStruct((n, KV_RANK), F32),
        jax.ShapeDtypeStruct((n, LANES), F32),
        jax.ShapeDtypeStruct((n, C_C), F32),
    )
    return pl.pallas_call(
        _in_proj_kernel,
        grid=(n // tm,),
        in_specs=[row(D_MODEL), _const_spec((1, D_MODEL)), _const_spec((D_MODEL, IN_COLS)),
                  _const_spec((1, Q_RANK)), _const_spec((1, KV_RANK)),
                  _const_spec((Q_RANK, H_B * MLA_SLAB)), _const_spec((1, LANES)), row(LANES), row(LANES)],
        out_specs=[row(W_A), row(W_A), row(W_A), row(W_A), row(W_A), row(H_A), row(H_B * MLA_SLAB),
                   row(KV_RANK), row(LANES), row(C_C)],
        out_shape=out_shapes,
        compiler_params=_cparams("parallel"),
        name="in_proj",
    )(x, g_mix, w_in_p, g_q, g_kv, w_uq_p, b_f128, cos, sin)


def _cumsum_kernel(x_ref, o_ref):
    c = x_ref[...]
    t = c.shape[-1]
    lane = lax.broadcasted_iota(jnp.int32, c.shape, 1)
    s = 1
    while s < t:
        c = c + jnp.where(lane >= s, pltpu.roll(c, s, 1), 0.0)
        s *= 2
    o_ref[...] = c * LOG2E


def _cumsum_time(logf_t):
    b, hh, t = logf_t.shape
    spec = pl.BlockSpec((None, hh, t), lambda i: (i, 0, 0))
    return pl.pallas_call(
        _cumsum_kernel, grid=(b,), in_specs=[spec], out_specs=spec,
        out_shape=jax.ShapeDtypeStruct((b, hh, t), F32),
        compiler_params=_cparams("parallel"), name="cumsum_logf",
    )(logf_t)


def _softmax_step(s, v, row_bias, m_ref, l_ref, acc_ref):
    width = s.shape[1]
    s_max = jnp.max(s, axis=-1, keepdims=True)
    m_old = m_ref[...]
    m_new = jnp.maximum(m_old, s_max if row_bias is None else s_max + row_bias)
    alpha = jnp.exp2(m_old - m_new)
    offset = m_new if row_bias is None else m_new - row_bias
    if width % LANES == 0:
        p = [jnp.exp2(s[:, c * LANES:(c + 1) * LANES] - offset) for c in range(width // LANES)]
        l_add = functools.reduce(lambda a, b: a + b, p)
        p = p[0].astype(BF16) if len(p) == 1 else jnp.concatenate([x.astype(BF16) for x in p], axis=1)
    else:
        p = jnp.exp2(s - offset[:, :1])
        lane = lax.broadcasted_iota(jnp.int32, m_old.shape, 1)
        l_add = jnp.where(lane == 0, jnp.sum(p, axis=-1, keepdims=True), 0.0)
        p = p.astype(BF16)
    l_ref[...] = alpha * l_ref[...] + l_add
    acc_ref[...] = alpha * acc_ref[...] + jnp.dot(p, v, preferred_element_type=F32)
    m_ref[...] = m_new


def _flash_sweep(heads, n_open, d0, tq, tk, diag_mask, m_ref, l_ref, acc_ref):
    for h in range(len(heads)):
        m_ref[h] = jnp.full(m_ref.shape[1:], NEG_INF, F32)
        l_ref[h] = jnp.zeros(l_ref.shape[1:], F32)
        acc_ref[h] = jnp.zeros(acc_ref.shape[1:], F32)

    def chunk(k0, width, mask):
        for h, (q, key_chunk, value_chunk, col_bias, row_bias) in enumerate(heads):
            s = _qk(q, key_chunk(k0, width))
            if col_bias is not None:
                s = s + col_bias(k0, width)
            if mask is not None:
                s = jnp.where(mask, s, NEG_INF)
            _softmax_step(s, value_chunk(k0, width), row_bias, m_ref.at[h], l_ref.at[h], acc_ref.at[h])

    def open_chunk(i, carry):
        chunk(pl.multiple_of(i * tk, tk), tk, None)
        return carry

    lax.fori_loop(0, n_open, open_chunk, 0)
    chunk(d0, tq, diag_mask)
    return [acc_ref[h] / jnp.sum(l_ref[h], axis=-1, keepdims=True) for h in range(len(heads))]


def _qk(q, k):
    return lax.dot_general(q, k, (((1,), (1,)), ((), ())), preferred_element_type=F32)


def _tile_extent(n_past, tq, tk, single_tile):
    if single_tile:
        return n_past // tk, n_past
    start = n_past + pl.program_id(2) * tq
    return start // tk, pl.multiple_of(start, tq)


def _fox_kernel(q_ref, k_ref, v_ref, cq_ref, ck_ref, o_ref, m_ref, l_ref, acc_ref, *, tq, tk, n_past, single_tile):
    lane = lax.broadcasted_iota(jnp.int32, (tq, LANES), 1)
    n_open, d0 = _tile_extent(n_past, tq, tk, single_tile)
    row = lax.broadcasted_iota(jnp.int32, (tq, tq), 0)
    col = lax.broadcasted_iota(jnp.int32, (tq, tq), 1)
    heads = []
    for pair in range(FOX_PAIRS_PER_STEP):
        lanes = slice(pair * LANES, (pair + 1) * LANES)
        q = q_ref[:, lanes]
        key_chunk = functools.partial(lambda k0, width, lanes: k_ref[pl.ds(k0, width), lanes], lanes=lanes)
        value_chunk = functools.partial(lambda k0, width, lanes: v_ref[pl.ds(k0, width), lanes], lanes=lanes)
        for j in range(2):
            qj = jnp.where((lane >= j * D_HA) & (lane < (j + 1) * D_HA), q, jnp.zeros_like(q))
            col_bias = functools.partial(
                lambda k0, width, pair, j: -ck_ref[pair, j:j + 1, pl.ds(k0, width)], pair=pair, j=j)
            row_bias = jnp.broadcast_to(cq_ref[pair, :, j:j + 1], (tq, LANES))
            heads.append((qj, key_chunk, value_chunk, col_bias, row_bias))
    outs = _flash_sweep(heads, n_open, d0, tq, tk, col <= row, m_ref, l_ref, acc_ref)
    for pair in range(FOX_PAIRS_PER_STEP):
        o_ref[:, pair * LANES:(pair + 1) * LANES] = jnp.where(lane < D_HA, outs[2 * pair], outs[2 * pair + 1])


def _fox_attention(q, k, v, c_q, c_k, *, n_past, tq, tk):
    b, t_q, _ = q.shape
    t_k = k.shape[1]
    t_kp = c_k.shape[-1]
    assert t_q % tq == 0 and n_past % tk == 0 and (t_q == tq or tq % tk == 0)
    npair = FOX_PAIRS_PER_STEP
    kern = functools.partial(_fox_kernel, tq=tq, tk=tk, n_past=n_past, single_tile=t_q == tq)
    return pl.pallas_call(
        kern,
        grid=(b, H_A // (2 * npair), t_q // tq),
        in_specs=[pl.BlockSpec((None, tq, npair * LANES), lambda bi, hp, qi: (bi, qi, hp)),
                  pl.BlockSpec((None, t_k, npair * LANES), lambda bi, hp, qi: (bi, 0, hp)),
                  pl.BlockSpec((None, t_k, npair * LANES), lambda bi, hp, qi: (bi, 0, hp)),
                  pl.BlockSpec((None, npair, tq, 2), lambda bi, hp, qi: (bi, hp, qi, 0)),
                  pl.BlockSpec((None, npair, 2, t_kp), lambda bi, hp, qi: (bi, hp, 0, 0))],
        out_specs=pl.BlockSpec((None, tq, npair * LANES), lambda bi, hp, qi: (bi, qi, hp)),
        out_shape=jax.ShapeDtypeStruct((b, t_q, W_A), F32),
        scratch_shapes=[pltpu.VMEM((2 * npair, tq, LANES), F32)] * 3,
        compiler_params=_cparams("parallel", "parallel", "arbitrary"),
        name="fox_attention",
    )(q, k, v, c_q, c_k)


def _mla_kernel(q_ref, k_ref, v_ref, o_ref, m_ref, l_ref, acc_ref, *, tq, tk, n_past, single_tile):
    n_open, d0 = _tile_extent(n_past, tq, tk, single_tile)
    row = lax.broadcasted_iota(jnp.int32, (tq, tq), 0)
    col = lax.broadcasted_iota(jnp.int32, (tq, tq), 1)
    heads = []
    for j in range(MLA_HEADS_PER_STEP):
        key_chunk = functools.partial(
            lambda k0, width, j: k_ref[pl.ds(k0, width), j * MLA_SLAB:(j + 1) * MLA_SLAB], j=j)
        value_chunk = functools.partial(lambda k0, width, j: v_ref[pl.ds(k0, width), j * V_HD:(j + 1) * V_HD], j=j)
        heads.append((q_ref[:, j * MLA_SLAB:(j + 1) * MLA_SLAB], key_chunk, value_chunk, None, None))
    mask = (d0 + col) >> _CHUNK_SHIFT <= (d0 + row) >> _CHUNK_SHIFT
    outs = _flash_sweep(heads, n_open, d0, tq, tk, mask, m_ref, l_ref, acc_ref)
    for j in range(MLA_HEADS_PER_STEP):
        o_ref[:, j * V_HD:(j + 1) * V_HD] = outs[j]


def _mla_attention(q, k, v, *, n_past, tq, tk):
    b, t_q, _ = q.shape
    t_k = k.shape[1]
    assert t_q % tq == 0 and n_past % tk == 0 and (t_q == tq or tq % tk == 0)
    nh = MLA_HEADS_PER_STEP
    kern = functools.partial(_mla_kernel, tq=tq, tk=tk, n_past=n_past, single_tile=t_q == tq)
    return pl.pallas_call(
        kern,
        grid=(b, H_B // nh, t_q // tq),
        in_specs=[pl.BlockSpec((None, tq, nh * MLA_SLAB), lambda bi, hd, qi: (bi, qi, hd)),
                  pl.BlockSpec((None, t_k, nh * MLA_SLAB), lambda bi, hd, qi: (bi, 0, hd)),
                  pl.BlockSpec((None, t_k, nh * V_HD), lambda bi, hd, qi: (bi, 0, hd))],
        out_specs=pl.BlockSpec((None, tq, nh * V_HD), lambda bi, hd, qi: (bi, qi, hd)),
        out_shape=jax.ShapeDtypeStruct((b, t_q, W_B), F32),
        scratch_shapes=[pltpu.VMEM((nh, tq, LANES), F32), pltpu.VMEM((nh, tq, LANES), F32),
                        pltpu.VMEM((nh, tq, V_HD), F32)],
        compiler_params=_cparams("parallel", "parallel", "arbitrary"),
        name="mla_attention",
    )(q, k, v)


def _kv_up_kernel(ckv_ref, kr_ref, wuk_ref, wuv_ref, k_ref, v_ref):
    c = ckv_ref[...].astype(BF16)
    kr = kr_ref[...].astype(BF16)
    kn = jnp.dot(c, wuk_ref[...], preferred_element_type=F32).astype(BF16)
    for hd in range(H_B):
        k_ref[:, hd * MLA_SLAB:hd * MLA_SLAB + NOPE] = kn[:, hd * NOPE:(hd + 1) * NOPE]
        k_ref[:, hd * MLA_SLAB + NOPE:(hd + 1) * MLA_SLAB] = kr
    v_ref[...] = jnp.dot(c, wuv_ref[...], preferred_element_type=F32).astype(BF16)


def _kv_up(ckv_n, krope128, w_uk, w_uv):
    r = ckv_n.shape[0]
    tm = ROW_TILE
    row = lambda w: pl.BlockSpec((tm, w), lambda i: (i, 0))
    return pl.pallas_call(
        _kv_up_kernel, grid=(r // tm,),
        in_specs=[row(KV_RANK), row(LANES), _const_spec((KV_RANK, H_B * NOPE)),
                  _const_spec((KV_RANK, H_B * V_HD))],
        out_specs=[row(H_B * MLA_SLAB), row(H_B * V_HD)],
        out_shape=(jax.ShapeDtypeStruct((r, H_B * MLA_SLAB), BF16), jax.ShapeDtypeStruct((r, H_B * V_HD), BF16)),
        compiler_params=_cparams("parallel"), name="mla_kv_up",
    )(ckv_n, krope128, w_uk, w_uv)


_CONV_SUB = 64
_CONV_HALO = -(-CONV_W // SUBLANES) * SUBLANES


def _conv_kernel(xp_ref, w_ref, b_ref, g_ref, beta_ref, o_ref, *, tt):
    t0 = pl.program_id(1) * tt
    w = w_ref[...]
    rows = min(_CONV_SUB, tt)
    for sub in range(tt // rows):
        base = pl.multiple_of(t0 + sub * rows, rows)
        xa = xp_ref[pl.ds(base, rows + _CONV_HALO), :]
        acc = None
        for b in range(SUBLANES):
            z = None
            for a in range(-(-CONV_W // SUBLANES)):
                tap = SUBLANES * a + b
                if tap < CONV_W:
                    term = xa[SUBLANES * a:SUBLANES * a + rows + SUBLANES, :] * w[tap:tap + 1, :]
                    z = term if z is None else z + term
            acc = z[b:b + rows, :] if acc is None else acc + z[b:b + rows, :]
        y = acc + b_ref[...]
        mu = jnp.mean(y, axis=-1, keepdims=True)
        yc = y - mu
        var = jnp.mean(yc * yc, axis=-1, keepdims=True)
        y = yc * lax.rsqrt(var + EPS) * g_ref[...] + beta_ref[...]
        o_ref[sub * rows:(sub + 1) * rows, :] = y * jax.nn.sigmoid(y)


def _conv_module(xp, w_dw, b_dw, ln_g, ln_b, *, t, tt):
    b, t_p, _ = xp.shape
    kern = functools.partial(_conv_kernel, tt=tt)
    return pl.pallas_call(
        kern, grid=(b, t // tt),
        in_specs=[pl.BlockSpec((None, t_p, C_C), lambda bi, ti: (bi, 0, 0)),
                  _const_spec((CONV_W, C_C)), _const_spec((1, C_C)), _const_spec((1, C_C)), _const_spec((1, C_C))],
        out_specs=pl.BlockSpec((None, tt, C_C), lambda bi, ti: (bi, ti, 0)),
        out_shape=jax.ShapeDtypeStruct((b, t, C_C), F32),
        compiler_params=_cparams("parallel", "arbitrary"), name="conv_module",
    )(xp, w_dw, b_dw, ln_g, ln_b)


def _out_proj_kernel(oa_ref, ob_ref, oc_ref, x_ref, gout_ref, wout_ref, gffn_ref, wr_ref, br_ref,
                     xn_ref, h2_ref, rl_ref):
    g = gout_ref[...]
    na = _rms(oa_ref[...], g[:, :W_A]).astype(BF16)
    nb = _rms(ob_ref[...], g[:, W_A:W_A + W_B]).astype(BF16)
    nc = _rms(oc_ref[...], g[:, W_A + W_B:]).astype(BF16)
    mix = jnp.dot(na, wout_ref[:W_A, :], preferred_element_type=F32)
    mix = mix + jnp.dot(nb, wout_ref[W_A:W_A + W_B, :], preferred_element_type=F32)
    mix = mix + jnp.dot(nc, wout_ref[W_A + W_B:, :], preferred_element_type=F32)
    xn = x_ref[...] + mix
    xn_ref[...] = xn
    h2 = _rms(xn, gffn_ref[...]).astype(BF16)
    rl_ref[...] = jnp.dot(h2, wr_ref[...], preferred_element_type=F32) + br_ref[...]
    bits = pltpu.bitcast(h2.astype(F32), jnp.uint32)
    h2_ref[...] = (bits[:, :D_MODEL // 2] >> 16) | (bits[:, D_MODEL // 2:] & jnp.uint32(0xFFFF0000))


def _out_proj(o_a, o_b, o_c, x, g_out, w_out, g_ffn, w_r, b_r):
    n = x.shape[0]
    tm = ROW_TILE
    row = lambda w: pl.BlockSpec((tm, w), lambda i: (i, 0))
    return pl.pallas_call(
        _out_proj_kernel, grid=(n // tm,),
        in_specs=[row(W_A), row(W_B), row(C_C), row(D_MODEL), _const_spec((1, D_MODEL)),
                  _const_spec((D_MODEL, D_MODEL)), _const_spec((1, D_MODEL)),
                  _const_spec((D_MODEL, LANES)), _const_spec((1, LANES))],
        out_specs=[row(D_MODEL), row(D_MODEL // 2), row(LANES)],
        out_shape=(jax.ShapeDtypeStruct((n, D_MODEL), F32), jax.ShapeDtypeStruct((n, D_MODEL // 2), jnp.uint32),
                   jax.ShapeDtypeStruct((n, LANES), F32)),
        compiler_params=_cparams("parallel"), name="out_proj_router",
    )(o_a, o_b, o_c, x, g_out, w_out, g_ffn, w_r, b_r)


def _expert_kernel(be_ref, nv_ref, src_ref, dst_ref, h_hbm, wg_ref, wu_ref, wd_ref, y_hbm,
                   xbuf, ybuf, zbuf, wgb_ref, wub_ref, wdb_ref, gsem, ssem, zsem, *, n_blocks, dump_row0):
    te = EXPERT_TILE
    xt = D_MODEL // 2 // LANES
    yt = D_MODEL // LANES
    i = pl.program_id(0)
    nv = nv_ref[0]
    cur = i % 2
    nxt = 1 - cur
    valid = i < nv

    def gather_copy(tok, buf, r):
        return pltpu.make_async_copy(h_hbm.at[tok], xbuf.at[buf, r // SUBLANES, :, r % SUBLANES, :], gsem.at[buf])

    def scatter_copy(row, buf, r):
        return pltpu.make_async_copy(ybuf.at[buf, r // SUBLANES, :, r % SUBLANES, :], y_hbm.at[row], ssem.at[buf])

    def gather_block(block, buf):
        for r in range(te):
            gather_copy(src_ref[block * te + r], buf, r).start(priority=1)

    def scatter_block(block, buf):
        for r in range(te):
            scatter_copy(dst_ref[block * te + r], buf, r).start(priority=r % 2)

    def wait_gather(buf):
        for r in range(te):
            gather_copy(0, buf, 0).wait()

    def wait_scatter(buf):
        for r in range(te):
            scatter_copy(0, buf, 0).wait()

    @pl.when(i == 0)
    def _():
        gather_block(0, 0)
        zbuf[...] = jnp.zeros(zbuf.shape, F32)
        fills = [pltpu.make_async_copy(zbuf, y_hbm.at[pl.ds(dump_row0 + k * SUBLANES, SUBLANES)], zsem)
                 for k in range(EXPERT_DUMP_ROWS // SUBLANES)]
        for fill in fills:
            fill.start()
        for fill in fills:
            fill.wait()

    prev_e = be_ref[jnp.maximum(i - 1, 0)]

    @pl.when(valid & ((i == 0) | (be_ref[i] != prev_e)))
    def _():
        wgb_ref[...] = wg_ref[...].astype(BF16)
        wub_ref[...] = wu_ref[...].astype(BF16)
        wdb_ref[...] = wd_ref[...].astype(BF16)

    @pl.when(valid)
    def _():
        wait_gather(cur)

    @pl.when(valid & (i >= 2))
    def _():
        wait_scatter(cur)

    def step(scatter_previous):
        gather_block(jnp.minimum(i + 1, n_blocks - 1), nxt)
        if scatter_previous:
            scatter_block(i - 1, nxt)
        u = jnp.concatenate([xbuf[cur, :, j].reshape(te, LANES) for j in range(xt)], axis=1)
        x_lo = pltpu.bitcast(u << 16, F32).astype(BF16)
        x_hi = pltpu.bitcast(u & jnp.uint32(0xFFFF0000), F32).astype(BF16)
        half = D_MODEL // 2
        gate = (jnp.dot(x_lo, wgb_ref[:half, :], preferred_element_type=F32)
                + jnp.dot(x_hi, wgb_ref[half:, :], preferred_element_type=F32))
        up = (jnp.dot(x_lo, wub_ref[:half, :], preferred_element_type=F32)
              + jnp.dot(x_hi, wub_ref[half:, :], preferred_element_type=F32))
        hid = (gate * jax.nn.sigmoid(gate) * up).astype(BF16)
        y = jnp.dot(hid, wdb_ref[...], preferred_element_type=F32)
        for j in range(yt):
            ybuf[cur, :, j] = y[:, j * LANES:(j + 1) * LANES].reshape(te // SUBLANES, SUBLANES, LANES)

    @pl.when(valid & (i == 0))
    def _():
        step(False)

    @pl.when(valid & (i > 0))
    def _():
        step(True)

    @pl.when(i == nv - 1)
    def _():
        scatter_block(i, cur)
        wait_scatter(cur)
        wait_gather(nxt)

    @pl.when((i == nv - 1) & (i > 0))
    def _():
        wait_scatter(nxt)


def _expert_blocks(block_e, n_valid, slot_src, slot_dst, h2_tiles, w_gate, w_up, w_down, *, layer):
    n_tok = h2_tiles.shape[0]
    te = EXPERT_TILE
    n_blocks = block_e.shape[0]
    xt, yt = D_MODEL // 2 // LANES, D_MODEL // LANES
    weights = lambda r, c: pl.BlockSpec((None, None, r, c), lambda i, be, nv, src, dst: (layer, be[i], 0, 0))
    grid_spec = pltpu.PrefetchScalarGridSpec(
        num_scalar_prefetch=4, grid=(n_blocks,),
        in_specs=[pl.BlockSpec(memory_space=pl.ANY), weights(D_MODEL, D_EXPERT), weights(D_MODEL, D_EXPERT),
                  weights(D_EXPERT, D_MODEL)],
        out_specs=pl.BlockSpec(memory_space=pl.ANY),
        scratch_shapes=[pltpu.VMEM((2, te // SUBLANES, xt, SUBLANES, LANES), jnp.uint32),
                        pltpu.VMEM((2, te // SUBLANES, yt, SUBLANES, LANES), F32),
                        pltpu.VMEM((SUBLANES, yt, LANES), F32),
                        pltpu.VMEM((D_MODEL, D_EXPERT), BF16), pltpu.VMEM((D_MODEL, D_EXPERT), BF16),
                        pltpu.VMEM((D_EXPERT, D_MODEL), BF16),
                        pltpu.SemaphoreType.DMA((2,)), pltpu.SemaphoreType.DMA((2,)), pltpu.SemaphoreType.DMA(())])
    kern = functools.partial(_expert_kernel, n_blocks=n_blocks, dump_row0=TOP_K * n_tok)
    return pl.pallas_call(
        kern, grid_spec=grid_spec,
        out_shape=jax.ShapeDtypeStruct((TOP_K * n_tok + EXPERT_DUMP_ROWS, yt, LANES), F32),
        compiler_params=_cparams("arbitrary"), name="expert_blocks",
    )(block_e, n_valid, slot_src, slot_dst, h2_tiles, w_gate, w_up, w_down)


def _route(rlog):
    n = rlog.shape[0]
    idx = jnp.arange(n)
    g_logits = rlog[:, :N_GROUPS]
    g_sel = jnp.argmax(g_logits, axis=-1)
    g_w = jax.nn.softmax(g_logits, axis=-1)[idx, g_sel]
    e_logits = rlog[:, N_GROUPS:N_GROUPS + N_EXPERTS].reshape(n, N_GROUPS, EXPERTS_PER_GROUP)[idx, g_sel]
    top_v, top_i = lax.top_k(e_logits, TOP_K)
    gate = jax.nn.softmax(top_v, axis=-1) * g_w[:, None]
    expert_ids = (g_sel[:, None] * EXPERTS_PER_GROUP + top_i).astype(jnp.int32)
    return expert_ids, gate


def _dispatch_plan(expert_ids):
    n = expert_ids.shape[0]
    te = EXPERT_TILE
    a = n * TOP_K
    n_blocks = -(-a // te) + N_EXPERTS
    rows = n_blocks * te
    flat_e = expert_ids.reshape(-1)
    experts = jnp.arange(N_EXPERTS, dtype=jnp.int32)
    counts = jnp.sum((flat_e[:, None] == experts[None, :]).astype(jnp.int32), axis=0)
    padded = (counts + te - 1) // te * te
    pad_end = jnp.cumsum(padded)
    pad_start = pad_end - padded
    start = jnp.cumsum(counts) - counts
    order = jnp.argsort(flat_e).astype(jnp.int32)
    slot = jnp.arange(rows, dtype=jnp.int32)
    slot_e = jnp.minimum(jnp.sum((pad_end[None, :] <= slot[:, None]).astype(jnp.int32), axis=-1), N_EXPERTS - 1)
    slot_hot = (slot_e[:, None] == experts[None, :]).astype(jnp.int32)
    slot_rank = slot - jnp.sum(slot_hot * pad_start[None, :], axis=-1)
    filled = slot_rank < jnp.sum(slot_hot * counts[None, :], axis=-1)
    src = jnp.where(filled, jnp.sum(slot_hot * start[None, :], axis=-1) + slot_rank, 0)
    slot_a = order[src]
    slot_src = jnp.where(filled, slot_a // TOP_K, 0)
    slot_dst = jnp.where(filled, (slot_a % TOP_K) * n + slot_a // TOP_K, a + slot % EXPERT_DUMP_ROWS)
    block_e = slot_e.reshape(n_blocks, te)[:, 0]
    n_valid = (pad_end[-1] // te).astype(jnp.int32).reshape(1)
    return slot_src, slot_dst, block_e, n_valid


def _combine_kernel(x_ref, y0_ref, y1_ref, gate_ref, gfin_ref, o_ref, z_ref, *, final):
    z_ref[...] = y0_ref[...] * gate_ref[:, 0:1, :] + y1_ref[...] * gate_ref[:, 1:2, :]
    x = jnp.concatenate([x_ref[:, j * LANES:(j + 1) * LANES] + z_ref[:, j, :] for j in range(D_MODEL // LANES)],
                        axis=1)
    o_ref[...] = _rms(x, gfin_ref[...]) if final else x


def _combine(x, ys, gate, g_final, *, row0, n_tok, final):
    n = x.shape[0]
    tm = ROW_TILE
    assert row0 % tm == 0 and n_tok % tm == 0
    row = pl.BlockSpec((tm, D_MODEL), lambda i: (i, 0))
    expert_rows = lambda k: pl.BlockSpec((tm, D_MODEL // LANES, LANES),
                                         lambda i: (i + (row0 + k * n_tok) // tm, 0, 0))
    return pl.pallas_call(
        functools.partial(_combine_kernel, final=final), grid=(n // tm,),
        in_specs=[row, expert_rows(0), expert_rows(1), pl.BlockSpec((tm, TOP_K, LANES), lambda i: (i, 0, 0)),
                  _const_spec((1, D_MODEL))],
        out_specs=row, out_shape=jax.ShapeDtypeStruct((n, D_MODEL), F32),
        scratch_shapes=[pltpu.VMEM((tm, D_MODEL // LANES, LANES), F32)],
        compiler_params=_cparams("parallel"), name="moe_combine",
    )(x, ys, ys, gate, g_final)


def _rope_tables(pos):
    half = ROPE // 2
    inv = ROPE_THETA ** (-jnp.arange(half, dtype=F32) / half)
    ang = pos.astype(F32)[:, None] * inv[None, :]
    zeros = jnp.zeros((pos.shape[0], LANES - ROPE), F32)
    cos = jnp.concatenate([jnp.cos(ang), jnp.cos(ang), zeros], axis=-1)
    sin = jnp.concatenate([-jnp.sin(ang), jnp.sin(ang), zeros], axis=-1)
    return cos, sin


def _prep_w_in(w_in):
    qa, ka, va, fa, cq, ckv, kr, ua, ug = jnp.split(
        w_in, np.cumsum([W_A, W_A, W_A, H_A, Q_RANK, KV_RANK, ROPE, C_C])[:].tolist(), axis=1)
    pad = jnp.zeros((D_MODEL, LANES - ROPE - H_A), w_in.dtype)
    return jnp.concatenate([qa * (D_HA ** -0.5), ka, va, cq, ckv, ua, ug, kr, fa, pad], axis=1).astype(BF16)


def _prep_w_uq(w_uq):
    w = w_uq.reshape(Q_RANK, H_B, NOPE + ROPE)
    w = jnp.pad(w, ((0, 0), (0, 0), (0, MLA_SLAB - NOPE - ROPE)))
    return w.reshape(Q_RANK, H_B * MLA_SLAB).astype(BF16)


def _prep_w_ukv(w_ukv):
    w = w_ukv.reshape(KV_RANK, H_B, NOPE + V_HD)
    return (w[:, :, :NOPE].reshape(KV_RANK, H_B * NOPE).astype(BF16),
            w[:, :, NOPE:].reshape(KV_RANK, H_B * V_HD).astype(BF16))


def _pad_lanes(x, width):
    return jnp.pad(x, ((0, 0),) * (x.ndim - 1) + ((0, width - x.shape[-1]),))


def _round_up(x, m):
    return -(-x // m) * m


def kernel(x_prompt, x_sample, cache_fox_k, cache_fox_v, cache_fox_logf, cache_mla_ckv, cache_mla_krope, state_conv, g_mix, w_in, b_f, g_q, w_uq, g_kv, w_ukv, w_dw, b_dw, ln_g, ln_b, g_out, w_out, g_ffn, w_rg, b_rg, w_re, b_re, w_gate, w_up, w_down, g_final):
    bp, tp, _ = x_prompt.shape
    bs, ts, _ = x_sample.shape
    n_past = cache_fox_k.shape[2]
    n_p, n_s = bp * tp, bs * ts
    depth = g_mix.shape[0]

    halo = CONV_W - 1
    tkp_s = _round_up(n_past + ts, LANES)

    x_p = x_prompt.reshape(n_p, D_MODEL)
    x_s = x_sample.reshape(n_s, D_MODEL)
    cos_p, sin_p = _rope_tables(jnp.arange(tp))
    cos_s, sin_s = _rope_tables(n_past + jnp.arange(ts))
    rope_p = (jnp.tile(cos_p, (bp, 1)), jnp.tile(sin_p, (bp, 1)))
    rope_s = (jnp.tile(cos_s, (bs, 1)), jnp.tile(sin_s, (bs, 1)))

    def split_heads(c):
        return c.reshape(c.shape[0], H_A // 2, 2, c.shape[-1])

    states_p, states_s = [], []
    for l in range(depth):
        w_uk, w_uv = _prep_w_ukv(w_ukv[l])
        b_f128 = jnp.pad(b_f[l], (_FA_LANE, LANES - _FA_LANE - H_A))[None, :]
        in_w = (g_mix[l][None], _prep_w_in(w_in[l]), g_q[l][None], g_kv[l][None], _prep_w_uq(w_uq[l]), b_f128)
        qa_p, ka_p, va_p, kab_p, vab_p, lf_p, qm_p, ckv_p, kr_p, u_p = _in_proj(x_p, *in_w, *rope_p)
        qa_s, ka_s, va_s, kab_s, vab_s, lf_s, qm_s, ckv_s, kr_s, u_s = _in_proj(x_s, *in_w, *rope_s)

        lf_p = lf_p.reshape(bp, tp, H_A)
        lf_s = lf_s.reshape(bs, ts, H_A)
        c_p = _cumsum_time(lf_p.transpose(0, 2, 1))
        lf_all = jnp.concatenate([cache_fox_logf[l].astype(F32), lf_s], axis=1)
        c_s = _cumsum_time(_pad_lanes(lf_all.transpose(0, 2, 1), tkp_s))
        o_a_p = _fox_attention(
            qa_p.reshape(bp, tp, W_A), kab_p.reshape(bp, tp, W_A), vab_p.reshape(bp, tp, W_A),
            split_heads(c_p).transpose(0, 1, 3, 2), split_heads(c_p), n_past=0, tq=TQ_PROMPT, tk=TQ_PROMPT)
        k_all = jnp.concatenate([cache_fox_k[l].reshape(bs, n_past, W_A).astype(BF16),
                                 kab_s.reshape(bs, ts, W_A)], axis=1)
        v_all = jnp.concatenate([cache_fox_v[l].reshape(bs, n_past, W_A).astype(BF16),
                                 vab_s.reshape(bs, ts, W_A)], axis=1)
        o_a_s = _fox_attention(
            qa_s.reshape(bs, ts, W_A), k_all, v_all,
            split_heads(c_s[:, :, n_past:n_past + ts]).transpose(0, 1, 3, 2), split_heads(c_s),
            n_past=n_past, tq=ts, tk=TK_SAMPLE)

        ckv_s = ckv_s.reshape(bs, ts, KV_RANK)
        kr_s = kr_s.reshape(bs, ts, LANES)
        ckv_all = jnp.concatenate([cache_mla_ckv[l].astype(F32), ckv_s], axis=1)
        kr_all = jnp.concatenate([_pad_lanes(cache_mla_krope[l].astype(F32), LANES), kr_s], axis=1)
        km_p, vm_p = _kv_up(ckv_p, kr_p, w_uk, w_uv)
        km_s, vm_s = _kv_up(ckv_all.reshape(-1, KV_RANK), kr_all.reshape(-1, LANES), w_uk, w_uv)
        o_b_p = _mla_attention(
            qm_p.reshape(bp, tp, H_B * MLA_SLAB), km_p.reshape(bp, tp, H_B * MLA_SLAB), vm_p.reshape(bp, tp, W_B),
            n_past=0, tq=TQ_PROMPT, tk=TQ_PROMPT)
        o_b_s = _mla_attention(
            qm_s.reshape(bs, ts, H_B * MLA_SLAB), km_s.reshape(bs, n_past + ts, H_B * MLA_SLAB),
            vm_s.reshape(bs, n_past + ts, W_B), n_past=n_past, tq=ts, tk=TK_SAMPLE)

        xp_p = jnp.pad(u_p.reshape(bp, tp, C_C), ((0, 0), (halo, _CONV_HALO - halo), (0, 0)))
        xp_s = jnp.concatenate([state_conv[l].astype(F32), u_s.reshape(bs, ts, C_C)], axis=1)
        conv_p = xp_p[:, tp:tp + halo]
        conv_s = xp_s[:, ts:ts + halo]
        xp_s = jnp.pad(xp_s, ((0, 0), (0, _CONV_HALO - halo), (0, 0)))
        conv_w = (w_dw[l], b_dw[l][None], ln_g[l][None], ln_b[l][None])
        o_c_p = _conv_module(xp_p, *conv_w, t=tp, tt=CONV_TILE)
        o_c_s = _conv_module(xp_s, *conv_w, t=ts, tt=ts)

        w_r = _pad_lanes(jnp.concatenate([w_rg[l], w_re[l]], axis=1), LANES).astype(BF16)
        b_r = _pad_lanes(jnp.concatenate([b_rg[l], b_re[l].reshape(-1)])[None, :].astype(F32), LANES)
        out_w = (g_out[l][None], w_out[l].astype(BF16), g_ffn[l][None], w_r, b_r)
        x_p, h2_p, rlog_p = _out_proj(o_a_p.reshape(n_p, W_A), o_b_p.reshape(n_p, W_B), o_c_p.reshape(n_p, C_C),
                                      x_p, *out_w)
        x_s, h2_s, rlog_s = _out_proj(o_a_s.reshape(n_s, W_A), o_b_s.reshape(n_s, W_B), o_c_s.reshape(n_s, C_C),
                                      x_s, *out_w)

        expert_ids, gate = _route(jnp.concatenate([rlog_p, rlog_s], axis=0))
        slot_src, slot_dst, block_e, n_valid = _dispatch_plan(expert_ids)
        h2_tiles = jnp.concatenate([h2_p, h2_s], axis=0).reshape(n_p + n_s, D_MODEL // 2 // LANES, LANES)
        ys = _expert_blocks(block_e, n_valid, slot_src, slot_dst, h2_tiles, w_gate, w_up, w_down, layer=l)
        final = l == depth - 1
        gate = jnp.broadcast_to(gate[:, :, None], (n_p + n_s, TOP_K, LANES))
        x_p = _combine(x_p, ys, gate[:n_p], g_final[None], row0=0, n_tok=n_p + n_s, final=final)
        x_s = _combine(x_s, ys, gate[n_p:], g_final[None], row0=n_p, n_tok=n_p + n_s, final=final)

        states_p.append((ka_p.reshape(bp, tp, H_A, D_HA), va_p.reshape(bp, tp, H_A, D_HA), lf_p,
                         ckv_p.reshape(bp, tp, KV_RANK), kr_p[:, :ROPE].reshape(bp, tp, ROPE), conv_p))
        states_s.append((ka_s.reshape(bs, ts, H_A, D_HA), va_s.reshape(bs, ts, H_A, D_HA), lf_s,
                         ckv_s, kr_s[:, :, :ROPE], conv_s))

    y_p = x_p.reshape(bp, tp, D_MODEL)
    y_s = x_s.reshape(bs, ts, D_MODEL)
    p_out = [jnp.stack(a) for a in zip(*states_p)]
    s_out = [jnp.stack(a) for a in zip(*states_s)]
    return (y_p, y_s, *p_out, *s_out)
```

```python
import functools

import numpy as np
import jax
import jax.numpy as jnp
from jax import lax
from jax.experimental import pallas as pl
from jax.experimental.pallas import tpu as pltpu

F32 = jnp.float32
BF16 = jnp.bfloat16

D_MODEL = 2048
DEPTH = 4
CHUNK = 64
H_A, D_HA = 8, 64
W_A = H_A * D_HA
H_B, NOPE, ROPE, V_HD = 8, 128, 64, 128
Q_RANK, KV_RANK = 512, 256
W_B = H_B * V_HD
C_C = D_MODEL - W_A - W_B
CONV_W = 31
_CHUNK_SHIFT = CHUNK.bit_length() - 1
assert 1 << _CHUNK_SHIFT == CHUNK
ROPE_THETA = 10000.0
N_GROUPS, EXPERTS_PER_GROUP = 4, 8
N_EXPERTS = N_GROUPS * EXPERTS_PER_GROUP
TOP_K = 2
D_EXPERT = 512
EPS = 1e-6
NEG_INF = -1e30
LOG2E = 1.4426950408889634

LANES = 128
SUBLANES = 8
TQ_PROMPT = 512
TK_SAMPLE = 512
MLA_HEADS_PER_STEP = 4
FOX_PAIRS_PER_STEP = 2
CONV_TILE = 256
MLA_SLAB = NOPE + LANES
ROW_TILE = 256
EXPERT_TILE = 256
EXPERT_DUMP_ROWS = 2 * EXPERT_TILE
VMEM_LIMIT = 56 * 1024 * 1024

_C_QA, _C_KA, _C_VA, _C_CQ, _C_CKV, _C_UA, _C_UG, _C_KRF = 0, 512, 1024, 1536, 2048, 2304, 2816, 3328
IN_COLS = _C_KRF + LANES
_FA_LANE = ROPE


def _cparams(*sem):
    return pltpu.CompilerParams(dimension_semantics=sem, vmem_limit_bytes=VMEM_LIMIT)


def _const_spec(shape):
    nd = len(shape)
    return pl.BlockSpec(shape, lambda *_: (0,) * nd)


def _rms(x, g):
    return x * lax.rsqrt(jnp.mean(x * x, axis=-1, keepdims=True) + EPS) * g


def _rope_block(x, cos, sin_signed):
    lane = lax.broadcasted_iota(jnp.int32, x.shape, 1)
    partner = jnp.where(lane < ROPE // 2, pltpu.roll(x, LANES - ROPE // 2, 1), pltpu.roll(x, ROPE // 2, 1))
    return x * cos + partner * sin_signed


def _in_proj_kernel(x_ref, gmix_ref, win_ref, gq_ref, gkv_ref, wuq_ref, bf_ref, cos_ref, sin_ref,
                    qa_ref, ka_ref, va_ref, kab_ref, vab_ref, logf_ref, qmla_ref, ckv_ref, kr_ref, u_ref):
    h = _rms(x_ref[...], gmix_ref[...]).astype(BF16)

    def proj(c0, width):
        return jnp.dot(h, win_ref[:, c0:c0 + width], preferred_element_type=F32)

    qa_ref[...] = (proj(_C_QA, W_A) * LOG2E).astype(BF16)
    ka = proj(_C_KA, W_A)
    ka_ref[...] = ka
    kab_ref[...] = ka.astype(BF16)
    va = proj(_C_VA, W_A)
    va_ref[...] = va
    vab_ref[...] = va.astype(BF16)

    cos = cos_ref[...]
    sin = sin_ref[...]
    krf = proj(_C_KRF, LANES)
    kr_ref[...] = _rope_block(krf, cos, sin)
    z = krf + bf_ref[...]
    logsig = jnp.minimum(z, 0.0) - jnp.log(1.0 + jnp.exp(-jnp.abs(z)))
    logf_ref[...] = logsig[:, _FA_LANE:_FA_LANE + H_A]

    ckv_ref[...] = _rms(proj(_C_CKV, KV_RANK), gkv_ref[...])

    u_ref[...] = proj(_C_UA, C_C) * jax.nn.sigmoid(proj(_C_UG, C_C))

    cqn = _rms(proj(_C_CQ, Q_RANK), gq_ref[...]).astype(BF16)
    scale = (NOPE + ROPE) ** -0.5 * LOG2E
    for hd in range(H_B):
        q = jnp.dot(cqn, wuq_ref[:, hd * MLA_SLAB:(hd + 1) * MLA_SLAB], preferred_element_type=F32)
        qmla_ref[:, hd * MLA_SLAB:hd * MLA_SLAB + NOPE] = (q[:, :NOPE] * scale).astype(BF16)
        qmla_ref[:, hd * MLA_SLAB + NOPE:(hd + 1) * MLA_SLAB] = (
            _rope_block(q[:, NOPE:], cos, sin) * scale).astype(BF16)


def _in_proj(x, g_mix, w_in_p, g_q, g_kv, w_uq_p, b_f128, cos, sin):
    n = x.shape[0]
    tm = ROW_TILE
    row = lambda w: pl.BlockSpec((tm, w), lambda i: (i, 0))
    out_shapes = (
        jax.ShapeDtypeStruct((n, W_A), BF16),
        jax.ShapeDtypeStruct((n, W_A), F32),
        jax.ShapeDtypeStruct((n, W_A), F32),
        jax.ShapeDtypeStruct((n, W_A), BF16),
        jax.ShapeDtypeStruct((n, W_A), BF16),
        jax.ShapeDtypeStruct((n, H_A), F32),
        jax.ShapeDtypeStruct((n, H_B * MLA_SLAB), BF16),
        jax.ShapeDtypeStruct((n, KV_RANK), F32),
        jax.ShapeDtypeStruct((n, LANES), F32),
        jax.ShapeDtypeStruct((n, C_C), F32),
    )
    return pl.pallas_call(
        _in_proj_kernel,
        grid=(n // tm,),
        in_specs=[row(D_MODEL), _const_spec((1, D_MODEL)), _const_spec((D_MODEL, IN_COLS)),
                  _const_spec((1, Q_RANK)), _const_spec((1, KV_RANK)),
                  _const_spec((Q_RANK, H_B * MLA_SLAB)), _const_spec((1, LANES)), row(LANES), row(LANES)],
        out_specs=[row(W_A), row(W_A), row(W_A), row(W_A), row(W_A), row(H_A), row(H_B * MLA_SLAB),
                   row(KV_RANK), row(LANES), row(C_C)],
        out_shape=out_shapes,
        compiler_params=_cparams("parallel"),
        name="in_proj",
    )(x, g_mix, w_in_p, g_q, g_kv, w_uq_p, b_f128, cos, sin)


def _cumsum_kernel(x_ref, o_ref):
    c = x_ref[...]
    t = c.shape[-1]
    lane = lax.broadcasted_iota(jnp.int32, c.shape, 1)
    s = 1
    while s < t:
        c = c + jnp.where(lane >= s, pltpu.roll(c, s, 1), 0.0)
        s *= 2
    o_ref[...] = c * LOG2E


def _cumsum_time(logf_t):
    b, hh, t = logf_t.shape
    spec = pl.BlockSpec((None, hh, t), lambda i: (i, 0, 0))
    return pl.pallas_call(
        _cumsum_kernel, grid=(b,), in_specs=[spec], out_specs=spec,
        out_shape=jax.ShapeDtypeStruct((b, hh, t), F32),
        compiler_params=_cparams("parallel"), name="cumsum_logf",
    )(logf_t)


def _softmax_step(s, v, row_bias, m_ref, l_ref, acc_ref):
    width = s.shape[1]
    s_max = jnp.max(s, axis=-1, keepdims=True)
    m_old = m_ref[...]
    m_new = jnp.maximum(m_old, s_max if row_bias is None else s_max + row_bias)
    alpha = jnp.exp2(m_old - m_new)
    offset = m_new if row_bias is None else m_new - row_bias
    if width % LANES == 0:
        p = [jnp.exp2(s[:, c * LANES:(c + 1) * LANES] - offset) for c in range(width // LANES)]
        l_add = functools.reduce(lambda a, b: a + b, p)
        p = p[0].astype(BF16) if len(p) == 1 else jnp.concatenate([x.astype(BF16) for x in p], axis=1)
    else:
        p = jnp.exp2(s - offset[:, :1])
        lane = lax.broadcasted_iota(jnp.int32, m_old.shape, 1)
        l_add = jnp.where(lane == 0, jnp.sum(p, axis=-1, keepdims=True), 0.0)
        p = p.astype(BF16)
    l_ref[...] = alpha * l_ref[...] + l_add
    acc_ref[...] = alpha * acc_ref[...] + jnp.dot(p, v, preferred_element_type=F32)
    m_ref[...] = m_new


def _flash_sweep(heads, n_open, d0, tq, tk, diag_mask, m_ref, l_ref, acc_ref):
    for h in range(len(heads)):
        m_ref[h] = jnp.full(m_ref.shape[1:], NEG_INF, F32)
        l_ref[h] = jnp.zeros(l_ref.shape[1:], F32)
        acc_ref[h] = jnp.zeros(acc_ref.shape[1:], F32)

    def chunk(k0, width, mask):
        for h, (q, key_chunk, value_chunk, col_bias, row_bias) in enumerate(heads):
            s = _qk(q, key_chunk(k0, width))
            if col_bias is not None:
                s = s + col_bias(k0, width)
            if mask is not None:
                s = jnp.where(mask, s, NEG_INF)
            _softmax_step(s, value_chunk(k0, width), row_bias, m_ref.at[h], l_ref.at[h], acc_ref.at[h])

    def open_chunk(i, carry):
        chunk(pl.multiple_of(i * tk, tk), tk, None)
        return carry

    lax.fori_loop(0, n_open, open_chunk, 0)
    chunk(d0, tq, diag_mask)
    return [acc_ref[h] / jnp.sum(l_ref[h], axis=-1, keepdims=True) for h in range(len(heads))]


def _qk(q, k):
    return lax.dot_general(q, k, (((1,), (1,)), ((), ())), preferred_element_type=F32)


def _tile_extent(n_past, tq, tk, single_tile):
    if single_tile:
        return n_past // tk, n_past
    start = n_past + pl.program_id(2) * tq
    return start // tk, pl.multiple_of(start, tq)


def _fox_kernel(q_ref, k_ref, v_ref, cq_ref, ck_ref, o_ref, m_ref, l_ref, acc_ref, *, tq, tk, n_past, single_tile):
    lane = lax.broadcasted_iota(jnp.int32, (tq, LANES), 1)
    n_open, d0 = _tile_extent(n_past, tq, tk, single_tile)
    row = lax.broadcasted_iota(jnp.int32, (tq, tq), 0)
    col = lax.broadcasted_iota(jnp.int32, (tq, tq), 1)
    heads = []
    for pair in range(FOX_PAIRS_PER_STEP):
        lanes = slice(pair * LANES, (pair + 1) * LANES)
        q = q_ref[:, lanes]
        key_chunk = functools.partial(lambda k0, width, lanes: k_ref[pl.ds(k0, width), lanes], lanes=lanes)
        value_chunk = functools.partial(lambda k0, width, lanes: v_ref[pl.ds(k0, width), lanes], lanes=lanes)
        for j in range(2):
            qj = jnp.where((lane >= j * D_HA) & (lane < (j + 1) * D_HA), q, jnp.zeros_like(q))
            col_bias = functools.partial(
                lambda k0, width, pair, j: -ck_ref[pair, j:j + 1, pl.ds(k0, width)], pair=pair, j=j)
            row_bias = jnp.broadcast_to(cq_ref[pair, :, j:j + 1], (tq, LANES))
            heads.append((qj, key_chunk, value_chunk, col_bias, row_bias))
    outs = _flash_sweep(heads, n_open, d0, tq, tk, col <= row, m_ref, l_ref, acc_ref)
    for pair in range(FOX_PAIRS_PER_STEP):
        o_ref[:, pair * LANES:(pair + 1) * LANES] = jnp.where(lane < D_HA, outs[2 * pair], outs[2 * pair + 1])


def _fox_attention(q, k, v, c_q, c_k, *, n_past, tq, tk):
    b, t_q, _ = q.shape
    t_k = k.shape[1]
    t_kp = c_k.shape[-1]
    assert t_q % tq == 0 and n_past % tk == 0 and (t_q == tq or tq % tk == 0)
    npair = FOX_PAIRS_PER_STEP
    kern = functools.partial(_fox_kernel, tq=tq, tk=tk, n_past=n_past, single_tile=t_q == tq)
    return pl.pallas_call(
        kern,
        grid=(b, H_A // (2 * npair), t_q // tq),
        in_specs=[pl.BlockSpec((None, tq, npair * LANES), lambda bi, hp, qi: (bi, qi, hp)),
                  pl.BlockSpec((None, t_k, npair * LANES), lambda bi, hp, qi: (bi, 0, hp)),
                  pl.BlockSpec((None, t_k, npair * LANES), lambda bi, hp, qi: (bi, 0, hp)),
                  pl.BlockSpec((None, npair, tq, 2), lambda bi, hp, qi: (bi, hp, qi, 0)),
                  pl.BlockSpec((None, npair, 2, t_kp), lambda bi, hp, qi: (bi, hp, 0, 0))],
        out_specs=pl.BlockSpec((None, tq, npair * LANES), lambda bi, hp, qi: (bi, qi, hp)),
        out_shape=jax.ShapeDtypeStruct((b, t_q, W_A), F32),
        scratch_shapes=[pltpu.VMEM((2 * npair, tq, LANES), F32)] * 3,
        compiler_params=_cparams("parallel", "parallel", "arbitrary"),
        name="fox_attention",
    )(q, k, v, c_q, c_k)


def _mla_kernel(q_ref, k_ref, v_ref, o_ref, m_ref, l_ref, acc_ref, *, tq, tk, n_past, single_tile):
    n_open, d0 = _tile_extent(n_past, tq, tk, single_tile)
    row = lax.broadcasted_iota(jnp.int32, (tq, tq), 0)
    col = lax.broadcasted_iota(jnp.int32, (tq, tq), 1)
    heads = []
    for j in range(MLA_HEADS_PER_STEP):
        key_chunk = functools.partial(
            lambda k0, width, j: k_ref[pl.ds(k0, width), j * MLA_SLAB:(j + 1) * MLA_SLAB], j=j)
        value_chunk = functools.partial(lambda k0, width, j: v_ref[pl.ds(k0, width), j * V_HD:(j + 1) * V_HD], j=j)
        heads.append((q_ref[:, j * MLA_SLAB:(j + 1) * MLA_SLAB], key_chunk, value_chunk, None, None))
    mask = (d0 + col) >> _CHUNK_SHIFT <= (d0 + row) >> _CHUNK_SHIFT
    outs = _flash_sweep(heads, n_open, d0, tq, tk, mask, m_ref, l_ref, acc_ref)
    for j in range(MLA_HEADS_PER_STEP):
        o_ref[:, j * V_HD:(j + 1) * V_HD] = outs[j]


def _mla_attention(q, k, v, *, n_past, tq, tk):
    b, t_q, _ = q.shape
    t_k = k.shape[1]
    assert t_q % tq == 0 and n_past % tk == 0 and (t_q == tq or tq % tk == 0)
    nh = MLA_HEADS_PER_STEP
    kern = functools.partial(_mla_kernel, tq=tq, tk=tk, n_past=n_past, single_tile=t_q == tq)
    return pl.pallas_call(
        kern,
        grid=(b, H_B // nh, t_q // tq),
        in_specs=[pl.BlockSpec((None, tq, nh * MLA_SLAB), lambda bi, hd, qi: (bi, qi, hd)),
                  pl.BlockSpec((None, t_k, nh * MLA_SLAB), lambda bi, hd, qi: (bi, 0, hd)),
                  pl.BlockSpec((None, t_k, nh * V_HD), lambda bi, hd, qi: (bi, 0, hd))],
        out_specs=pl.BlockSpec((None, tq, nh * V_HD), lambda bi, hd, qi: (bi, qi, hd)),
        out_shape=jax.ShapeDtypeStruct((b, t_q, W_B), F32),
        scratch_shapes=[pltpu.VMEM((nh, tq, LANES), F32), pltpu.VMEM((nh, tq, LANES), F32),
                        pltpu.VMEM((nh, tq, V_HD), F32)],
        compiler_params=_cparams("parallel", "parallel", "arbitrary"),
        name="mla_attention",
    )(q, k, v)


def _kv_up_kernel(ckv_ref, kr_ref, wuk_ref, wuv_ref, k_ref, v_ref):
    c = ckv_ref[...].astype(BF16)
    kr = kr_ref[...].astype(BF16)
    kn = jnp.dot(c, wuk_ref[...], preferred_element_type=F32).astype(BF16)
    for hd in range(H_B):
        k_ref[:, hd * MLA_SLAB:hd * MLA_SLAB + NOPE] = kn[:, hd * NOPE:(hd + 1) * NOPE]
        k_ref[:, hd * MLA_SLAB + NOPE:(hd + 1) * MLA_SLAB] = kr
    v_ref[...] = jnp.dot(c, wuv_ref[...], preferred_element_type=F32).astype(BF16)


def _kv_up(ckv_n, krope128, w_uk, w_uv):
    r = ckv_n.shape[0]
    tm = ROW_TILE
    row = lambda w: pl.BlockSpec((tm, w), lambda i: (i, 0))
    return pl.pallas_call(
        _kv_up_kernel, grid=(r // tm,),
        in_specs=[row(KV_RANK), row(LANES), _const_spec((KV_RANK, H_B * NOPE)),
                  _const_spec((KV_RANK, H_B * V_HD))],
        out_specs=[row(H_B * MLA_SLAB), row(H_B * V_HD)],
        out_shape=(jax.ShapeDtypeStruct((r, H_B * MLA_SLAB), BF16), jax.ShapeDtypeStruct((r, H_B * V_HD), BF16)),
        compiler_params=_cparams("parallel"), name="mla_kv_up",
    )(ckv_n, krope128, w_uk, w_uv)


_CONV_SUB = 64
_CONV_HALO = -(-CONV_W // SUBLANES) * SUBLANES


def _conv_kernel(xp_ref, w_ref, b_ref, g_ref, beta_ref, o_ref, *, tt):
    t0 = pl.program_id(1) * tt
    w = w_ref[...]
    rows = min(_CONV_SUB, tt)
    for sub in range(tt // rows):
        base = pl.multiple_of(t0 + sub * rows, rows)
        xa = xp_ref[pl.ds(base, rows + _CONV_HALO), :]
        acc = None
        for b in range(SUBLANES):
            z = None
            for a in range(-(-CONV_W // SUBLANES)):
                tap = SUBLANES * a + b
                if tap < CONV_W:
                    term = xa[SUBLANES * a:SUBLANES * a + rows + SUBLANES, :] * w[tap:tap + 1, :]
                    z = term if z is None else z + term
            acc = z[b:b + rows, :] if acc is None else acc + z[b:b + rows, :]
        y = acc + b_ref[...]
        mu = jnp.mean(y, axis=-1, keepdims=True)
        yc = y - mu
        var = jnp.mean(yc * yc, axis=-1, keepdims=True)
        y = yc * lax.rsqrt(var + EPS) * g_ref[...] + beta_ref[...]
        o_ref[sub * rows:(sub + 1) * rows, :] = y * jax.nn.sigmoid(y)


def _conv_module(xp, w_dw, b_dw, ln_g, ln_b, *, t, tt):
    b, t_p, _ = xp.shape
    kern = functools.partial(_conv_kernel, tt=tt)
    return pl.pallas_call(
        kern, grid=(b, t // tt),
        in_specs=[pl.BlockSpec((None, t_p, C_C), lambda bi, ti: (bi, 0, 0)),
                  _const_spec((CONV_W, C_C)), _const_spec((1, C_C)), _const_spec((1, C_C)), _const_spec((1, C_C))],
        out_specs=pl.BlockSpec((None, tt, C_C), lambda bi, ti: (bi, ti, 0)),
        out_shape=jax.ShapeDtypeStruct((b, t, C_C), F32),
        compiler_params=_cparams("parallel", "arbitrary"), name="conv_module",
    )(xp, w_dw, b_dw, ln_g, ln_b)


def _out_proj_kernel(oa_ref, ob_ref, oc_ref, x_ref, gout_ref, wout_ref, gffn_ref, wr_ref, br_ref,
                     xn_ref, h2_ref, rl_ref):
    g = gout_ref[...]
    na = _rms(oa_ref[...], g[:, :W_A]).astype(BF16)
    nb = _rms(ob_ref[...], g[:, W_A:W_A + W_B]).astype(BF16)
    nc = _rms(oc_ref[...], g[:, W_A + W_B:]).astype(BF16)
    mix = jnp.dot(na, wout_ref[:W_A, :], preferred_element_type=F32)
    mix = mix + jnp.dot(nb, wout_ref[W_A:W_A + W_B, :], preferred_element_type=F32)
    mix = mix + jnp.dot(nc, wout_ref[W_A + W_B:, :], preferred_element_type=F32)
    xn = x_ref[...] + mix
    xn_ref[...] = xn
    h2 = _rms(xn, gffn_ref[...]).astype(BF16)
    rl_ref[...] = jnp.dot(h2, wr_ref[...], preferred_element_type=F32) + br_ref[...]
    bits = pltpu.bitcast(h2.astype(F32), jnp.uint32)
    h2_ref[...] = (bits[:, :D_MODEL // 2] >> 16) | (bits[:, D_MODEL // 2:] & jnp.uint32(0xFFFF0000))


def _out_proj(o_a, o_b, o_c, x, g_out, w_out, g_ffn, w_r, b_r):
    n = x.shape[0]
    tm = ROW_TILE
    row = lambda w: pl.BlockSpec((tm, w), lambda i: (i, 0))
    return pl.pallas_call(
        _out_proj_kernel, grid=(n // tm,),
        in_specs=[row(W_A), row(W_B), row(C_C), row(D_MODEL), _const_spec((1, D_MODEL)),
                  _const_spec((D_MODEL, D_MODEL)), _const_spec((1, D_MODEL)),
                  _const_spec((D_MODEL, LANES)), _const_spec((1, LANES))],
        out_specs=[row(D_MODEL), row(D_MODEL // 2), row(LANES)],
        out_shape=(jax.ShapeDtypeStruct((n, D_MODEL), F32), jax.ShapeDtypeStruct((n, D_MODEL // 2), jnp.uint32),
                   jax.ShapeDtypeStruct((n, LANES), F32)),
        compiler_params=_cparams("parallel"), name="out_proj_router",
    )(o_a, o_b, o_c, x, g_out, w_out, g_ffn, w_r, b_r)


def _expert_kernel(be_ref, nv_ref, src_ref, dst_ref, h_hbm, wg_ref, wu_ref, wd_ref, y_hbm,
                   xbuf, ybuf, zbuf, wgb_ref, wub_ref, wdb_ref, gsem, ssem, zsem, *, n_blocks, dump_row0):
    te = EXPERT_TILE
    xt = D_MODEL // 2 // LANES
    yt = D_MODEL // LANES
    i = pl.program_id(0)
    nv = nv_ref[0]
    cur = i % 2
    nxt = 1 - cur
    valid = i < nv

    def gather_copy(tok, buf, r):
        return pltpu.make_async_copy(h_hbm.at[tok], xbuf.at[buf, pl.ds(r * (xt + 1), xt), :], gsem.at[buf])

    def scatter_copy(row, buf, r):
        return pltpu.make_async_copy(ybuf.at[buf, pl.ds(r * (yt + 1), yt), :], y_hbm.at[row], ssem.at[buf])

    def gather_block(block, buf):
        for r in range(te):
            gather_copy(src_ref[block * te + r], buf, r).start(priority=1)

    def scatter_block(block, buf):
        for r in range(te):
            scatter_copy(dst_ref[block * te + r], buf, r).start(priority=r % 2)

    def wait_gather(buf):
        for r in range(te):
            gather_copy(0, buf, 0).wait()

    def wait_scatter(buf):
        for r in range(te):
            scatter_copy(0, buf, 0).wait()

    @pl.when(i == 0)
    def _():
        gather_block(0, 0)
        zbuf[...] = jnp.zeros(zbuf.shape, F32)
        fills = [pltpu.make_async_copy(zbuf, y_hbm.at[pl.ds(dump_row0 + k * SUBLANES, SUBLANES)], zsem)
                 for k in range(EXPERT_DUMP_ROWS // SUBLANES)]
        for fill in fills:
            fill.start()
        for fill in fills:
            fill.wait()

    prev_e = be_ref[jnp.maximum(i - 1, 0)]

    @pl.when(valid & ((i == 0) | (be_ref[i] != prev_e)))
    def _():
        wgb_ref[...] = wg_ref[...].astype(BF16)
        wub_ref[...] = wu_ref[...].astype(BF16)
        wdb_ref[...] = wd_ref[...].astype(BF16)

    @pl.when(valid)
    def _():
        wait_gather(cur)

    @pl.when(valid & (i >= 2))
    def _():
        wait_scatter(cur)

    def step(scatter_previous):
        gather_block(jnp.minimum(i + 1, n_blocks - 1), nxt)
        if scatter_previous:
            scatter_block(i - 1, nxt)
        u = jnp.concatenate([xbuf[cur, pl.ds(j, te, stride=xt + 1), :] for j in range(xt)], axis=1)
        x_lo = pltpu.bitcast(u << 16, F32).astype(BF16)
        x_hi = pltpu.bitcast(u & jnp.uint32(0xFFFF0000), F32).astype(BF16)
        half = D_MODEL // 2
        gate = (jnp.dot(x_lo, wgb_ref[:half, :], preferred_element_type=F32)
                + jnp.dot(x_hi, wgb_ref[half:, :], preferred_element_type=F32))
        up = (jnp.dot(x_lo, wub_ref[:half, :], preferred_element_type=F32)
              + jnp.dot(x_hi, wub_ref[half:, :], preferred_element_type=F32))
        hid = (gate * jax.nn.sigmoid(gate) * up).astype(BF16)
        y = jnp.dot(hid, wdb_ref[...], preferred_element_type=F32)
        for j in range(yt):
            ybuf[cur, pl.ds(j, te, stride=yt + 1), :] = y[:, j * LANES:(j + 1) * LANES]

    @pl.when(valid & (i == 0))
    def _():
        step(False)

    @pl.when(valid & (i > 0))
    def _():
        step(True)

    @pl.when(i == nv - 1)
    def _():
        scatter_block(i, cur)
        wait_scatter(cur)
        wait_gather(nxt)

    @pl.when((i == nv - 1) & (i > 0))
    def _():
        wait_scatter(nxt)


def _expert_blocks(block_e, n_valid, slot_src, slot_dst, h2_tiles, w_gate, w_up, w_down, *, layer):
    n_tok = h2_tiles.shape[0]
    te = EXPERT_TILE
    n_blocks = block_e.shape[0]
    xt, yt = D_MODEL // 2 // LANES, D_MODEL // LANES
    weights = lambda r, c: pl.BlockSpec((None, None, r, c), lambda i, be, nv, src, dst: (layer, be[i], 0, 0))
    grid_spec = pltpu.PrefetchScalarGridSpec(
        num_scalar_prefetch=4, grid=(n_blocks,),
        in_specs=[pl.BlockSpec(memory_space=pl.ANY), weights(D_MODEL, D_EXPERT), weights(D_MODEL, D_EXPERT),
                  weights(D_EXPERT, D_MODEL)],
        out_specs=pl.BlockSpec(memory_space=pl.ANY),
        scratch_shapes=[pltpu.VMEM((2, te * (xt + 1), LANES), jnp.uint32),
                        pltpu.VMEM((2, te * (yt + 1), LANES), F32),
                        pltpu.VMEM((SUBLANES, yt, LANES), F32),
                        pltpu.VMEM((D_MODEL, D_EXPERT), BF16), pltpu.VMEM((D_MODEL, D_EXPERT), BF16),
                        pltpu.VMEM((D_EXPERT, D_MODEL), BF16),
                        pltpu.SemaphoreType.DMA((2,)), pltpu.SemaphoreType.DMA((2,)), pltpu.SemaphoreType.DMA(())])
    kern = functools.partial(_expert_kernel, n_blocks=n_blocks, dump_row0=TOP_K * n_tok)
    return pl.pallas_call(
        kern, grid_spec=grid_spec,
        out_shape=jax.ShapeDtypeStruct((TOP_K * n_tok + EXPERT_DUMP_ROWS, yt, LANES), F32),
        compiler_params=_cparams("arbitrary"), name="expert_blocks",
    )(block_e, n_valid, slot_src, slot_dst, h2_tiles, w_gate, w_up, w_down)


def _route(rlog):
    n = rlog.shape[0]
    idx = jnp.arange(n)
    g_logits = rlog[:, :N_GROUPS]
    g_sel = jnp.argmax(g_logits, axis=-1)
    g_w = jax.nn.softmax(g_logits, axis=-1)[idx, g_sel]
    e_logits = rlog[:, N_GROUPS:N_GROUPS + N_EXPERTS].reshape(n, N_GROUPS, EXPERTS_PER_GROUP)[idx, g_sel]
    top_v, top_i = lax.top_k(e_logits, TOP_K)
    gate = jax.nn.softmax(top_v, axis=-1) * g_w[:, None]
    expert_ids = (g_sel[:, None] * EXPERTS_PER_GROUP + top_i).astype(jnp.int32)
    return expert_ids, gate


def _dispatch_plan(expert_ids):
    n = expert_ids.shape[0]
    te = EXPERT_TILE
    a = n * TOP_K
    n_blocks = -(-a // te) + N_EXPERTS
    rows = n_blocks * te
    flat_e = expert_ids.reshape(-1)
    experts = jnp.arange(N_EXPERTS, dtype=jnp.int32)
    counts = jnp.sum((flat_e[:, None] == experts[None, :]).astype(jnp.int32), axis=0)
    padded = (counts + te - 1) // te * te
    pad_end = jnp.cumsum(padded)
    pad_start = pad_end - padded
    start = jnp.cumsum(counts) - counts
    order = jnp.argsort(flat_e).astype(jnp.int32)
    slot = jnp.arange(rows, dtype=jnp.int32)
    slot_e = jnp.minimum(jnp.sum((pad_end[None, :] <= slot[:, None]).astype(jnp.int32), axis=-1), N_EXPERTS - 1)
    slot_hot = (slot_e[:, None] == experts[None, :]).astype(jnp.int32)
    slot_rank = slot - jnp.sum(slot_hot * pad_start[None, :], axis=-1)
    filled = slot_rank < jnp.sum(slot_hot * counts[None, :], axis=-1)
    src = jnp.where(filled, jnp.sum(slot_hot * start[None, :], axis=-1) + slot_rank, 0)
    slot_a = order[src]
    slot_src = jnp.where(filled, slot_a // TOP_K, 0)
    slot_dst = jnp.where(filled, (slot_a % TOP_K) * n + slot_a // TOP_K, a + slot % EXPERT_DUMP_ROWS)
    block_e = slot_e.reshape(n_blocks, te)[:, 0]
    n_valid = (pad_end[-1] // te).astype(jnp.int32).reshape(1)
    return slot_src, slot_dst, block_e, n_valid


def _combine_kernel(x_ref, y0_ref, y1_ref, g0_ref, g1_ref, gfin_ref, o_ref, *, final):
    tm = x_ref.shape[0]
    yt = D_MODEL // LANES
    g0 = g0_ref[...]
    g1 = g1_ref[...]
    x = jnp.concatenate(
        [x_ref[:, j * LANES:(j + 1) * LANES]
         + (y0_ref[pl.ds(j, tm, stride=yt), :] * g0 + y1_ref[pl.ds(j, tm, stride=yt), :] * g1) for j in range(yt)],
        axis=1)
    o_ref[...] = _rms(x, gfin_ref[...]) if final else x


def _combine(x, ys, gate, g_final, *, row0, n_tok, final):
    n = x.shape[0]
    tm = ROW_TILE
    yt = D_MODEL // LANES
    assert row0 % tm == 0 and n_tok % tm == 0
    row = pl.BlockSpec((tm, D_MODEL), lambda i: (i, 0))
    lanes = pl.BlockSpec((tm, LANES), lambda i: (i, 0))
    expert_rows = lambda k: pl.BlockSpec((tm * yt, LANES), lambda i: (i + (row0 + k * n_tok) // tm, 0))
    return pl.pallas_call(
        functools.partial(_combine_kernel, final=final), grid=(n // tm,),
        in_specs=[row, expert_rows(0), expert_rows(1), lanes, lanes, _const_spec((1, D_MODEL))],
        out_specs=row, out_shape=jax.ShapeDtypeStruct((n, D_MODEL), F32),
        compiler_params=_cparams("parallel"), name="moe_combine",
    )(x, ys, ys, *gate, g_final)


def _rope_tables(pos):
    half = ROPE // 2
    inv = ROPE_THETA ** (-jnp.arange(half, dtype=F32) / half)
    ang = pos.astype(F32)[:, None] * inv[None, :]
    zeros = jnp.zeros((pos.shape[0], LANES - ROPE), F32)
    cos = jnp.concatenate([jnp.cos(ang), jnp.cos(ang), zeros], axis=-1)
    sin = jnp.concatenate([-jnp.sin(ang), jnp.sin(ang), zeros], axis=-1)
    return cos, sin


def _prep_w_in(w_in):
    qa, ka, va, fa, cq, ckv, kr, ua, ug = jnp.split(
        w_in, np.cumsum([W_A, W_A, W_A, H_A, Q_RANK, KV_RANK, ROPE, C_C])[:].tolist(), axis=1)
    pad = jnp.zeros((D_MODEL, LANES - ROPE - H_A), w_in.dtype)
    return jnp.concatenate([qa * (D_HA ** -0.5), ka, va, cq, ckv, ua, ug, kr, fa, pad], axis=1).astype(BF16)


def _prep_w_uq(w_uq):
    w = w_uq.reshape(Q_RANK, H_B, NOPE + ROPE)
    w = jnp.pad(w, ((0, 0), (0, 0), (0, MLA_SLAB - NOPE - ROPE)))
    return w.reshape(Q_RANK, H_B * MLA_SLAB).astype(BF16)


def _prep_w_ukv(w_ukv):
    w = w_ukv.reshape(KV_RANK, H_B, NOPE + V_HD)
    return (w[:, :, :NOPE].reshape(KV_RANK, H_B * NOPE).astype(BF16),
            w[:, :, NOPE:].reshape(KV_RANK, H_B * V_HD).astype(BF16))


def _pad_lanes(x, width):
    return jnp.pad(x, ((0, 0),) * (x.ndim - 1) + ((0, width - x.shape[-1]),))


def _round_up(x, m):
    return -(-x // m) * m


def kernel(x_prompt, x_sample, cache_fox_k, cache_fox_v, cache_fox_logf, cache_mla_ckv, cache_mla_krope, state_conv, g_mix, w_in, b_f, g_q, w_uq, g_kv, w_ukv, w_dw, b_dw, ln_g, ln_b, g_out, w_out, g_ffn, w_rg, b_rg, w_re, b_re, w_gate, w_up, w_down, g_final):
    bp, tp, _ = x_prompt.shape
    bs, ts, _ = x_sample.shape
    n_past = cache_fox_k.shape[2]
    n_p, n_s = bp * tp, bs * ts
    depth = g_mix.shape[0]

    halo = CONV_W - 1
    tkp_s = _round_up(n_past + ts, LANES)

    x_p = x_prompt.reshape(n_p, D_MODEL)
    x_s = x_sample.reshape(n_s, D_MODEL)
    cos_p, sin_p = _rope_tables(jnp.arange(tp))
    cos_s, sin_s = _rope_tables(n_past + jnp.arange(ts))
    rope_p = (jnp.tile(cos_p, (bp, 1)), jnp.tile(sin_p, (bp, 1)))
    rope_s = (jnp.tile(cos_s, (bs, 1)), jnp.tile(sin_s, (bs, 1)))

    def split_heads(c):
        return c.reshape(c.shape[0], H_A // 2, 2, c.shape[-1])

    states_p, states_s = [], []
    for l in range(depth):
        w_uk, w_uv = _prep_w_ukv(w_ukv[l])
        b_f128 = jnp.pad(b_f[l], (_FA_LANE, LANES - _FA_LANE - H_A))[None, :]
        in_w = (g_mix[l][None], _prep_w_in(w_in[l]), g_q[l][None], g_kv[l][None], _prep_w_uq(w_uq[l]), b_f128)
        qa_p, ka_p, va_p, kab_p, vab_p, lf_p, qm_p, ckv_p, kr_p, u_p = _in_proj(x_p, *in_w, *rope_p)
        qa_s, ka_s, va_s, kab_s, vab_s, lf_s, qm_s, ckv_s, kr_s, u_s = _in_proj(x_s, *in_w, *rope_s)

        lf_p = lf_p.reshape(bp, tp, H_A)
        lf_s = lf_s.reshape(bs, ts, H_A)
        c_p = _cumsum_time(lf_p.transpose(0, 2, 1))
        lf_all = jnp.concatenate([cache_fox_logf[l].astype(F32), lf_s], axis=1)
        c_s = _cumsum_time(_pad_lanes(lf_all.transpose(0, 2, 1), tkp_s))
        o_a_p = _fox_attention(
            qa_p.reshape(bp, tp, W_A), kab_p.reshape(bp, tp, W_A), vab_p.reshape(bp, tp, W_A),
            split_heads(c_p).transpose(0, 1, 3, 2), split_heads(c_p), n_past=0, tq=TQ_PROMPT, tk=TQ_PROMPT)
        k_all = jnp.concatenate([cache_fox_k[l].reshape(bs, n_past, W_A).astype(BF16),
                                 kab_s.reshape(bs, ts, W_A)], axis=1)
        v_all = jnp.concatenate([cache_fox_v[l].reshape(bs, n_past, W_A).astype(BF16),
                                 vab_s.reshape(bs, ts, W_A)], axis=1)
        o_a_s = _fox_attention(
            qa_s.reshape(bs, ts, W_A), k_all, v_all,
            split_heads(c_s[:, :, n_past:n_past + ts]).transpose(0, 1, 3, 2), split_heads(c_s),
            n_past=n_past, tq=ts, tk=TK_SAMPLE)

        ckv_s = ckv_s.reshape(bs, ts, KV_RANK)
        kr_s = kr_s.reshape(bs, ts, LANES)
        ckv_all = jnp.concatenate([cache_mla_ckv[l].astype(F32), ckv_s], axis=1)
        kr_all = jnp.concatenate([_pad_lanes(cache_mla_krope[l].astype(F32), LANES), kr_s], axis=1)
        km_p, vm_p = _kv_up(ckv_p, kr_p, w_uk, w_uv)
        km_s, vm_s = _kv_up(ckv_all.reshape(-1, KV_RANK), kr_all.reshape(-1, LANES), w_uk, w_uv)
        o_b_p = _mla_attention(
            qm_p.reshape(bp, tp, H_B * MLA_SLAB), km_p.reshape(bp, tp, H_B * MLA_SLAB), vm_p.reshape(bp, tp, W_B),
            n_past=0, tq=TQ_PROMPT, tk=TQ_PROMPT)
        o_b_s = _mla_attention(
            qm_s.reshape(bs, ts, H_B * MLA_SLAB), km_s.reshape(bs, n_past + ts, H_B * MLA_SLAB),
            vm_s.reshape(bs, n_past + ts, W_B), n_past=n_past, tq=ts, tk=TK_SAMPLE)

        xp_p = jnp.pad(u_p.reshape(bp, tp, C_C), ((0, 0), (halo, _CONV_HALO - halo), (0, 0)))
        xp_s = jnp.concatenate([state_conv[l].astype(F32), u_s.reshape(bs, ts, C_C)], axis=1)
        conv_p = xp_p[:, tp:tp + halo]
        conv_s = xp_s[:, ts:ts + halo]
        xp_s = jnp.pad(xp_s, ((0, 0), (0, _CONV_HALO - halo), (0, 0)))
        conv_w = (w_dw[l], b_dw[l][None], ln_g[l][None], ln_b[l][None])
        o_c_p = _conv_module(xp_p, *conv_w, t=tp, tt=CONV_TILE)
        o_c_s = _conv_module(xp_s, *conv_w, t=ts, tt=ts)

        w_r = _pad_lanes(jnp.concatenate([w_rg[l], w_re[l]], axis=1), LANES).astype(BF16)
        b_r = _pad_lanes(jnp.concatenate([b_rg[l], b_re[l].reshape(-1)])[None, :].astype(F32), LANES)
        out_w = (g_out[l][None], w_out[l].astype(BF16), g_ffn[l][None], w_r, b_r)
        x_p, h2_p, rlog_p = _out_proj(o_a_p.reshape(n_p, W_A), o_b_p.reshape(n_p, W_B), o_c_p.reshape(n_p, C_C),
                                      x_p, *out_w)
        x_s, h2_s, rlog_s = _out_proj(o_a_s.reshape(n_s, W_A), o_b_s.reshape(n_s, W_B), o_c_s.reshape(n_s, C_C),
                                      x_s, *out_w)

        expert_ids, gate = _route(jnp.concatenate([rlog_p, rlog_s], axis=0))
        slot_src, slot_dst, block_e, n_valid = _dispatch_plan(expert_ids)
        h2_tiles = jnp.concatenate([h2_p, h2_s], axis=0).reshape(n_p + n_s, D_MODEL // 2 // LANES, LANES)
        ys = _expert_blocks(block_e, n_valid, slot_src, slot_dst, h2_tiles, w_gate, w_up, w_down, layer=l)
        final = l == depth - 1
        ys = ys.reshape(-1, LANES)
        gates = lambda rows: [jnp.broadcast_to(gate[rows, k:k + 1], (gate[rows].shape[0], LANES))
                              for k in range(TOP_K)]
        x_p = _combine(x_p, ys, gates(slice(0, n_p)), g_final[None], row0=0, n_tok=n_p + n_s, final=final)
        x_s = _combine(x_s, ys, gates(slice(n_p, None)), g_final[None], row0=n_p, n_tok=n_p + n_s, final=final)

        states_p.append((ka_p.reshape(bp, tp, H_A, D_HA), va_p.reshape(bp, tp, H_A, D_HA), lf_p,
                         ckv_p.reshape(bp, tp, KV_RANK), kr_p[:, :ROPE].reshape(bp, tp, ROPE), conv_p))
        states_s.append((ka_s.reshape(bs, ts, H_A, D_HA), va_s.reshape(bs, ts, H_A, D_HA), lf_s,
                         ckv_s, kr_s[:, :, :ROPE], conv_s))

    y_p = x_p.reshape(bp, tp, D_MODEL)
    y_s = x_s.reshape(bs, ts, D_MODEL)
    p_out = [jnp.stack(a) for a in zip(*states_p)]
    s_out = [jnp.stack(a) for a in zip(*states_s)]
    return (y_p, y_s, *p_out, *s_out)
```

```python
import functools

import numpy as np
import jax
import jax.numpy as jnp
from jax import lax
from jax.experimental import pallas as pl
from jax.experimental.pallas import tpu as pltpu

F32 = jnp.float32
BF16 = jnp.bfloat16

D_MODEL = 2048
DEPTH = 4
CHUNK = 64
H_A, D_HA = 8, 64
W_A = H_A * D_HA
H_B, NOPE, ROPE, V_HD = 8, 128, 64, 128
Q_RANK, KV_RANK = 512, 256
W_B = H_B * V_HD
C_C = D_MODEL - W_A - W_B
CONV_W = 31
_CHUNK_SHIFT = CHUNK.bit_length() - 1
assert 1 << _CHUNK_SHIFT == CHUNK
ROPE_THETA = 10000.0
N_GROUPS, EXPERTS_PER_GROUP = 4, 8
N_EXPERTS = N_GROUPS * EXPERTS_PER_GROUP
TOP_K = 2
D_EXPERT = 512
EPS = 1e-6
NEG_INF = -1e30
LOG2E = 1.4426950408889634

LANES = 128
SUBLANES = 8
TQ_PROMPT = 512
TK_PROMPT = 1024
TK_SAMPLE = 512
MLA_HEADS_PER_STEP = 4
FOX_PAIRS_PER_STEP = 2
CONV_TILE = 256
MLA_SLAB = NOPE + LANES
ROW_TILE = 256
IN_PROJ_TILE = 512
EXPERT_TILE = 256
EXPERT_DUMP_ROWS = 2 * EXPERT_TILE
VMEM_LIMIT = 56 * 1024 * 1024

_C_QA, _C_KA, _C_VA, _C_CQ, _C_CKV, _C_UA, _C_UG, _C_KRF = 0, 512, 1024, 1536, 2048, 2304, 2816, 3328
IN_COLS = _C_KRF + LANES
_FA_LANE = ROPE


def _cparams(*sem):
    return pltpu.CompilerParams(dimension_semantics=sem, vmem_limit_bytes=VMEM_LIMIT)


def _const_spec(shape):
    nd = len(shape)
    return pl.BlockSpec(shape, lambda *_: (0,) * nd)


def _rms(x, g):
    return x * lax.rsqrt(jnp.mean(x * x, axis=-1, keepdims=True) + EPS) * g


def _rope_block(x, cos, sin_signed):
    lane = lax.broadcasted_iota(jnp.int32, x.shape, 1)
    partner = jnp.where(lane < ROPE // 2, pltpu.roll(x, LANES - ROPE // 2, 1), pltpu.roll(x, ROPE // 2, 1))
    return x * cos + partner * sin_signed


def _store_heads(o_ref, x):
    rows = x.shape[0]
    for hd in range(H_A):
        o_ref[pl.ds(hd, rows, stride=H_A), :] = x[:, hd * D_HA:(hd + 1) * D_HA]


def _in_proj_kernel(x_ref, gmix_ref, win_ref, gq_ref, gkv_ref, wuq_ref, bf_ref, cos_ref, sin_ref,
                    qa_ref, ka_ref, va_ref, kab_ref, vab_ref, logf_ref, qmla_ref, ckv_ref, kr_ref, u_ref):
    h = _rms(x_ref[...], gmix_ref[...]).astype(BF16)

    def proj(c0, width):
        return jnp.dot(h, win_ref[:, c0:c0 + width], preferred_element_type=F32)

    qa_ref[...] = (proj(_C_QA, W_A) * LOG2E).astype(BF16)
    ka = proj(_C_KA, W_A)
    _store_heads(ka_ref, ka)
    kab_ref[...] = ka.astype(BF16)
    va = proj(_C_VA, W_A)
    _store_heads(va_ref, va)
    vab_ref[...] = va.astype(BF16)

    cos = cos_ref[...]
    sin = sin_ref[...]
    krf = proj(_C_KRF, LANES)
    kr_ref[...] = _rope_block(krf, cos, sin)
    z = krf + bf_ref[...]
    logsig = jnp.minimum(z, 0.0) - jnp.log(1.0 + jnp.exp(-jnp.abs(z)))
    logf_ref[...] = logsig[:, _FA_LANE:_FA_LANE + H_A]

    ckv_ref[...] = _rms(proj(_C_CKV, KV_RANK), gkv_ref[...])

    u_ref[...] = proj(_C_UA, C_C) * jax.nn.sigmoid(proj(_C_UG, C_C))

    cqn = _rms(proj(_C_CQ, Q_RANK), gq_ref[...]).astype(BF16)
    scale = (NOPE + ROPE) ** -0.5 * LOG2E
    for hd in range(H_B):
        q = jnp.dot(cqn, wuq_ref[:, hd * MLA_SLAB:(hd + 1) * MLA_SLAB], preferred_element_type=F32)
        qmla_ref[:, hd * MLA_SLAB:hd * MLA_SLAB + NOPE] = (q[:, :NOPE] * scale).astype(BF16)
        qmla_ref[:, hd * MLA_SLAB + NOPE:(hd + 1) * MLA_SLAB] = (
            _rope_block(q[:, NOPE:], cos, sin) * scale).astype(BF16)


def _in_proj(x, g_mix, w_in_p, g_q, g_kv, w_uq_p, b_f128, cos, sin):
    n = x.shape[0]
    tm = min(IN_PROJ_TILE, n)
    assert n % tm == 0
    row = lambda w: pl.BlockSpec((tm, w), lambda i: (i, 0))
    heads = pl.BlockSpec((tm * H_A, D_HA), lambda i: (i, 0))
    out_shapes = (
        jax.ShapeDtypeStruct((n, W_A), BF16),
        jax.ShapeDtypeStruct((n * H_A, D_HA), F32),
        jax.ShapeDtypeStruct((n * H_A, D_HA), F32),
        jax.ShapeDtypeStruct((n, W_A), BF16),
        jax.ShapeDtypeStruct((n, W_A), BF16),
        jax.ShapeDtypeStruct((n, H_A), F32),
        jax.ShapeDtypeStruct((n, H_B * MLA_SLAB), BF16),
        jax.ShapeDtypeStruct((n, KV_RANK), F32),
        jax.ShapeDtypeStruct((n, LANES), F32),
        jax.ShapeDtypeStruct((n, C_C), F32),
    )
    return pl.pallas_call(
        _in_proj_kernel,
        grid=(n // tm,),
        in_specs=[row(D_MODEL), _const_spec((1, D_MODEL)), _const_spec((D_MODEL, IN_COLS)),
                  _const_spec((1, Q_RANK)), _const_spec((1, KV_RANK)),
                  _const_spec((Q_RANK, H_B * MLA_SLAB)), _const_spec((1, LANES)), row(LANES), row(LANES)],
        out_specs=[row(W_A), heads, heads, row(W_A), row(W_A), row(H_A), row(H_B * MLA_SLAB),
                   row(KV_RANK), row(LANES), row(C_C)],
        out_shape=out_shapes,
        compiler_params=_cparams("parallel"),
        name="in_proj",
    )(x, g_mix, w_in_p, g_q, g_kv, w_uq_p, b_f128, cos, sin)


def _cumsum_kernel(x_ref, o_ref):
    c = x_ref[...]
    t = c.shape[-1]
    lane = lax.broadcasted_iota(jnp.int32, c.shape, 1)
    s = 1
    while s < t:
        c = c + jnp.where(lane >= s, pltpu.roll(c, s, 1), 0.0)
        s *= 2
    o_ref[...] = c * LOG2E


def _cumsum_time(logf_t):
    b, hh, t = logf_t.shape
    spec = pl.BlockSpec((None, hh, t), lambda i: (i, 0, 0))
    return pl.pallas_call(
        _cumsum_kernel, grid=(b,), in_specs=[spec], out_specs=spec,
        out_shape=jax.ShapeDtypeStruct((b, hh, t), F32),
        compiler_params=_cparams("parallel"), name="cumsum_logf",
    )(logf_t)


def _softmax_step(s, v, row_bias, m_ref, l_ref, acc_ref):
    width = s.shape[1]
    s_max = jnp.max(s, axis=-1, keepdims=True)
    m_old = m_ref[...]
    m_new = jnp.maximum(m_old, s_max if row_bias is None else s_max + row_bias)
    alpha = jnp.exp2(m_old - m_new)
    offset = m_new if row_bias is None else m_new - row_bias
    if width % LANES == 0:
        p = [jnp.exp2(s[:, c * LANES:(c + 1) * LANES] - offset) for c in range(width // LANES)]
        l_add = functools.reduce(lambda a, b: a + b, p)
        p = p[0].astype(BF16) if len(p) == 1 else jnp.concatenate([x.astype(BF16) for x in p], axis=1)
    else:
        p = jnp.exp2(s - offset[:, :1])
        lane = lax.broadcasted_iota(jnp.int32, m_old.shape, 1)
        l_add = jnp.where(lane == 0, jnp.sum(p, axis=-1, keepdims=True), 0.0)
        p = p.astype(BF16)
    l_ref[...] = alpha * l_ref[...] + l_add
    acc_ref[...] = alpha * acc_ref[...] + jnp.dot(p, v, preferred_element_type=F32)
    m_ref[...] = m_new


def _flash_sweep(heads, n_open, d0, tq, tk, diag_mask, m_ref, l_ref, acc_ref):
    for h in range(len(heads)):
        m_ref[h] = jnp.full(m_ref.shape[1:], NEG_INF, F32)
        l_ref[h] = jnp.zeros(l_ref.shape[1:], F32)
        acc_ref[h] = jnp.zeros(acc_ref.shape[1:], F32)

    def chunk(k0, width, mask):
        for h, (q, key_chunk, value_chunk, col_bias, row_bias) in enumerate(heads):
            s = _qk(q, key_chunk(k0, width))
            if col_bias is not None:
                s = s + col_bias(k0, width)
            if mask is not None:
                s = jnp.where(mask, s, NEG_INF)
            _softmax_step(s, value_chunk(k0, width), row_bias, m_ref.at[h], l_ref.at[h], acc_ref.at[h])

    n_wide, n_narrow = n_open

    def wide_chunk(i, carry):
        chunk(pl.multiple_of(i * tk, tk), tk, None)
        return carry

    def narrow_chunk(i, carry):
        chunk(pl.multiple_of(n_wide * tk + i * tq, tq), tq, None)
        return carry

    lax.fori_loop(0, n_wide, wide_chunk, 0)
    if not (isinstance(n_narrow, int) and n_narrow == 0):
        lax.fori_loop(0, n_narrow, narrow_chunk, 0)
    chunk(d0, tq, diag_mask)
    return [acc_ref[h] / jnp.sum(l_ref[h], axis=-1, keepdims=True) for h in range(len(heads))]


def _qk(q, k):
    return lax.dot_general(q, k, (((1,), (1,)), ((), ())), preferred_element_type=F32)


def _tile_extent(n_past, tq, tk, single_tile):
    if single_tile:
        return (n_past // tk, (n_past % tk) // tq), n_past
    start = n_past + pl.program_id(2) * tq
    n_wide = start // tk
    n_narrow = 0 if tk == tq else (start - n_wide * tk) // tq
    return (n_wide, n_narrow), pl.multiple_of(start, tq)


def _fox_kernel(q_ref, k_ref, v_ref, cq_ref, ck_ref, o_ref, m_ref, l_ref, acc_ref, *, tq, tk, n_past, single_tile):
    lane = lax.broadcasted_iota(jnp.int32, (tq, LANES), 1)
    n_open, d0 = _tile_extent(n_past, tq, tk, single_tile)
    row = lax.broadcasted_iota(jnp.int32, (tq, tq), 0)
    col = lax.broadcasted_iota(jnp.int32, (tq, tq), 1)
    heads = []
    for pair in range(FOX_PAIRS_PER_STEP):
        lanes = slice(pair * LANES, (pair + 1) * LANES)
        q = q_ref[:, lanes]
        key_chunk = functools.partial(lambda k0, width, lanes: k_ref[pl.ds(k0, width), lanes], lanes=lanes)
        value_chunk = functools.partial(lambda k0, width, lanes: v_ref[pl.ds(k0, width), lanes], lanes=lanes)
        for j in range(2):
            qj = jnp.where((lane >= j * D_HA) & (lane < (j + 1) * D_HA), q, jnp.zeros_like(q))
            col_bias = functools.partial(
                lambda k0, width, pair, j: -ck_ref[pair, j:j + 1, pl.ds(k0, width)], pair=pair, j=j)
            row_bias = jnp.broadcast_to(cq_ref[pair, :, j:j + 1], (tq, LANES))
            heads.append((qj, key_chunk, value_chunk, col_bias, row_bias))
    outs = _flash_sweep(heads, n_open, d0, tq, tk, col <= row, m_ref, l_ref, acc_ref)
    for pair in range(FOX_PAIRS_PER_STEP):
        o_ref[:, pair * LANES:(pair + 1) * LANES] = jnp.where(lane < D_HA, outs[2 * pair], outs[2 * pair + 1])


def _fox_attention(q, k, v, c_q, c_k, *, n_past, tq, tk):
    b, t_q, _ = q.shape
    t_k = k.shape[1]
    t_kp = c_k.shape[-1]
    assert t_q % tq == 0 and tk % tq == 0 and n_past % tq == 0
    npair = FOX_PAIRS_PER_STEP
    kern = functools.partial(_fox_kernel, tq=tq, tk=tk, n_past=n_past, single_tile=t_q == tq)
    return pl.pallas_call(
        kern,
        grid=(b, H_A // (2 * npair), t_q // tq),
        in_specs=[pl.BlockSpec((None, tq, npair * LANES), lambda bi, hp, qi: (bi, qi, hp)),
                  pl.BlockSpec((None, t_k, npair * LANES), lambda bi, hp, qi: (bi, 0, hp)),
                  pl.BlockSpec((None, t_k, npair * LANES), lambda bi, hp, qi: (bi, 0, hp)),
                  pl.BlockSpec((None, npair, tq, 2), lambda bi, hp, qi: (bi, hp, qi, 0)),
                  pl.BlockSpec((None, npair, 2, t_kp), lambda bi, hp, qi: (bi, hp, 0, 0))],
        out_specs=pl.BlockSpec((None, tq, npair * LANES), lambda bi, hp, qi: (bi, qi, hp)),
        out_shape=jax.ShapeDtypeStruct((b, t_q, W_A), F32),
        scratch_shapes=[pltpu.VMEM((2 * npair, tq, LANES), F32)] * 3,
        compiler_params=_cparams("parallel", "parallel", "arbitrary"),
        name="fox_attention",
    )(q, k, v, c_q, c_k)


def _mla_kernel(q_ref, k_ref, v_ref, o_ref, m_ref, l_ref, acc_ref, *, tq, tk, n_past, single_tile):
    n_open, d0 = _tile_extent(n_past, tq, tk, single_tile)
    row = lax.broadcasted_iota(jnp.int32, (tq, tq), 0)
    col = lax.broadcasted_iota(jnp.int32, (tq, tq), 1)
    heads = []
    for j in range(MLA_HEADS_PER_STEP):
        key_chunk = functools.partial(
            lambda k0, width, j: k_ref[pl.ds(k0, width), j * MLA_SLAB:(j + 1) * MLA_SLAB], j=j)
        value_chunk = functools.partial(lambda k0, width, j: v_ref[pl.ds(k0, width), j * V_HD:(j + 1) * V_HD], j=j)
        heads.append((q_ref[:, j * MLA_SLAB:(j + 1) * MLA_SLAB], key_chunk, value_chunk, None, None))
    mask = (d0 + col) >> _CHUNK_SHIFT <= (d0 + row) >> _CHUNK_SHIFT
    outs = _flash_sweep(heads, n_open, d0, tq, tk, mask, m_ref, l_ref, acc_ref)
    for j in range(MLA_HEADS_PER_STEP):
        o_ref[:, j * V_HD:(j + 1) * V_HD] = outs[j]


def _mla_attention(q, k, v, *, n_past, tq, tk):
    b, t_q, _ = q.shape
    t_k = k.shape[1]
    assert t_q % tq == 0 and tk % tq == 0 and n_past % tq == 0
    nh = MLA_HEADS_PER_STEP
    kern = functools.partial(_mla_kernel, tq=tq, tk=tk, n_past=n_past, single_tile=t_q == tq)
    return pl.pallas_call(
        kern,
        grid=(b, H_B // nh, t_q // tq),
        in_specs=[pl.BlockSpec((None, tq, nh * MLA_SLAB), lambda bi, hd, qi: (bi, qi, hd)),
                  pl.BlockSpec((None, t_k, nh * MLA_SLAB), lambda bi, hd, qi: (bi, 0, hd)),
                  pl.BlockSpec((None, t_k, nh * V_HD), lambda bi, hd, qi: (bi, 0, hd))],
        out_specs=pl.BlockSpec((None, tq, nh * V_HD), lambda bi, hd, qi: (bi, qi, hd)),
        out_shape=jax.ShapeDtypeStruct((b, t_q, W_B), F32),
        scratch_shapes=[pltpu.VMEM((nh, tq, LANES), F32), pltpu.VMEM((nh, tq, LANES), F32),
                        pltpu.VMEM((nh, tq, V_HD), F32)],
        compiler_params=_cparams("parallel", "parallel", "arbitrary"),
        name="mla_attention",
    )(q, k, v)


def _kv_up_kernel(ckv_ref, kr_ref, wuk_ref, wuv_ref, k_ref, v_ref):
    c = ckv_ref[...].astype(BF16)
    kr = kr_ref[...].astype(BF16)
    kn = jnp.dot(c, wuk_ref[...], preferred_element_type=F32).astype(BF16)
    for hd in range(H_B):
        k_ref[:, hd * MLA_SLAB:hd * MLA_SLAB + NOPE] = kn[:, hd * NOPE:(hd + 1) * NOPE]
        k_ref[:, hd * MLA_SLAB + NOPE:(hd + 1) * MLA_SLAB] = kr
    v_ref[...] = jnp.dot(c, wuv_ref[...], preferred_element_type=F32).astype(BF16)


def _kv_up(ckv_n, krope128, w_uk, w_uv):
    r = ckv_n.shape[0]
    tm = ROW_TILE
    row = lambda w: pl.BlockSpec((tm, w), lambda i: (i, 0))
    return pl.pallas_call(
        _kv_up_kernel, grid=(r // tm,),
        in_specs=[row(KV_RANK), row(LANES), _const_spec((KV_RANK, H_B * NOPE)),
                  _const_spec((KV_RANK, H_B * V_HD))],
        out_specs=[row(H_B * MLA_SLAB), row(H_B * V_HD)],
        out_shape=(jax.ShapeDtypeStruct((r, H_B * MLA_SLAB), BF16), jax.ShapeDtypeStruct((r, H_B * V_HD), BF16)),
        compiler_params=_cparams("parallel"), name="mla_kv_up",
    )(ckv_n, krope128, w_uk, w_uv)


_CONV_SUB = 64
_CONV_HALO = -(-CONV_W // SUBLANES) * SUBLANES


def _conv_kernel(xp_ref, w_ref, b_ref, g_ref, beta_ref, o_ref, *, tt):
    t0 = pl.program_id(1) * tt
    w = w_ref[...]
    rows = min(_CONV_SUB, tt)
    for sub in range(tt // rows):
        base = pl.multiple_of(t0 + sub * rows, rows)
        xa = xp_ref[pl.ds(base, rows + _CONV_HALO), :]
        acc = None
        for b in range(SUBLANES):
            z = None
            for a in range(-(-CONV_W // SUBLANES)):
                tap = SUBLANES * a + b
                if tap < CONV_W:
                    term = xa[SUBLANES * a:SUBLANES * a + rows + SUBLANES, :] * w[tap:tap + 1, :]
                    z = term if z is None else z + term
            acc = z[b:b + rows, :] if acc is None else acc + z[b:b + rows, :]
        y = acc + b_ref[...]
        mu = jnp.mean(y, axis=-1, keepdims=True)
        yc = y - mu
        var = jnp.mean(yc * yc, axis=-1, keepdims=True)
        y = yc * lax.rsqrt(var + EPS) * g_ref[...] + beta_ref[...]
        o_ref[sub * rows:(sub + 1) * rows, :] = y * jax.nn.sigmoid(y)


def _conv_module(xp, w_dw, b_dw, ln_g, ln_b, *, t, tt):
    b, t_p, _ = xp.shape
    kern = functools.partial(_conv_kernel, tt=tt)
    return pl.pallas_call(
        kern, grid=(b, t // tt),
        in_specs=[pl.BlockSpec((None, t_p, C_C), lambda bi, ti: (bi, 0, 0)),
                  _const_spec((CONV_W, C_C)), _const_spec((1, C_C)), _const_spec((1, C_C)), _const_spec((1, C_C))],
        out_specs=pl.BlockSpec((None, tt, C_C), lambda bi, ti: (bi, ti, 0)),
        out_shape=jax.ShapeDtypeStruct((b, t, C_C), F32),
        compiler_params=_cparams("parallel", "arbitrary"), name="conv_module",
    )(xp, w_dw, b_dw, ln_g, ln_b)


def _out_proj_kernel(oa_ref, ob_ref, oc_ref, x_ref, gout_ref, wout_ref, gffn_ref, wr_ref, br_ref,
                     xn_ref, h2_ref, rl_ref):
    g = gout_ref[...]
    na = _rms(oa_ref[...], g[:, :W_A]).astype(BF16)
    nb = _rms(ob_ref[...], g[:, W_A:W_A + W_B]).astype(BF16)
    nc = _rms(oc_ref[...], g[:, W_A + W_B:]).astype(BF16)
    mix = jnp.dot(na, wout_ref[:W_A, :], preferred_element_type=F32)
    mix = mix + jnp.dot(nb, wout_ref[W_A:W_A + W_B, :], preferred_element_type=F32)
    mix = mix + jnp.dot(nc, wout_ref[W_A + W_B:, :], preferred_element_type=F32)
    xn = x_ref[...] + mix
    xn_ref[...] = xn
    h2 = _rms(xn, gffn_ref[...]).astype(BF16)
    rl_ref[...] = jnp.dot(h2, wr_ref[...], preferred_element_type=F32) + br_ref[...]
    bits = pltpu.bitcast(h2.astype(F32), jnp.uint32)
    h2_ref[...] = (bits[:, :D_MODEL // 2] >> 16) | (bits[:, D_MODEL // 2:] & jnp.uint32(0xFFFF0000))


def _out_proj(o_a, o_b, o_c, x, g_out, w_out, g_ffn, w_r, b_r):
    n = x.shape[0]
    tm = ROW_TILE
    row = lambda w: pl.BlockSpec((tm, w), lambda i: (i, 0))
    return pl.pallas_call(
        _out_proj_kernel, grid=(n // tm,),
        in_specs=[row(W_A), row(W_B), row(C_C), row(D_MODEL), _const_spec((1, D_MODEL)),
                  _const_spec((D_MODEL, D_MODEL)), _const_spec((1, D_MODEL)),
                  _const_spec((D_MODEL, LANES)), _const_spec((1, LANES))],
        out_specs=[row(D_MODEL), row(D_MODEL // 2), row(LANES)],
        out_shape=(jax.ShapeDtypeStruct((n, D_MODEL), F32), jax.ShapeDtypeStruct((n, D_MODEL // 2), jnp.uint32),
                   jax.ShapeDtypeStruct((n, LANES), F32)),
        compiler_params=_cparams("parallel"), name="out_proj_router",
    )(o_a, o_b, o_c, x, g_out, w_out, g_ffn, w_r, b_r)


def _expert_kernel(be_ref, nv_ref, src_ref, dst_ref, h_hbm, wg_ref, wu_ref, wd_ref, y_hbm,
                   xbuf, ybuf, zbuf, wgb_ref, wub_ref, wdb_ref, gsem, ssem, zsem, *, n_blocks, dump_row0):
    te = EXPERT_TILE
    xt = D_MODEL // 2 // LANES
    yt = D_MODEL // LANES
    i = pl.program_id(0)
    nv = nv_ref[0]
    cur = i % 2
    nxt = 1 - cur
    valid = i < nv

    def gather_copy(tok, buf, r):
        return pltpu.make_async_copy(h_hbm.at[tok], xbuf.at[buf, pl.ds(r * (xt + 1), xt), :], gsem.at[buf])

    def scatter_copy(row, buf, r):
        return pltpu.make_async_copy(ybuf.at[buf, pl.ds(r * (yt + 1), yt), :], y_hbm.at[row], ssem.at[buf])

    def gather_block(block, buf):
        for r in range(te):
            gather_copy(src_ref[block * te + r], buf, r).start(priority=r % 2)

    def scatter_block(block, buf):
        for r in range(te):
            scatter_copy(dst_ref[block * te + r], buf, r).start(priority=(r + 1) % 2)

    def wait_gather(buf):
        for r in range(te):
            gather_copy(0, buf, 0).wait()

    def wait_scatter(buf):
        for r in range(te):
            scatter_copy(0, buf, 0).wait()

    @pl.when(i == 0)
    def _():
        gather_block(0, 0)
        zbuf[...] = jnp.zeros(zbuf.shape, F32)
        fills = [pltpu.make_async_copy(zbuf, y_hbm.at[pl.ds(dump_row0 + k * SUBLANES, SUBLANES)], zsem)
                 for k in range(EXPERT_DUMP_ROWS // SUBLANES)]
        for fill in fills:
            fill.start()
        for fill in fills:
            fill.wait()

    prev_e = be_ref[jnp.maximum(i - 1, 0)]

    @pl.when(valid & ((i == 0) | (be_ref[i] != prev_e)))
    def _():
        wgb_ref[...] = wg_ref[...].astype(BF16)
        wub_ref[...] = wu_ref[...].astype(BF16)
        wdb_ref[...] = wd_ref[...].astype(BF16)

    @pl.when(valid)
    def _():
        wait_gather(cur)

    @pl.when(valid & (i >= 2))
    def _():
        wait_scatter(cur)

    def step(scatter_previous):
        gather_block(jnp.minimum(i + 1, n_blocks - 1), nxt)
        if scatter_previous:
            scatter_block(i - 1, nxt)
        u = jnp.concatenate([xbuf[cur, pl.ds(j, te, stride=xt + 1), :] for j in range(xt)], axis=1)
        x_lo = pltpu.bitcast(u << 16, F32).astype(BF16)
        x_hi = pltpu.bitcast(u & jnp.uint32(0xFFFF0000), F32).astype(BF16)
        half = D_MODEL // 2
        gate = (jnp.dot(x_lo, wgb_ref[:half, :], preferred_element_type=F32)
                + jnp.dot(x_hi, wgb_ref[half:, :], preferred_element_type=F32))
        up = (jnp.dot(x_lo, wub_ref[:half, :], preferred_element_type=F32)
              + jnp.dot(x_hi, wub_ref[half:, :], preferred_element_type=F32))
        hid = (gate * jax.nn.sigmoid(gate) * up).astype(BF16)
        y = jnp.dot(hid, wdb_ref[...], preferred_element_type=F32)
        for j in range(yt):
            ybuf[cur, pl.ds(j, te, stride=yt + 1), :] = y[:, j * LANES:(j + 1) * LANES]

    @pl.when(valid & (i == 0))
    def _():
        step(False)

    @pl.when(valid & (i > 0))
    def _():
        step(True)

    @pl.when(i == nv - 1)
    def _():
        scatter_block(i, cur)
        wait_scatter(cur)
        wait_gather(nxt)

    @pl.when((i == nv - 1) & (i > 0))
    def _():
        wait_scatter(nxt)


def _expert_blocks(block_e, n_valid, slot_src, slot_dst, h2_tiles, w_gate, w_up, w_down, *, layer):
    n_tok = h2_tiles.shape[0]
    te = EXPERT_TILE
    n_blocks = block_e.shape[0]
    xt, yt = D_MODEL // 2 // LANES, D_MODEL // LANES
    weights = lambda r, c: pl.BlockSpec((None, None, r, c), lambda i, be, nv, src, dst: (layer, be[i], 0, 0))
    grid_spec = pltpu.PrefetchScalarGridSpec(
        num_scalar_prefetch=4, grid=(n_blocks,),
        in_specs=[pl.BlockSpec(memory_space=pl.ANY), weights(D_MODEL, D_EXPERT), weights(D_MODEL, D_EXPERT),
                  weights(D_EXPERT, D_MODEL)],
        out_specs=pl.BlockSpec(memory_space=pl.ANY),
        scratch_shapes=[pltpu.VMEM((2, te * (xt + 1), LANES), jnp.uint32),
                        pltpu.VMEM((2, te * (yt + 1), LANES), F32),
                        pltpu.VMEM((SUBLANES, yt, LANES), F32),
                        pltpu.VMEM((D_MODEL, D_EXPERT), BF16), pltpu.VMEM((D_MODEL, D_EXPERT), BF16),
                        pltpu.VMEM((D_EXPERT, D_MODEL), BF16),
                        pltpu.SemaphoreType.DMA((2,)), pltpu.SemaphoreType.DMA((2,)), pltpu.SemaphoreType.DMA(())])
    kern = functools.partial(_expert_kernel, n_blocks=n_blocks, dump_row0=TOP_K * n_tok)
    return pl.pallas_call(
        kern, grid_spec=grid_spec,
        out_shape=jax.ShapeDtypeStruct((TOP_K * n_tok + EXPERT_DUMP_ROWS, yt, LANES), F32),
        compiler_params=_cparams("arbitrary"), name="expert_blocks",
    )(block_e, n_valid, slot_src, slot_dst, h2_tiles, w_gate, w_up, w_down)


def _route(rlog):
    n = rlog.shape[0]
    idx = jnp.arange(n)
    g_logits = rlog[:, :N_GROUPS]
    g_sel = jnp.argmax(g_logits, axis=-1)
    g_w = jax.nn.softmax(g_logits, axis=-1)[idx, g_sel]
    e_logits = rlog[:, N_GROUPS:N_GROUPS + N_EXPERTS].reshape(n, N_GROUPS, EXPERTS_PER_GROUP)[idx, g_sel]
    top_v, top_i = lax.top_k(e_logits, TOP_K)
    gate = jax.nn.softmax(top_v, axis=-1) * g_w[:, None]
    expert_ids = (g_sel[:, None] * EXPERTS_PER_GROUP + top_i).astype(jnp.int32)
    return expert_ids, gate


def _dispatch_plan(expert_ids):
    n = expert_ids.shape[0]
    te = EXPERT_TILE
    a = n * TOP_K
    n_blocks = -(-a // te) + N_EXPERTS
    rows = n_blocks * te
    flat_e = expert_ids.reshape(-1)
    experts = jnp.arange(N_EXPERTS, dtype=jnp.int32)
    counts = jnp.sum((flat_e[:, None] == experts[None, :]).astype(jnp.int32), axis=0)
    padded = (counts + te - 1) // te * te
    pad_end = jnp.cumsum(padded)
    pad_start = pad_end - padded
    start = jnp.cumsum(counts) - counts
    order = jnp.argsort(flat_e).astype(jnp.int32)
    slot = jnp.arange(rows, dtype=jnp.int32)
    slot_e = jnp.minimum(jnp.sum((pad_end[None, :] <= slot[:, None]).astype(jnp.int32), axis=-1), N_EXPERTS - 1)
    slot_hot = (slot_e[:, None] == experts[None, :]).astype(jnp.int32)
    slot_rank = slot - jnp.sum(slot_hot * pad_start[None, :], axis=-1)
    filled = slot_rank < jnp.sum(slot_hot * counts[None, :], axis=-1)
    src = jnp.where(filled, jnp.sum(slot_hot * start[None, :], axis=-1) + slot_rank, 0)
    slot_a = order[src]
    slot_src = jnp.where(filled, slot_a // TOP_K, 0)
    slot_dst = jnp.where(filled, (slot_a % TOP_K) * n + slot_a // TOP_K, a + slot % EXPERT_DUMP_ROWS)
    block_e = slot_e.reshape(n_blocks, te)[:, 0]
    n_valid = (pad_end[-1] // te).astype(jnp.int32).reshape(1)
    return slot_src, slot_dst, block_e, n_valid


def _combine_kernel(x_ref, y0_ref, y1_ref, g0_ref, g1_ref, gfin_ref, o_ref, *, final):
    tm = x_ref.shape[0]
    yt = D_MODEL // LANES
    g0 = g0_ref[...]
    g1 = g1_ref[...]
    x = jnp.concatenate(
        [x_ref[:, j * LANES:(j + 1) * LANES]
         + (y0_ref[pl.ds(j, tm, stride=yt), :] * g0 + y1_ref[pl.ds(j, tm, stride=yt), :] * g1) for j in range(yt)],
        axis=1)
    o_ref[...] = _rms(x, gfin_ref[...]) if final else x


def _combine(x, ys, gate, g_final, *, row0, n_tok, final):
    n = x.shape[0]
    tm = ROW_TILE
    yt = D_MODEL // LANES
    assert row0 % tm == 0 and n_tok % tm == 0
    row = pl.BlockSpec((tm, D_MODEL), lambda i: (i, 0))
    lanes = pl.BlockSpec((tm, LANES), lambda i: (i, 0))
    expert_rows = lambda k: pl.BlockSpec((tm * yt, LANES), lambda i: (i + (row0 + k * n_tok) // tm, 0))
    return pl.pallas_call(
        functools.partial(_combine_kernel, final=final), grid=(n // tm,),
        in_specs=[row, expert_rows(0), expert_rows(1), lanes, lanes, _const_spec((1, D_MODEL))],
        out_specs=row, out_shape=jax.ShapeDtypeStruct((n, D_MODEL), F32),
        compiler_params=_cparams("parallel"), name="moe_combine",
    )(x, ys, ys, *gate, g_final)


def _rope_tables(pos):
    half = ROPE // 2
    inv = ROPE_THETA ** (-jnp.arange(half, dtype=F32) / half)
    ang = pos.astype(F32)[:, None] * inv[None, :]
    zeros = jnp.zeros((pos.shape[0], LANES - ROPE), F32)
    cos = jnp.concatenate([jnp.cos(ang), jnp.cos(ang), zeros], axis=-1)
    sin = jnp.concatenate([-jnp.sin(ang), jnp.sin(ang), zeros], axis=-1)
    return cos, sin


def _prep_w_in(w_in):
    qa, ka, va, fa, cq, ckv, kr, ua, ug = jnp.split(
        w_in, np.cumsum([W_A, W_A, W_A, H_A, Q_RANK, KV_RANK, ROPE, C_C])[:].tolist(), axis=1)
    pad = jnp.zeros((D_MODEL, LANES - ROPE - H_A), w_in.dtype)
    return jnp.concatenate([qa * (D_HA ** -0.5), ka, va, cq, ckv, ua, ug, kr, fa, pad], axis=1).astype(BF16)


def _prep_w_uq(w_uq):
    w = w_uq.reshape(Q_RANK, H_B, NOPE + ROPE)
    w = jnp.pad(w, ((0, 0), (0, 0), (0, MLA_SLAB - NOPE - ROPE)))
    return w.reshape(Q_RANK, H_B * MLA_SLAB).astype(BF16)


def _prep_w_ukv(w_ukv):
    w = w_ukv.reshape(KV_RANK, H_B, NOPE + V_HD)
    return (w[:, :, :NOPE].reshape(KV_RANK, H_B * NOPE).astype(BF16),
            w[:, :, NOPE:].reshape(KV_RANK, H_B * V_HD).astype(BF16))


def _pad_lanes(x, width):
    return jnp.pad(x, ((0, 0),) * (x.ndim - 1) + ((0, width - x.shape[-1]),))


def _round_up(x, m):
    return -(-x // m) * m


def kernel(x_prompt, x_sample, cache_fox_k, cache_fox_v, cache_fox_logf, cache_mla_ckv, cache_mla_krope, state_conv, g_mix, w_in, b_f, g_q, w_uq, g_kv, w_ukv, w_dw, b_dw, ln_g, ln_b, g_out, w_out, g_ffn, w_rg, b_rg, w_re, b_re, w_gate, w_up, w_down, g_final):
    bp, tp, _ = x_prompt.shape
    bs, ts, _ = x_sample.shape
    n_past = cache_fox_k.shape[2]
    n_p, n_s = bp * tp, bs * ts
    depth = g_mix.shape[0]

    halo = CONV_W - 1
    tkp_s = _round_up(n_past + ts, LANES)

    x_p = x_prompt.reshape(n_p, D_MODEL)
    x_s = x_sample.reshape(n_s, D_MODEL)
    cos_p, sin_p = _rope_tables(jnp.arange(tp))
    cos_s, sin_s = _rope_tables(n_past + jnp.arange(ts))
    rope_p = (jnp.tile(cos_p, (bp, 1)), jnp.tile(sin_p, (bp, 1)))
    rope_s = (jnp.tile(cos_s, (bs, 1)), jnp.tile(sin_s, (bs, 1)))

    def split_heads(c):
        return c.reshape(c.shape[0], H_A // 2, 2, c.shape[-1])

    states_p, states_s = [], []
    for l in range(depth):
        w_uk, w_uv = _prep_w_ukv(w_ukv[l])
        b_f128 = jnp.pad(b_f[l], (_FA_LANE, LANES - _FA_LANE - H_A))[None, :]
        in_w = (g_mix[l][None], _prep_w_in(w_in[l]), g_q[l][None], g_kv[l][None], _prep_w_uq(w_uq[l]), b_f128)
        qa_p, ka_p, va_p, kab_p, vab_p, lf_p, qm_p, ckv_p, kr_p, u_p = _in_proj(x_p, *in_w, *rope_p)
        qa_s, ka_s, va_s, kab_s, vab_s, lf_s, qm_s, ckv_s, kr_s, u_s = _in_proj(x_s, *in_w, *rope_s)

        lf_p = lf_p.reshape(bp, tp, H_A)
        lf_s = lf_s.reshape(bs, ts, H_A)
        c_p = _cumsum_time(lf_p.transpose(0, 2, 1))
        lf_all = jnp.concatenate([cache_fox_logf[l].astype(F32), lf_s], axis=1)
        c_s = _cumsum_time(_pad_lanes(lf_all.transpose(0, 2, 1), tkp_s))
        o_a_p = _fox_attention(
            qa_p.reshape(bp, tp, W_A), kab_p.reshape(bp, tp, W_A), vab_p.reshape(bp, tp, W_A),
            split_heads(c_p).transpose(0, 1, 3, 2), split_heads(c_p), n_past=0, tq=TQ_PROMPT, tk=TK_PROMPT)
        k_all = jnp.concatenate([cache_fox_k[l].reshape(bs, n_past, W_A).astype(BF16),
                                 kab_s.reshape(bs, ts, W_A)], axis=1)
        v_all = jnp.concatenate([cache_fox_v[l].reshape(bs, n_past, W_A).astype(BF16),
                                 vab_s.reshape(bs, ts, W_A)], axis=1)
        o_a_s = _fox_attention(
            qa_s.reshape(bs, ts, W_A), k_all, v_all,
            split_heads(c_s[:, :, n_past:n_past + ts]).transpose(0, 1, 3, 2), split_heads(c_s),
            n_past=n_past, tq=ts, tk=TK_SAMPLE)

        ckv_s = ckv_s.reshape(bs, ts, KV_RANK)
        kr_s = kr_s.reshape(bs, ts, LANES)
        ckv_all = jnp.concatenate([cache_mla_ckv[l].astype(F32), ckv_s], axis=1)
        kr_all = jnp.concatenate([_pad_lanes(cache_mla_krope[l].astype(F32), LANES), kr_s], axis=1)
        km_p, vm_p = _kv_up(ckv_p, kr_p, w_uk, w_uv)
        km_s, vm_s = _kv_up(ckv_all.reshape(-1, KV_RANK), kr_all.reshape(-1, LANES), w_uk, w_uv)
        o_b_p = _mla_attention(
            qm_p.reshape(bp, tp, H_B * MLA_SLAB), km_p.reshape(bp, tp, H_B * MLA_SLAB), vm_p.reshape(bp, tp, W_B),
            n_past=0, tq=TQ_PROMPT, tk=TK_PROMPT)
        o_b_s = _mla_attention(
            qm_s.reshape(bs, ts, H_B * MLA_SLAB), km_s.reshape(bs, n_past + ts, H_B * MLA_SLAB),
            vm_s.reshape(bs, n_past + ts, W_B), n_past=n_past, tq=ts, tk=TK_SAMPLE)

        xp_p = jnp.pad(u_p.reshape(bp, tp, C_C), ((0, 0), (halo, _CONV_HALO - halo), (0, 0)))
        xp_s = jnp.concatenate([state_conv[l].astype(F32), u_s.reshape(bs, ts, C_C)], axis=1)
        conv_p = xp_p[:, tp:tp + halo]
        conv_s = xp_s[:, ts:ts + halo]
        xp_s = jnp.pad(xp_s, ((0, 0), (0, _CONV_HALO - halo), (0, 0)))
        conv_w = (w_dw[l], b_dw[l][None], ln_g[l][None], ln_b[l][None])
        o_c_p = _conv_module(xp_p, *conv_w, t=tp, tt=CONV_TILE)
        o_c_s = _conv_module(xp_s, *conv_w, t=ts, tt=ts)

        w_r = _pad_lanes(jnp.concatenate([w_rg[l], w_re[l]], axis=1), LANES).astype(BF16)
        b_r = _pad_lanes(jnp.concatenate([b_rg[l], b_re[l].reshape(-1)])[None, :].astype(F32), LANES)
        out_w = (g_out[l][None], w_out[l].astype(BF16), g_ffn[l][None], w_r, b_r)
        x_p, h2_p, rlog_p = _out_proj(o_a_p.reshape(n_p, W_A), o_b_p.reshape(n_p, W_B), o_c_p.reshape(n_p, C_C),
                                      x_p, *out_w)
        x_s, h2_s, rlog_s = _out_proj(o_a_s.reshape(n_s, W_A), o_b_s.reshape(n_s, W_B), o_c_s.reshape(n_s, C_C),
                                      x_s, *out_w)

        expert_ids, gate = _route(jnp.concatenate([rlog_p, rlog_s], axis=0))
        slot_src, slot_dst, block_e, n_valid = _dispatch_plan(expert_ids)
        h2_tiles = jnp.concatenate([h2_p, h2_s], axis=0).reshape(n_p + n_s, D_MODEL // 2 // LANES, LANES)
        ys = _expert_blocks(block_e, n_valid, slot_src, slot_dst, h2_tiles, w_gate, w_up, w_down, layer=l)
        final = l == depth - 1
        ys = ys.reshape(-1, LANES)
        gates = lambda rows: [jnp.broadcast_to(gate[rows, k:k + 1], (gate[rows].shape[0], LANES))
                              for k in range(TOP_K)]
        x_p = _combine(x_p, ys, gates(slice(0, n_p)), g_final[None], row0=0, n_tok=n_p + n_s, final=final)
        x_s = _combine(x_s, ys, gates(slice(n_p, None)), g_final[None], row0=n_p, n_tok=n_p + n_s, final=final)

        states_p.append((ka_p.reshape(bp, tp, H_A, D_HA), va_p.reshape(bp, tp, H_A, D_HA), lf_p,
                         ckv_p.reshape(bp, tp, KV_RANK), kr_p[:, :ROPE].reshape(bp, tp, ROPE), conv_p))
        states_s.append((ka_s.reshape(bs, ts, H_A, D_HA), va_s.reshape(bs, ts, H_A, D_HA), lf_s,
                         ckv_s, kr_s[:, :, :ROPE], conv_s))

    y_p = x_p.reshape(bp, tp, D_MODEL)
    y_s = x_s.reshape(bs, ts, D_MODEL)
    p_out = [jnp.stack(a) for a in zip(*states_p)]
    s_out = [jnp.stack(a) for a in zip(*states_s)]
    return (y_p, y_s, *p_out, *s_out)
```

```python
import functools

import numpy as np
import jax
import jax.numpy as jnp
from jax import lax
from jax.experimental import pallas as pl
from jax.experimental.pallas import tpu as pltpu

F32 = jnp.float32
BF16 = jnp.bfloat16

D_MODEL = 2048
DEPTH = 4
CHUNK = 64
H_A, D_HA = 8, 64
W_A = H_A * D_HA
H_B, NOPE, ROPE, V_HD = 8, 128, 64, 128
Q_RANK, KV_RANK = 512, 256
W_B = H_B * V_HD
C_C = D_MODEL - W_A - W_B
CONV_W = 31
_CHUNK_SHIFT = CHUNK.bit_length() - 1
assert 1 << _CHUNK_SHIFT == CHUNK
ROPE_THETA = 10000.0
N_GROUPS, EXPERTS_PER_GROUP = 4, 8
N_EXPERTS = N_GROUPS * EXPERTS_PER_GROUP
TOP_K = 2
D_EXPERT = 512
EPS = 1e-6
NEG_INF = -1e30
LOG2E = 1.4426950408889634

LANES = 128
SUBLANES = 8
TQ_PROMPT = 512
TK_PROMPT = 1024
TK_SAMPLE = 1024
MLA_HEADS_PER_STEP = 4
FOX_PAIRS_PER_STEP = 2
CONV_TILE = 256
MLA_SLAB = NOPE + LANES
ROW_TILE = 256
IN_PROJ_TILE = 512
EXPERT_TILE = 256
EXPERT_DUMP_ROWS = 2 * EXPERT_TILE
VMEM_LIMIT = 56 * 1024 * 1024

_C_QA, _C_KA, _C_VA, _C_CQ, _C_CKV, _C_UA, _C_UG, _C_KRF = 0, 512, 1024, 1536, 2048, 2304, 2816, 3328
IN_COLS = _C_KRF + LANES
_FA_LANE = ROPE


def _cparams(*sem):
    return pltpu.CompilerParams(dimension_semantics=sem, vmem_limit_bytes=VMEM_LIMIT)


def _const_spec(shape):
    nd = len(shape)
    return pl.BlockSpec(shape, lambda *_: (0,) * nd)


def _rms(x, g):
    return x * lax.rsqrt(jnp.mean(x * x, axis=-1, keepdims=True) + EPS) * g


def _rope_block(x, cos, sin_signed):
    lane = lax.broadcasted_iota(jnp.int32, x.shape, 1)
    partner = jnp.where(lane < ROPE // 2, pltpu.roll(x, LANES - ROPE // 2, 1), pltpu.roll(x, ROPE // 2, 1))
    return x * cos + partner * sin_signed


def _store_heads(o_ref, x):
    rows = x.shape[0]
    for hd in range(H_A):
        o_ref[pl.ds(hd, rows, stride=H_A), :] = x[:, hd * D_HA:(hd + 1) * D_HA]


def _in_proj_kernel(x_ref, gmix_ref, win_ref, gq_ref, gkv_ref, wuq_ref, bf_ref, cos_ref, sin_ref,
                    qa_ref, ka_ref, va_ref, kab_ref, vab_ref, logf_ref, qmla_ref, ckv_ref, kr_ref, u_ref):
    h = _rms(x_ref[...], gmix_ref[...]).astype(BF16)

    def proj(c0, width):
        return jnp.dot(h, win_ref[:, c0:c0 + width], preferred_element_type=F32)

    qa_ref[...] = (proj(_C_QA, W_A) * LOG2E).astype(BF16)
    ka = proj(_C_KA, W_A)
    _store_heads(ka_ref, ka)
    kab_ref[...] = ka.astype(BF16)
    va = proj(_C_VA, W_A)
    _store_heads(va_ref, va)
    vab_ref[...] = va.astype(BF16)

    cos = cos_ref[...]
    sin = sin_ref[...]
    krf = proj(_C_KRF, LANES)
    kr_ref[...] = _rope_block(krf, cos, sin)
    z = krf + bf_ref[...]
    logsig = jnp.minimum(z, 0.0) - jnp.log(1.0 + jnp.exp(-jnp.abs(z)))
    logf_ref[...] = logsig[:, _FA_LANE:_FA_LANE + H_A]

    ckv_ref[...] = _rms(proj(_C_CKV, KV_RANK), gkv_ref[...])

    u_ref[...] = proj(_C_UA, C_C) * jax.nn.sigmoid(proj(_C_UG, C_C))

    cqn = _rms(proj(_C_CQ, Q_RANK), gq_ref[...]).astype(BF16)
    scale = (NOPE + ROPE) ** -0.5 * LOG2E
    for hd in range(H_B):
        q = jnp.dot(cqn, wuq_ref[:, hd * MLA_SLAB:(hd + 1) * MLA_SLAB], preferred_element_type=F32)
        qmla_ref[:, hd * MLA_SLAB:hd * MLA_SLAB + NOPE] = (q[:, :NOPE] * scale).astype(BF16)
        qmla_ref[:, hd * MLA_SLAB + NOPE:(hd + 1) * MLA_SLAB] = (
            _rope_block(q[:, NOPE:], cos, sin) * scale).astype(BF16)


def _in_proj(x, g_mix, w_in_p, g_q, g_kv, w_uq_p, b_f128, cos, sin):
    n = x.shape[0]
    tm = min(IN_PROJ_TILE, n)
    assert n % tm == 0
    row = lambda w: pl.BlockSpec((tm, w), lambda i: (i, 0))
    heads = pl.BlockSpec((tm * H_A, D_HA), lambda i: (i, 0))
    out_shapes = (
        jax.ShapeDtypeStruct((n, W_A), BF16),
        jax.ShapeDtypeStruct((n * H_A, D_HA), F32),
        jax.ShapeDtypeStruct((n * H_A, D_HA), F32),
        jax.ShapeDtypeStruct((n, W_A), BF16),
        jax.ShapeDtypeStruct((n, W_A), BF16),
        jax.ShapeDtypeStruct((n, H_A), F32),
        jax.ShapeDtypeStruct((n, H_B * MLA_SLAB), BF16),
        jax.ShapeDtypeStruct((n, KV_RANK), F32),
        jax.ShapeDtypeStruct((n, LANES), F32),
        jax.ShapeDtypeStruct((n, C_C), F32),
    )
    return pl.pallas_call(
        _in_proj_kernel,
        grid=(n // tm,),
        in_specs=[row(D_MODEL), _const_spec((1, D_MODEL)), _const_spec((D_MODEL, IN_COLS)),
                  _const_spec((1, Q_RANK)), _const_spec((1, KV_RANK)),
                  _const_spec((Q_RANK, H_B * MLA_SLAB)), _const_spec((1, LANES)), row(LANES), row(LANES)],
        out_specs=[row(W_A), heads, heads, row(W_A), row(W_A), row(H_A), row(H_B * MLA_SLAB),
                   row(KV_RANK), row(LANES), row(C_C)],
        out_shape=out_shapes,
        compiler_params=_cparams("parallel"),
        name="in_proj",
    )(x, g_mix, w_in_p, g_q, g_kv, w_uq_p, b_f128, cos, sin)


def _cumsum_kernel(x_ref, o_ref):
    c = x_ref[...]
    t = c.shape[-1]
    lane = lax.broadcasted_iota(jnp.int32, c.shape, 1)
    s = 1
    while s < t:
        c = c + jnp.where(lane >= s, pltpu.roll(c, s, 1), 0.0)
        s *= 2
    o_ref[...] = c * LOG2E


def _cumsum_time(logf_t):
    b, hh, t = logf_t.shape
    spec = pl.BlockSpec((None, hh, t), lambda i: (i, 0, 0))
    return pl.pallas_call(
        _cumsum_kernel, grid=(b,), in_specs=[spec], out_specs=spec,
        out_shape=jax.ShapeDtypeStruct((b, hh, t), F32),
        compiler_params=_cparams("parallel"), name="cumsum_logf",
    )(logf_t)


def _softmax_step(s, v, row_bias, m_ref, l_ref, acc_ref):
    width = s.shape[1]
    s_max = jnp.max(s, axis=-1, keepdims=True)
    m_old = m_ref[...]
    m_new = jnp.maximum(m_old, s_max if row_bias is None else s_max + row_bias)
    alpha = jnp.exp2(m_old - m_new)
    offset = m_new if row_bias is None else m_new - row_bias
    if width % LANES == 0:
        p = [jnp.exp2(s[:, c * LANES:(c + 1) * LANES] - offset) for c in range(width // LANES)]
        l_add = functools.reduce(lambda a, b: a + b, p)
        p = p[0].astype(BF16) if len(p) == 1 else jnp.concatenate([x.astype(BF16) for x in p], axis=1)
    else:
        p = jnp.exp2(s - offset[:, :1])
        lane = lax.broadcasted_iota(jnp.int32, m_old.shape, 1)
        l_add = jnp.where(lane == 0, jnp.sum(p, axis=-1, keepdims=True), 0.0)
        p = p.astype(BF16)
    l_ref[...] = alpha * l_ref[...] + l_add
    acc_ref[...] = alpha * acc_ref[...] + jnp.dot(p, v, preferred_element_type=F32)
    m_ref[...] = m_new


def _flash_sweep(heads, n_open, d0, tq, tk, diag_mask, m_ref, l_ref, acc_ref):
    for h in range(len(heads)):
        m_ref[h] = jnp.full(m_ref.shape[1:], NEG_INF, F32)
        l_ref[h] = jnp.zeros(l_ref.shape[1:], F32)
        acc_ref[h] = jnp.zeros(acc_ref.shape[1:], F32)

    def chunk(k0, width, mask):
        for h, (q, key_chunk, value_chunk, col_bias, row_bias) in enumerate(heads):
            s = _qk(q, key_chunk(k0, width))
            if col_bias is not None:
                s = s + col_bias(k0, width)
            if mask is not None:
                s = jnp.where(mask, s, NEG_INF)
            _softmax_step(s, value_chunk(k0, width), row_bias, m_ref.at[h], l_ref.at[h], acc_ref.at[h])

    n_wide, n_narrow = n_open

    def wide_chunk(i, carry):
        chunk(pl.multiple_of(i * tk, tk), tk, None)
        return carry

    def narrow_chunk(i, carry):
        chunk(pl.multiple_of(n_wide * tk + i * tq, tq), tq, None)
        return carry

    lax.fori_loop(0, n_wide, wide_chunk, 0)
    if not (isinstance(n_narrow, int) and n_narrow == 0):
        lax.fori_loop(0, n_narrow, narrow_chunk, 0)
    chunk(d0, tq, diag_mask)
    return [acc_ref[h] / jnp.sum(l_ref[h], axis=-1, keepdims=True) for h in range(len(heads))]


def _qk(q, k):
    return lax.dot_general(q, k, (((1,), (1,)), ((), ())), preferred_element_type=F32)


def _tile_extent(n_past, tq, tk, single_tile):
    if single_tile:
        return (n_past // tk, (n_past % tk) // tq), n_past
    start = n_past + pl.program_id(2) * tq
    n_wide = start // tk
    n_narrow = 0 if tk == tq else (start - n_wide * tk) // tq
    return (n_wide, n_narrow), pl.multiple_of(start, tq)


def _fox_kernel(q_ref, k_ref, v_ref, cq_ref, ck_ref, o_ref, m_ref, l_ref, acc_ref, *, tq, tk, n_past, single_tile):
    lane = lax.broadcasted_iota(jnp.int32, (tq, LANES), 1)
    n_open, d0 = _tile_extent(n_past, tq, tk, single_tile)
    row = lax.broadcasted_iota(jnp.int32, (tq, tq), 0)
    col = lax.broadcasted_iota(jnp.int32, (tq, tq), 1)
    heads = []
    for pair in range(q_ref.shape[1] // LANES):
        lanes = slice(pair * LANES, (pair + 1) * LANES)
        q = q_ref[:, lanes]
        key_chunk = functools.partial(lambda k0, width, lanes: k_ref[pl.ds(k0, width), lanes], lanes=lanes)
        value_chunk = functools.partial(lambda k0, width, lanes: v_ref[pl.ds(k0, width), lanes], lanes=lanes)
        for j in range(2):
            qj = jnp.where((lane >= j * D_HA) & (lane < (j + 1) * D_HA), q, jnp.zeros_like(q))
            col_bias = functools.partial(
                lambda k0, width, pair, j: -ck_ref[pair, j:j + 1, pl.ds(k0, width)], pair=pair, j=j)
            row_bias = jnp.broadcast_to(cq_ref[pair, :, j:j + 1], (tq, LANES))
            heads.append((qj, key_chunk, value_chunk, col_bias, row_bias))
    outs = _flash_sweep(heads, n_open, d0, tq, tk, col <= row, m_ref, l_ref, acc_ref)
    for pair in range(q_ref.shape[1] // LANES):
        o_ref[:, pair * LANES:(pair + 1) * LANES] = jnp.where(lane < D_HA, outs[2 * pair], outs[2 * pair + 1])


def _fox_attention(q, k, v, c_q, c_k, *, n_past, tq, tk, pairs_per_step):
    b, t_q, _ = q.shape
    t_k = k.shape[1]
    t_kp = c_k.shape[-1]
    assert t_q % tq == 0 and tk % tq == 0 and n_past % tq == 0
    npair = pairs_per_step
    kern = functools.partial(_fox_kernel, tq=tq, tk=tk, n_past=n_past, single_tile=t_q == tq)
    return pl.pallas_call(
        kern,
        grid=(b, H_A // (2 * npair), t_q // tq),
        in_specs=[pl.BlockSpec((None, tq, npair * LANES), lambda bi, hp, qi: (bi, qi, hp)),
                  pl.BlockSpec((None, t_k, npair * LANES), lambda bi, hp, qi: (bi, 0, hp)),
                  pl.BlockSpec((None, t_k, npair * LANES), lambda bi, hp, qi: (bi, 0, hp)),
                  pl.BlockSpec((None, npair, tq, 2), lambda bi, hp, qi: (bi, hp, qi, 0)),
                  pl.BlockSpec((None, npair, 2, t_kp), lambda bi, hp, qi: (bi, hp, 0, 0))],
        out_specs=pl.BlockSpec((None, tq, npair * LANES), lambda bi, hp, qi: (bi, qi, hp)),
        out_shape=jax.ShapeDtypeStruct((b, t_q, W_A), F32),
        scratch_shapes=[pltpu.VMEM((2 * npair, tq, LANES), F32)] * 3,
        compiler_params=_cparams("parallel", "parallel", "arbitrary"),
        name="fox_attention",
    )(q, k, v, c_q, c_k)


def _mla_kernel(q_ref, k_ref, v_ref, o_ref, m_ref, l_ref, acc_ref, *, tq, tk, n_past, single_tile):
    n_open, d0 = _tile_extent(n_past, tq, tk, single_tile)
    row = lax.broadcasted_iota(jnp.int32, (tq, tq), 0)
    col = lax.broadcasted_iota(jnp.int32, (tq, tq), 1)
    heads = []
    for j in range(q_ref.shape[1] // MLA_SLAB):
        key_chunk = functools.partial(
            lambda k0, width, j: k_ref[pl.ds(k0, width), j * MLA_SLAB:(j + 1) * MLA_SLAB], j=j)
        value_chunk = functools.partial(lambda k0, width, j: v_ref[pl.ds(k0, width), j * V_HD:(j + 1) * V_HD], j=j)
        heads.append((q_ref[:, j * MLA_SLAB:(j + 1) * MLA_SLAB], key_chunk, value_chunk, None, None))
    mask = (d0 + col) >> _CHUNK_SHIFT <= (d0 + row) >> _CHUNK_SHIFT
    outs = _flash_sweep(heads, n_open, d0, tq, tk, mask, m_ref, l_ref, acc_ref)
    for j in range(q_ref.shape[1] // MLA_SLAB):
        o_ref[:, j * V_HD:(j + 1) * V_HD] = outs[j]


def _mla_attention(q, k, v, *, n_past, tq, tk, heads_per_step):
    b, t_q, _ = q.shape
    t_k = k.shape[1]
    assert t_q % tq == 0 and tk % tq == 0 and n_past % tq == 0
    nh = heads_per_step
    kern = functools.partial(_mla_kernel, tq=tq, tk=tk, n_past=n_past, single_tile=t_q == tq)
    return pl.pallas_call(
        kern,
        grid=(b, H_B // nh, t_q // tq),
        in_specs=[pl.BlockSpec((None, tq, nh * MLA_SLAB), lambda bi, hd, qi: (bi, qi, hd)),
                  pl.BlockSpec((None, t_k, nh * MLA_SLAB), lambda bi, hd, qi: (bi, 0, hd)),
                  pl.BlockSpec((None, t_k, nh * V_HD), lambda bi, hd, qi: (bi, 0, hd))],
        out_specs=pl.BlockSpec((None, tq, nh * V_HD), lambda bi, hd, qi: (bi, qi, hd)),
        out_shape=jax.ShapeDtypeStruct((b, t_q, W_B), F32),
        scratch_shapes=[pltpu.VMEM((nh, tq, LANES), F32), pltpu.VMEM((nh, tq, LANES), F32),
                        pltpu.VMEM((nh, tq, V_HD), F32)],
        compiler_params=_cparams("parallel", "parallel", "arbitrary"),
        name="mla_attention",
    )(q, k, v)


def _kv_up_kernel(ckv_ref, kr_ref, wuk_ref, wuv_ref, k_ref, v_ref):
    c = ckv_ref[...].astype(BF16)
    kr = kr_ref[...].astype(BF16)
    kn = jnp.dot(c, wuk_ref[...], preferred_element_type=F32).astype(BF16)
    for hd in range(H_B):
        k_ref[:, hd * MLA_SLAB:hd * MLA_SLAB + NOPE] = kn[:, hd * NOPE:(hd + 1) * NOPE]
        k_ref[:, hd * MLA_SLAB + NOPE:(hd + 1) * MLA_SLAB] = kr
    v_ref[...] = jnp.dot(c, wuv_ref[...], preferred_element_type=F32).astype(BF16)


def _kv_up(ckv_n, krope128, w_uk, w_uv):
    r = ckv_n.shape[0]
    tm = ROW_TILE
    row = lambda w: pl.BlockSpec((tm, w), lambda i: (i, 0))
    return pl.pallas_call(
        _kv_up_kernel, grid=(r // tm,),
        in_specs=[row(KV_RANK), row(LANES), _const_spec((KV_RANK, H_B * NOPE)),
                  _const_spec((KV_RANK, H_B * V_HD))],
        out_specs=[row(H_B * MLA_SLAB), row(H_B * V_HD)],
        out_shape=(jax.ShapeDtypeStruct((r, H_B * MLA_SLAB), BF16), jax.ShapeDtypeStruct((r, H_B * V_HD), BF16)),
        compiler_params=_cparams("parallel"), name="mla_kv_up",
    )(ckv_n, krope128, w_uk, w_uv)


_CONV_SUB = 64
_CONV_HALO = -(-CONV_W // SUBLANES) * SUBLANES


def _conv_kernel(xp_ref, w_ref, b_ref, g_ref, beta_ref, o_ref, *, tt):
    t0 = pl.program_id(1) * tt
    w = w_ref[...]
    rows = min(_CONV_SUB, tt)
    for sub in range(tt // rows):
        base = pl.multiple_of(t0 + sub * rows, rows)
        xa = xp_ref[pl.ds(base, rows + _CONV_HALO), :]
        acc = None
        for b in range(SUBLANES):
            z = None
            for a in range(-(-CONV_W // SUBLANES)):
                tap = SUBLANES * a + b
                if tap < CONV_W:
                    term = xa[SUBLANES * a:SUBLANES * a + rows + SUBLANES, :] * w[tap:tap + 1, :]
                    z = term if z is None else z + term
            acc = z[b:b + rows, :] if acc is None else acc + z[b:b + rows, :]
        y = acc + b_ref[...]
        mu = jnp.mean(y, axis=-1, keepdims=True)
        yc = y - mu
        var = jnp.mean(yc * yc, axis=-1, keepdims=True)
        y = yc * lax.rsqrt(var + EPS) * g_ref[...] + beta_ref[...]
        o_ref[sub * rows:(sub + 1) * rows, :] = y * jax.nn.sigmoid(y)


def _conv_module(xp, w_dw, b_dw, ln_g, ln_b, *, t, tt):
    b, t_p, _ = xp.shape
    kern = functools.partial(_conv_kernel, tt=tt)
    return pl.pallas_call(
        kern, grid=(b, t // tt),
        in_specs=[pl.BlockSpec((None, t_p, C_C), lambda bi, ti: (bi, 0, 0)),
                  _const_spec((CONV_W, C_C)), _const_spec((1, C_C)), _const_spec((1, C_C)), _const_spec((1, C_C))],
        out_specs=pl.BlockSpec((None, tt, C_C), lambda bi, ti: (bi, ti, 0)),
        out_shape=jax.ShapeDtypeStruct((b, t, C_C), F32),
        compiler_params=_cparams("parallel", "arbitrary"), name="conv_module",
    )(xp, w_dw, b_dw, ln_g, ln_b)


def _out_proj_kernel(oa_ref, ob_ref, oc_ref, x_ref, gout_ref, wout_ref, gffn_ref, wr_ref, br_ref,
                     xn_ref, h2_ref, rl_ref):
    g = gout_ref[...]
    na = _rms(oa_ref[...], g[:, :W_A]).astype(BF16)
    nb = _rms(ob_ref[...], g[:, W_A:W_A + W_B]).astype(BF16)
    nc = _rms(oc_ref[...], g[:, W_A + W_B:]).astype(BF16)
    mix = jnp.dot(na, wout_ref[:W_A, :], preferred_element_type=F32)
    mix = mix + jnp.dot(nb, wout_ref[W_A:W_A + W_B, :], preferred_element_type=F32)
    mix = mix + jnp.dot(nc, wout_ref[W_A + W_B:, :], preferred_element_type=F32)
    xn = x_ref[...] + mix
    xn_ref[...] = xn
    h2 = _rms(xn, gffn_ref[...]).astype(BF16)
    rl_ref[...] = jnp.dot(h2, wr_ref[...], preferred_element_type=F32) + br_ref[...]
    bits = pltpu.bitcast(h2.astype(F32), jnp.uint32)
    h2_ref[...] = (bits[:, :D_MODEL // 2] >> 16) | (bits[:, D_MODEL // 2:] & jnp.uint32(0xFFFF0000))


def _out_proj(o_a, o_b, o_c, x, g_out, w_out, g_ffn, w_r, b_r):
    n = x.shape[0]
    tm = ROW_TILE
    row = lambda w: pl.BlockSpec((tm, w), lambda i: (i, 0))
    return pl.pallas_call(
        _out_proj_kernel, grid=(n // tm,),
        in_specs=[row(W_A), row(W_B), row(C_C), row(D_MODEL), _const_spec((1, D_MODEL)),
                  _const_spec((D_MODEL, D_MODEL)), _const_spec((1, D_MODEL)),
                  _const_spec((D_MODEL, LANES)), _const_spec((1, LANES))],
        out_specs=[row(D_MODEL), row(D_MODEL // 2), row(LANES)],
        out_shape=(jax.ShapeDtypeStruct((n, D_MODEL), F32), jax.ShapeDtypeStruct((n, D_MODEL // 2), jnp.uint32),
                   jax.ShapeDtypeStruct((n, LANES), F32)),
        compiler_params=_cparams("parallel"), name="out_proj_router",
    )(o_a, o_b, o_c, x, g_out, w_out, g_ffn, w_r, b_r)


def _expert_kernel(be_ref, nv_ref, src_ref, dst_ref, h_hbm, wg_ref, wu_ref, wd_ref, y_hbm,
                   xbuf, ybuf, zbuf, wgb_ref, wub_ref, wdb_ref, gsem, ssem, zsem, *, n_blocks, dump_row0):
    te = EXPERT_TILE
    xt = D_MODEL // 2 // LANES
    yt = D_MODEL // LANES
    i = pl.program_id(0)
    nv = nv_ref[0]
    cur = i % 2
    nxt = 1 - cur
    valid = i < nv

    def gather_copy(tok, buf, r):
        return pltpu.make_async_copy(h_hbm.at[tok], xbuf.at[buf, pl.ds(r * (xt + 1), xt), :], gsem.at[buf])

    def scatter_copy(row, buf, r):
        return pltpu.make_async_copy(ybuf.at[buf, pl.ds(r * (yt + 1), yt), :], y_hbm.at[row], ssem.at[buf])

    def gather_block(block, buf):
        for r in range(te):
            gather_copy(src_ref[block * te + r], buf, r).start(priority=r % 2)

    def scatter_block(block, buf):
        for r in range(te):
            scatter_copy(dst_ref[block * te + r], buf, r).start(priority=(r + 1) % 2)

    def wait_gather(buf):
        for r in range(te):
            gather_copy(0, buf, 0).wait()

    def wait_scatter(buf):
        for r in range(te):
            scatter_copy(0, buf, 0).wait()

    @pl.when(i == 0)
    def _():
        gather_block(0, 0)
        zbuf[...] = jnp.zeros(zbuf.shape, F32)
        fills = [pltpu.make_async_copy(zbuf, y_hbm.at[pl.ds(dump_row0 + k * SUBLANES, SUBLANES)], zsem)
                 for k in range(EXPERT_DUMP_ROWS // SUBLANES)]
        for fill in fills:
            fill.start()
        for fill in fills:
            fill.wait()

    prev_e = be_ref[jnp.maximum(i - 1, 0)]

    @pl.when(valid & ((i == 0) | (be_ref[i] != prev_e)))
    def _():
        wgb_ref[...] = wg_ref[...].astype(BF16)
        wub_ref[...] = wu_ref[...].astype(BF16)
        wdb_ref[...] = wd_ref[...].astype(BF16)

    @pl.when(valid)
    def _():
        wait_gather(cur)

    @pl.when(valid & (i >= 2))
    def _():
        wait_scatter(cur)

    def step(scatter_previous):
        u = jnp.concatenate([xbuf[cur, pl.ds(j, te, stride=xt + 1), :] for j in range(xt)], axis=1)
        x_lo = pltpu.bitcast(u << 16, F32).astype(BF16)
        x_hi = pltpu.bitcast(u & jnp.uint32(0xFFFF0000), F32).astype(BF16)
        gather_block(jnp.minimum(i + 1, n_blocks - 1), nxt)
        if scatter_previous:
            scatter_block(i - 1, nxt)
        half = D_MODEL // 2
        gate = (jnp.dot(x_lo, wgb_ref[:half, :], preferred_element_type=F32)
                + jnp.dot(x_hi, wgb_ref[half:, :], preferred_element_type=F32))
        up = (jnp.dot(x_lo, wub_ref[:half, :], preferred_element_type=F32)
              + jnp.dot(x_hi, wub_ref[half:, :], preferred_element_type=F32))
        hid = (gate * jax.nn.sigmoid(gate) * up).astype(BF16)
        y = jnp.dot(hid, wdb_ref[...], preferred_element_type=F32)
        for j in range(yt):
            ybuf[cur, pl.ds(j, te, stride=yt + 1), :] = y[:, j * LANES:(j + 1) * LANES]

    @pl.when(valid & (i == 0))
    def _():
        step(False)

    @pl.when(valid & (i > 0))
    def _():
        step(True)

    @pl.when(i == nv - 1)
    def _():
        scatter_block(i, cur)
        wait_scatter(cur)
        wait_gather(nxt)

    @pl.when((i == nv - 1) & (i > 0))
    def _():
        wait_scatter(nxt)


def _expert_blocks(block_e, n_valid, slot_src, slot_dst, h2_tiles, w_gate, w_up, w_down, *, layer):
    n_tok = h2_tiles.shape[0]
    te = EXPERT_TILE
    n_blocks = block_e.shape[0]
    xt, yt = D_MODEL // 2 // LANES, D_MODEL // LANES
    weights = lambda r, c: pl.BlockSpec((None, None, r, c), lambda i, be, nv, src, dst: (layer, be[i], 0, 0))
    grid_spec = pltpu.PrefetchScalarGridSpec(
        num_scalar_prefetch=4, grid=(n_blocks,),
        in_specs=[pl.BlockSpec(memory_space=pl.ANY), weights(D_MODEL, D_EXPERT), weights(D_MODEL, D_EXPERT),
                  weights(D_EXPERT, D_MODEL)],
        out_specs=pl.BlockSpec(memory_space=pl.ANY),
        scratch_shapes=[pltpu.VMEM((2, te * (xt + 1), LANES), jnp.uint32),
                        pltpu.VMEM((2, te * (yt + 1), LANES), F32),
                        pltpu.VMEM((SUBLANES, yt, LANES), F32),
                        pltpu.VMEM((D_MODEL, D_EXPERT), BF16), pltpu.VMEM((D_MODEL, D_EXPERT), BF16),
                        pltpu.VMEM((D_EXPERT, D_MODEL), BF16),
                        pltpu.SemaphoreType.DMA((2,)), pltpu.SemaphoreType.DMA((2,)), pltpu.SemaphoreType.DMA(())])
    kern = functools.partial(_expert_kernel, n_blocks=n_blocks, dump_row0=TOP_K * n_tok)
    return pl.pallas_call(
        kern, grid_spec=grid_spec,
        out_shape=jax.ShapeDtypeStruct((TOP_K * n_tok + EXPERT_DUMP_ROWS, yt, LANES), F32),
        compiler_params=_cparams("arbitrary"), name="expert_blocks",
    )(block_e, n_valid, slot_src, slot_dst, h2_tiles, w_gate, w_up, w_down)


def _route(rlog):
    n = rlog.shape[0]
    idx = jnp.arange(n)
    g_logits = rlog[:, :N_GROUPS]
    g_sel = jnp.argmax(g_logits, axis=-1)
    g_w = jax.nn.softmax(g_logits, axis=-1)[idx, g_sel]
    e_logits = rlog[:, N_GROUPS:N_GROUPS + N_EXPERTS].reshape(n, N_GROUPS, EXPERTS_PER_GROUP)[idx, g_sel]
    top_v, top_i = lax.top_k(e_logits, TOP_K)
    gate = jax.nn.softmax(top_v, axis=-1) * g_w[:, None]
    expert_ids = (g_sel[:, None] * EXPERTS_PER_GROUP + top_i).astype(jnp.int32)
    return expert_ids, gate


def _dispatch_plan(expert_ids):
    n = expert_ids.shape[0]
    te = EXPERT_TILE
    a = n * TOP_K
    n_blocks = -(-a // te) + N_EXPERTS
    rows = n_blocks * te
    flat_e = expert_ids.reshape(-1)
    experts = jnp.arange(N_EXPERTS, dtype=jnp.int32)
    counts = jnp.sum((flat_e[:, None] == experts[None, :]).astype(jnp.int32), axis=0)
    padded = (counts + te - 1) // te * te
    pad_end = jnp.cumsum(padded)
    pad_start = pad_end - padded
    start = jnp.cumsum(counts) - counts
    order = jnp.argsort(flat_e).astype(jnp.int32)
    slot = jnp.arange(rows, dtype=jnp.int32)
    slot_e = jnp.minimum(jnp.sum((pad_end[None, :] <= slot[:, None]).astype(jnp.int32), axis=-1), N_EXPERTS - 1)
    slot_hot = (slot_e[:, None] == experts[None, :]).astype(jnp.int32)
    slot_rank = slot - jnp.sum(slot_hot * pad_start[None, :], axis=-1)
    filled = slot_rank < jnp.sum(slot_hot * counts[None, :], axis=-1)
    src = jnp.where(filled, jnp.sum(slot_hot * start[None, :], axis=-1) + slot_rank, 0)
    slot_a = order[src]
    slot_src = jnp.where(filled, slot_a // TOP_K, 0)
    slot_dst = jnp.where(filled, (slot_a % TOP_K) * n + slot_a // TOP_K, a + slot % EXPERT_DUMP_ROWS)
    block_e = slot_e.reshape(n_blocks, te)[:, 0]
    n_valid = (pad_end[-1] // te).astype(jnp.int32).reshape(1)
    return slot_src, slot_dst, block_e, n_valid


def _combine_kernel(x_ref, y0_ref, y1_ref, g0_ref, g1_ref, gfin_ref, o_ref, *, final):
    tm = x_ref.shape[0]
    yt = D_MODEL // LANES
    g0 = g0_ref[...]
    g1 = g1_ref[...]
    x = jnp.concatenate(
        [x_ref[:, j * LANES:(j + 1) * LANES]
         + (y0_ref[pl.ds(j, tm, stride=yt), :] * g0 + y1_ref[pl.ds(j, tm, stride=yt), :] * g1) for j in range(yt)],
        axis=1)
    o_ref[...] = _rms(x, gfin_ref[...]) if final else x


def _combine(x, ys, gate, g_final, *, row0, n_tok, final):
    n = x.shape[0]
    tm = ROW_TILE
    yt = D_MODEL // LANES
    assert row0 % tm == 0 and n_tok % tm == 0
    row = pl.BlockSpec((tm, D_MODEL), lambda i: (i, 0))
    lanes = pl.BlockSpec((tm, LANES), lambda i: (i, 0))
    expert_rows = lambda k: pl.BlockSpec((tm * yt, LANES), lambda i: (i + (row0 + k * n_tok) // tm, 0))
    return pl.pallas_call(
        functools.partial(_combine_kernel, final=final), grid=(n // tm,),
        in_specs=[row, expert_rows(0), expert_rows(1), lanes, lanes, _const_spec((1, D_MODEL))],
        out_specs=row, out_shape=jax.ShapeDtypeStruct((n, D_MODEL), F32),
        compiler_params=_cparams("parallel"), name="moe_combine",
    )(x, ys, ys, *gate, g_final)


def _rope_tables(pos):
    half = ROPE // 2
    inv = ROPE_THETA ** (-jnp.arange(half, dtype=F32) / half)
    ang = pos.astype(F32)[:, None] * inv[None, :]
    zeros = jnp.zeros((pos.shape[0], LANES - ROPE), F32)
    cos = jnp.concatenate([jnp.cos(ang), jnp.cos(ang), zeros], axis=-1)
    sin = jnp.concatenate([-jnp.sin(ang), jnp.sin(ang), zeros], axis=-1)
    return cos, sin


def _prep_w_in(w_in):
    qa, ka, va, fa, cq, ckv, kr, ua, ug = jnp.split(
        w_in, np.cumsum([W_A, W_A, W_A, H_A, Q_RANK, KV_RANK, ROPE, C_C])[:].tolist(), axis=1)
    pad = jnp.zeros((D_MODEL, LANES - ROPE - H_A), w_in.dtype)
    return jnp.concatenate([qa * (D_HA ** -0.5), ka, va, cq, ckv, ua, ug, kr, fa, pad], axis=1).astype(BF16)


def _prep_w_uq(w_uq):
    w = w_uq.reshape(Q_RANK, H_B, NOPE + ROPE)
    w = jnp.pad(w, ((0, 0), (0, 0), (0, MLA_SLAB - NOPE - ROPE)))
    return w.reshape(Q_RANK, H_B * MLA_SLAB).astype(BF16)


def _prep_w_ukv(w_ukv):
    w = w_ukv.reshape(KV_RANK, H_B, NOPE + V_HD)
    return (w[:, :, :NOPE].reshape(KV_RANK, H_B * NOPE).astype(BF16),
            w[:, :, NOPE:].reshape(KV_RANK, H_B * V_HD).astype(BF16))


def _pad_lanes(x, width):
    return jnp.pad(x, ((0, 0),) * (x.ndim - 1) + ((0, width - x.shape[-1]),))


def _round_up(x, m):
    return -(-x // m) * m


def kernel(x_prompt, x_sample, cache_fox_k, cache_fox_v, cache_fox_logf, cache_mla_ckv, cache_mla_krope, state_conv, g_mix, w_in, b_f, g_q, w_uq, g_kv, w_ukv, w_dw, b_dw, ln_g, ln_b, g_out, w_out, g_ffn, w_rg, b_rg, w_re, b_re, w_gate, w_up, w_down, g_final):
    bp, tp, _ = x_prompt.shape
    bs, ts, _ = x_sample.shape
    n_past = cache_fox_k.shape[2]
    n_p, n_s = bp * tp, bs * ts
    depth = g_mix.shape[0]

    halo = CONV_W - 1
    tkp_s = _round_up(n_past + ts, LANES)

    x_p = x_prompt.reshape(n_p, D_MODEL)
    x_s = x_sample.reshape(n_s, D_MODEL)
    cos_p, sin_p = _rope_tables(jnp.arange(tp))
    cos_s, sin_s = _rope_tables(n_past + jnp.arange(ts))
    rope_p = (jnp.tile(cos_p, (bp, 1)), jnp.tile(sin_p, (bp, 1)))
    rope_s = (jnp.tile(cos_s, (bs, 1)), jnp.tile(sin_s, (bs, 1)))

    def split_heads(c):
        return c.reshape(c.shape[0], H_A // 2, 2, c.shape[-1])

    states_p, states_s = [], []
    for l in range(depth):
        w_uk, w_uv = _prep_w_ukv(w_ukv[l])
        b_f128 = jnp.pad(b_f[l], (_FA_LANE, LANES - _FA_LANE - H_A))[None, :]
        in_w = (g_mix[l][None], _prep_w_in(w_in[l]), g_q[l][None], g_kv[l][None], _prep_w_uq(w_uq[l]), b_f128)
        qa_p, ka_p, va_p, kab_p, vab_p, lf_p, qm_p, ckv_p, kr_p, u_p = _in_proj(x_p, *in_w, *rope_p)
        qa_s, ka_s, va_s, kab_s, vab_s, lf_s, qm_s, ckv_s, kr_s, u_s = _in_proj(x_s, *in_w, *rope_s)

        lf_p = lf_p.reshape(bp, tp, H_A)
        lf_s = lf_s.reshape(bs, ts, H_A)
        c_p = _cumsum_time(lf_p.transpose(0, 2, 1))
        lf_all = jnp.concatenate([cache_fox_logf[l].astype(F32), lf_s], axis=1)
        c_s = _cumsum_time(_pad_lanes(lf_all.transpose(0, 2, 1), tkp_s))
        o_a_p = _fox_attention(
            qa_p.reshape(bp, tp, W_A), kab_p.reshape(bp, tp, W_A), vab_p.reshape(bp, tp, W_A),
            split_heads(c_p).transpose(0, 1, 3, 2), split_heads(c_p), n_past=0, tq=TQ_PROMPT, tk=TK_PROMPT,
            pairs_per_step=FOX_PAIRS_PER_STEP)
        k_all = jnp.concatenate([cache_fox_k[l].reshape(bs, n_past, W_A).astype(BF16),
                                 kab_s.reshape(bs, ts, W_A)], axis=1)
        v_all = jnp.concatenate([cache_fox_v[l].reshape(bs, n_past, W_A).astype(BF16),
                                 vab_s.reshape(bs, ts, W_A)], axis=1)
        o_a_s = _fox_attention(
            qa_s.reshape(bs, ts, W_A), k_all, v_all,
            split_heads(c_s[:, :, n_past:n_past + ts]).transpose(0, 1, 3, 2), split_heads(c_s),
            n_past=n_past, tq=ts, tk=TK_SAMPLE, pairs_per_step=H_A // 2)

        ckv_s = ckv_s.reshape(bs, ts, KV_RANK)
        kr_s = kr_s.reshape(bs, ts, LANES)
        ckv_all = jnp.concatenate([cache_mla_ckv[l].astype(F32), ckv_s], axis=1)
        kr_all = jnp.concatenate([_pad_lanes(cache_mla_krope[l].astype(F32), LANES), kr_s], axis=1)
        km_p, vm_p = _kv_up(ckv_p, kr_p, w_uk, w_uv)
        km_s, vm_s = _kv_up(ckv_all.reshape(-1, KV_RANK), kr_all.reshape(-1, LANES), w_uk, w_uv)
        o_b_p = _mla_attention(
            qm_p.reshape(bp, tp, H_B * MLA_SLAB), km_p.reshape(bp, tp, H_B * MLA_SLAB), vm_p.reshape(bp, tp, W_B),
            n_past=0, tq=TQ_PROMPT, tk=TK_PROMPT, heads_per_step=MLA_HEADS_PER_STEP)
        o_b_s = _mla_attention(
            qm_s.reshape(bs, ts, H_B * MLA_SLAB), km_s.reshape(bs, n_past + ts, H_B * MLA_SLAB),
            vm_s.reshape(bs, n_past + ts, W_B), n_past=n_past, tq=ts, tk=TK_SAMPLE, heads_per_step=H_B)

        xp_p = jnp.pad(u_p.reshape(bp, tp, C_C), ((0, 0), (halo, _CONV_HALO - halo), (0, 0)))
        xp_s = jnp.concatenate([state_conv[l].astype(F32), u_s.reshape(bs, ts, C_C)], axis=1)
        conv_p = xp_p[:, tp:tp + halo]
        conv_s = xp_s[:, ts:ts + halo]
        xp_s = jnp.pad(xp_s, ((0, 0), (0, _CONV_HALO - halo), (0, 0)))
        conv_w = (w_dw[l], b_dw[l][None], ln_g[l][None], ln_b[l][None])
        o_c_p = _conv_module(xp_p, *conv_w, t=tp, tt=CONV_TILE)
        o_c_s = _conv_module(xp_s, *conv_w, t=ts, tt=ts)

        w_r = _pad_lanes(jnp.concatenate([w_rg[l], w_re[l]], axis=1), LANES).astype(BF16)
        b_r = _pad_lanes(jnp.concatenate([b_rg[l], b_re[l].reshape(-1)])[None, :].astype(F32), LANES)
        out_w = (g_out[l][None], w_out[l].astype(BF16), g_ffn[l][None], w_r, b_r)
        x_p, h2_p, rlog_p = _out_proj(o_a_p.reshape(n_p, W_A), o_b_p.reshape(n_p, W_B), o_c_p.reshape(n_p, C_C),
                                      x_p, *out_w)
        x_s, h2_s, rlog_s = _out_proj(o_a_s.reshape(n_s, W_A), o_b_s.reshape(n_s, W_B), o_c_s.reshape(n_s, C_C),
                                      x_s, *out_w)

        expert_ids, gate = _route(jnp.concatenate([rlog_p, rlog_s], axis=0))
        slot_src, slot_dst, block_e, n_valid = _dispatch_plan(expert_ids)
        h2_tiles = jnp.concatenate([h2_p, h2_s], axis=0).reshape(n_p + n_s, D_MODEL // 2 // LANES, LANES)
        ys = _expert_blocks(block_e, n_valid, slot_src, slot_dst, h2_tiles, w_gate, w_up, w_down, layer=l)
        final = l == depth - 1
        ys = ys.reshape(-1, LANES)
        gates = lambda rows: [jnp.broadcast_to(gate[rows, k:k + 1], (gate[rows].shape[0], LANES))
                              for k in range(TOP_K)]
        x_p = _combine(x_p, ys, gates(slice(0, n_p)), g_final[None], row0=0, n_tok=n_p + n_s, final=final)
        x_s = _combine(x_s, ys, gates(slice(n_p, None)), g_final[None], row0=n_p, n_tok=n_p + n_s, final=final)

        states_p.append((ka_p.reshape(bp, tp, H_A, D_HA), va_p.reshape(bp, tp, H_A, D_HA), lf_p,
                         ckv_p.reshape(bp, tp, KV_RANK), kr_p[:, :ROPE].reshape(bp, tp, ROPE), conv_p))
        states_s.append((ka_s.reshape(bs, ts, H_A, D_HA), va_s.reshape(bs, ts, H_A, D_HA), lf_s,
                         ckv_s, kr_s[:, :, :ROPE], conv_s))

    y_p = x_p.reshape(bp, tp, D_MODEL)
    y_s = x_s.reshape(bs, ts, D_MODEL)
    p_out = [jnp.stack(a) for a in zip(*states_p)]
    s_out = [jnp.stack(a) for a in zip(*states_s)]
    return (y_p, y_s, *p_out, *s_out)
```

```python
import functools

import numpy as np
import jax
import jax.numpy as jnp
from jax import lax
from jax.experimental import pallas as pl
from jax.experimental.pallas import tpu as pltpu

F32 = jnp.float32
BF16 = jnp.bfloat16

D_MODEL = 2048
DEPTH = 4
CHUNK = 64
H_A, D_HA = 8, 64
W_A = H_A * D_HA
H_B, NOPE, ROPE, V_HD = 8, 128, 64, 128
Q_RANK, KV_RANK = 512, 256
W_B = H_B * V_HD
C_C = D_MODEL - W_A - W_B
CONV_W = 31
_CHUNK_SHIFT = CHUNK.bit_length() - 1
assert 1 << _CHUNK_SHIFT == CHUNK
ROPE_THETA = 10000.0
N_GROUPS, EXPERTS_PER_GROUP = 4, 8
N_EXPERTS = N_GROUPS * EXPERTS_PER_GROUP
TOP_K = 2
D_EXPERT = 512
EPS = 1e-6
NEG_INF = -1e30
LOG2E = 1.4426950408889634

LANES = 128
SUBLANES = 8
TQ_PROMPT = 512
TK_PROMPT = 1024
TK_SAMPLE = 1024
MLA_HEADS_PER_STEP = 4
FOX_PAIRS_PER_STEP = 2
CONV_TILE = 256
MLA_SLAB = NOPE + LANES
ROW_TILE = 256
IN_PROJ_TILE = 512
EXPERT_TILE = 256
EXPERT_DUMP_ROWS = 2 * EXPERT_TILE
VMEM_LIMIT = 56 * 1024 * 1024

_C_QA, _C_KA, _C_VA, _C_CQ, _C_CKV, _C_UA, _C_UG, _C_KRF = 0, 512, 1024, 1536, 2048, 2304, 2816, 3328
IN_COLS = _C_KRF + LANES
_FA_LANE = ROPE


def _cparams(*sem):
    return pltpu.CompilerParams(dimension_semantics=sem, vmem_limit_bytes=VMEM_LIMIT)


def _const_spec(shape):
    nd = len(shape)
    return pl.BlockSpec(shape, lambda *_: (0,) * nd)


def _rms(x, g):
    return x * lax.rsqrt(jnp.mean(x * x, axis=-1, keepdims=True) + EPS) * g


def _rope_block(x, cos, sin_signed):
    lane = lax.broadcasted_iota(jnp.int32, x.shape, 1)
    partner = jnp.where(lane < ROPE // 2, pltpu.roll(x, LANES - ROPE // 2, 1), pltpu.roll(x, ROPE // 2, 1))
    return x * cos + partner * sin_signed


def _store_heads(o_ref, x):
    rows = x.shape[0]
    for hd in range(H_A):
        o_ref[pl.ds(hd, rows, stride=H_A), :] = x[:, hd * D_HA:(hd + 1) * D_HA]


def _in_proj_kernel(x_ref, gmix_ref, win_ref, gq_ref, gkv_ref, wuq_ref, bf_ref, cos_ref, sin_ref,
                    qa_ref, ka_ref, va_ref, kab_ref, vab_ref, logf_ref, qmla_ref, ckv_ref, kr_ref, u_ref):
    h = _rms(x_ref[...], gmix_ref[...]).astype(BF16)

    def proj(c0, width):
        return jnp.dot(h, win_ref[:, c0:c0 + width], preferred_element_type=F32)

    qa_ref[...] = (proj(_C_QA, W_A) * LOG2E).astype(BF16)
    ka = proj(_C_KA, W_A)
    _store_heads(ka_ref, ka)
    kab_ref[...] = ka.astype(BF16)
    va = proj(_C_VA, W_A)
    _store_heads(va_ref, va)
    vab_ref[...] = va.astype(BF16)

    cos = cos_ref[...]
    sin = sin_ref[...]
    krf = proj(_C_KRF, LANES)
    kr_ref[...] = _rope_block(krf, cos, sin)
    z = krf + bf_ref[...]
    logsig = jnp.minimum(z, 0.0) - jnp.log(1.0 + jnp.exp(-jnp.abs(z)))
    logf_ref[...] = logsig[:, _FA_LANE:_FA_LANE + H_A]

    ckv_ref[...] = _rms(proj(_C_CKV, KV_RANK), gkv_ref[...])

    u_ref[...] = proj(_C_UA, C_C) * jax.nn.sigmoid(proj(_C_UG, C_C))

    cqn = _rms(proj(_C_CQ, Q_RANK), gq_ref[...]).astype(BF16)
    scale = (NOPE + ROPE) ** -0.5 * LOG2E
    for hd in range(H_B):
        q = jnp.dot(cqn, wuq_ref[:, hd * MLA_SLAB:(hd + 1) * MLA_SLAB], preferred_element_type=F32)
        qmla_ref[:, hd * MLA_SLAB:hd * MLA_SLAB + NOPE] = (q[:, :NOPE] * scale).astype(BF16)
        qmla_ref[:, hd * MLA_SLAB + NOPE:(hd + 1) * MLA_SLAB] = (
            _rope_block(q[:, NOPE:], cos, sin) * scale).astype(BF16)


def _in_proj(x, g_mix, w_in_p, g_q, g_kv, w_uq_p, b_f128, cos, sin):
    n = x.shape[0]
    tm = min(IN_PROJ_TILE, n)
    assert n % tm == 0
    row = lambda w: pl.BlockSpec((tm, w), lambda i: (i, 0))
    heads = pl.BlockSpec((tm * H_A, D_HA), lambda i: (i, 0))
    out_shapes = (
        jax.ShapeDtypeStruct((n, W_A), BF16),
        jax.ShapeDtypeStruct((n * H_A, D_HA), F32),
        jax.ShapeDtypeStruct((n * H_A, D_HA), F32),
        jax.ShapeDtypeStruct((n, W_A), BF16),
        jax.ShapeDtypeStruct((n, W_A), BF16),
        jax.ShapeDtypeStruct((n, H_A), F32),
        jax.ShapeDtypeStruct((n, H_B * MLA_SLAB), BF16),
        jax.ShapeDtypeStruct((n, KV_RANK), F32),
        jax.ShapeDtypeStruct((n, LANES), F32),
        jax.ShapeDtypeStruct((n, C_C), F32),
    )
    return pl.pallas_call(
        _in_proj_kernel,
        grid=(n // tm,),
        in_specs=[row(D_MODEL), _const_spec((1, D_MODEL)), _const_spec((D_MODEL, IN_COLS)),
                  _const_spec((1, Q_RANK)), _const_spec((1, KV_RANK)),
                  _const_spec((Q_RANK, H_B * MLA_SLAB)), _const_spec((1, LANES)), row(LANES), row(LANES)],
        out_specs=[row(W_A), heads, heads, row(W_A), row(W_A), row(H_A), row(H_B * MLA_SLAB),
                   row(KV_RANK), row(LANES), row(C_C)],
        out_shape=out_shapes,
        compiler_params=_cparams("parallel"),
        name="in_proj",
    )(x, g_mix, w_in_p, g_q, g_kv, w_uq_p, b_f128, cos, sin)


def _cumsum_kernel(x_ref, o_ref):
    c = x_ref[...]
    t = c.shape[-1]
    lane = lax.broadcasted_iota(jnp.int32, c.shape, 1)
    s = 1
    while s < t:
        c = c + jnp.where(lane >= s, pltpu.roll(c, s, 1), 0.0)
        s *= 2
    o_ref[...] = c * LOG2E


def _cumsum_time(logf_t):
    b, hh, t = logf_t.shape
    spec = pl.BlockSpec((None, hh, t), lambda i: (i, 0, 0))
    return pl.pallas_call(
        _cumsum_kernel, grid=(b,), in_specs=[spec], out_specs=spec,
        out_shape=jax.ShapeDtypeStruct((b, hh, t), F32),
        compiler_params=_cparams("parallel"), name="cumsum_logf",
    )(logf_t)


def _softmax_step(s, v, row_bias, m_ref, l_ref, acc_ref):
    width = s.shape[1]
    s_max = jnp.max(s, axis=-1, keepdims=True)
    m_old = m_ref[...]
    m_new = jnp.maximum(m_old, s_max if row_bias is None else s_max + row_bias)
    alpha = jnp.exp2(m_old - m_new)
    offset = m_new if row_bias is None else m_new - row_bias
    if width % LANES == 0:
        p = [jnp.exp2(s[:, c * LANES:(c + 1) * LANES] - offset) for c in range(width // LANES)]
        l_add = functools.reduce(lambda a, b: a + b, p)
        p = p[0].astype(BF16) if len(p) == 1 else jnp.concatenate([x.astype(BF16) for x in p], axis=1)
    else:
        p = jnp.exp2(s - offset[:, :1])
        lane = lax.broadcasted_iota(jnp.int32, m_old.shape, 1)
        l_add = jnp.where(lane == 0, jnp.sum(p, axis=-1, keepdims=True), 0.0)
        p = p.astype(BF16)
    l_ref[...] = alpha * l_ref[...] + l_add
    acc_ref[...] = alpha * acc_ref[...] + jnp.dot(p, v, preferred_element_type=F32)
    m_ref[...] = m_new


def _flash_sweep(heads, n_open, d0, tq, tk, visible, split_diagonal, m_ref, l_ref, acc_ref):
    for h in range(len(heads)):
        m_ref[h] = jnp.full(m_ref.shape[1:], NEG_INF, F32)
        l_ref[h] = jnp.zeros(l_ref.shape[1:], F32)
        acc_ref[h] = jnp.zeros(acc_ref.shape[1:], F32)

    def chunk(k0, width, masked=False, r0=0, c0=0):
        nr = tq - r0
        rows = (slice(None),) if r0 == 0 else (pl.ds(r0, nr),)
        for h, (q, key_chunk, value_chunk, col_bias, row_bias) in enumerate(heads):
            s = _qk(q[r0:], key_chunk(k0, width))
            if col_bias is not None:
                s = s + col_bias(k0, width)
            if masked:
                r = r0 + lax.broadcasted_iota(jnp.int32, (nr, width), 0)
                c = c0 + lax.broadcasted_iota(jnp.int32, (nr, width), 1)
                s = jnp.where(visible(r, c), s, NEG_INF)
            _softmax_step(s, value_chunk(k0, width), None if row_bias is None else row_bias[r0:],
                          m_ref.at[(h, *rows)], l_ref.at[(h, *rows)], acc_ref.at[(h, *rows)])

    n_wide, n_narrow = n_open

    def wide_chunk(i, carry):
        chunk(pl.multiple_of(i * tk, tk), tk)
        return carry

    def narrow_chunk(i, carry):
        chunk(pl.multiple_of(n_wide * tk + i * tq, tq), tq)
        return carry

    lax.fori_loop(0, n_wide, wide_chunk, 0)
    if not (isinstance(n_narrow, int) and n_narrow == 0):
        lax.fori_loop(0, n_narrow, narrow_chunk, 0)
    if split_diagonal:
        half = tq // 2
        chunk(d0, half, masked=True)
        chunk(pl.multiple_of(d0 + half, half), half, masked=True, r0=half, c0=half)
    else:
        chunk(d0, tq, masked=True)
    return [acc_ref[h] / jnp.sum(l_ref[h], axis=-1, keepdims=True) for h in range(len(heads))]


def _split_diagonal(tq, n_past, single_tile):
    half = tq // 2
    return not single_tile and half % LANES == 0 and half % CHUNK == 0 and n_past % CHUNK == 0


def _qk(q, k):
    return lax.dot_general(q, k, (((1,), (1,)), ((), ())), preferred_element_type=F32)


def _tile_extent(n_past, tq, tk, single_tile):
    if single_tile:
        return (n_past // tk, (n_past % tk) // tq), n_past
    start = n_past + pl.program_id(2) * tq
    n_wide = start // tk
    n_narrow = 0 if tk == tq else (start - n_wide * tk) // tq
    return (n_wide, n_narrow), pl.multiple_of(start, tq)


def _fox_kernel(q_ref, k_ref, v_ref, cq_ref, ck_ref, o_ref, m_ref, l_ref, acc_ref, *, tq, tk, n_past, single_tile):
    lane = lax.broadcasted_iota(jnp.int32, (tq, LANES), 1)
    n_open, d0 = _tile_extent(n_past, tq, tk, single_tile)
    heads = []
    for pair in range(q_ref.shape[1] // LANES):
        lanes = slice(pair * LANES, (pair + 1) * LANES)
        q = q_ref[:, lanes]
        key_chunk = functools.partial(lambda k0, width, lanes: k_ref[pl.ds(k0, width), lanes], lanes=lanes)
        value_chunk = functools.partial(lambda k0, width, lanes: v_ref[pl.ds(k0, width), lanes], lanes=lanes)
        for j in range(2):
            qj = jnp.where((lane >= j * D_HA) & (lane < (j + 1) * D_HA), q, jnp.zeros_like(q))
            col_bias = functools.partial(
                lambda k0, width, pair, j: -ck_ref[pair, j:j + 1, pl.ds(k0, width)], pair=pair, j=j)
            row_bias = jnp.broadcast_to(cq_ref[pair, :, j:j + 1], (tq, LANES))
            heads.append((qj, key_chunk, value_chunk, col_bias, row_bias))
    outs = _flash_sweep(heads, n_open, d0, tq, tk, lambda r, c: c <= r, _split_diagonal(tq, n_past, single_tile),
                        m_ref, l_ref, acc_ref)
    for pair in range(q_ref.shape[1] // LANES):
        o_ref[:, pair * LANES:(pair + 1) * LANES] = jnp.where(lane < D_HA, outs[2 * pair], outs[2 * pair + 1])


def _fox_attention(q, k, v, c_q, c_k, *, n_past, tq, tk, pairs_per_step):
    b, t_q, _ = q.shape
    t_k = k.shape[1]
    t_kp = c_k.shape[-1]
    assert t_q % tq == 0 and tk % tq == 0 and n_past % tq == 0
    npair = pairs_per_step
    kern = functools.partial(_fox_kernel, tq=tq, tk=tk, n_past=n_past, single_tile=t_q == tq)
    return pl.pallas_call(
        kern,
        grid=(b, H_A // (2 * npair), t_q // tq),
        in_specs=[pl.BlockSpec((None, tq, npair * LANES), lambda bi, hp, qi: (bi, qi, hp)),
                  pl.BlockSpec((None, t_k, npair * LANES), lambda bi, hp, qi: (bi, 0, hp)),
                  pl.BlockSpec((None, t_k, npair * LANES), lambda bi, hp, qi: (bi, 0, hp)),
                  pl.BlockSpec((None, npair, tq, 2), lambda bi, hp, qi: (bi, hp, qi, 0)),
                  pl.BlockSpec((None, npair, 2, t_kp), lambda bi, hp, qi: (bi, hp, 0, 0))],
        out_specs=pl.BlockSpec((None, tq, npair * LANES), lambda bi, hp, qi: (bi, qi, hp)),
        out_shape=jax.ShapeDtypeStruct((b, t_q, W_A), F32),
        scratch_shapes=[pltpu.VMEM((2 * npair, tq, LANES), F32)] * 3,
        compiler_params=_cparams("parallel", "parallel", "arbitrary"),
        name="fox_attention",
    )(q, k, v, c_q, c_k)


def _mla_kernel(q_ref, kn_ref, kr_ref, v_ref, o_ref, m_ref, l_ref, acc_ref, *, tq, tk, n_past, single_tile):
    n_open, d0 = _tile_extent(n_past, tq, tk, single_tile)
    heads = []
    for j in range(q_ref.shape[1] // MLA_SLAB):
        key_chunk = functools.partial(
            lambda k0, width, j: jnp.concatenate(
                [kn_ref[pl.ds(k0, width), j * NOPE:(j + 1) * NOPE], kr_ref[pl.ds(k0, width), :]], axis=1), j=j)
        value_chunk = functools.partial(lambda k0, width, j: v_ref[pl.ds(k0, width), j * V_HD:(j + 1) * V_HD], j=j)
        heads.append((q_ref[:, j * MLA_SLAB:(j + 1) * MLA_SLAB], key_chunk, value_chunk, None, None))
    visible = lambda r, c: (d0 + c) >> _CHUNK_SHIFT <= (d0 + r) >> _CHUNK_SHIFT
    outs = _flash_sweep(heads, n_open, d0, tq, tk, visible, _split_diagonal(tq, n_past, single_tile),
                        m_ref, l_ref, acc_ref)
    for j in range(q_ref.shape[1] // MLA_SLAB):
        o_ref[:, j * V_HD:(j + 1) * V_HD] = outs[j]


def _mla_attention(q, k_nope, k_rope, v, *, n_past, tq, tk, heads_per_step):
    b, t_q, _ = q.shape
    t_k = k_nope.shape[1]
    assert t_q % tq == 0 and tk % tq == 0 and n_past % tq == 0
    nh = heads_per_step
    kern = functools.partial(_mla_kernel, tq=tq, tk=tk, n_past=n_past, single_tile=t_q == tq)
    return pl.pallas_call(
        kern,
        grid=(b, H_B // nh, t_q // tq),
        in_specs=[pl.BlockSpec((None, tq, nh * MLA_SLAB), lambda bi, hd, qi: (bi, qi, hd)),
                  pl.BlockSpec((None, t_k, nh * NOPE), lambda bi, hd, qi: (bi, 0, hd)),
                  pl.BlockSpec((None, t_k, LANES), lambda bi, hd, qi: (bi, 0, 0)),
                  pl.BlockSpec((None, t_k, nh * V_HD), lambda bi, hd, qi: (bi, 0, hd))],
        out_specs=pl.BlockSpec((None, tq, nh * V_HD), lambda bi, hd, qi: (bi, qi, hd)),
        out_shape=jax.ShapeDtypeStruct((b, t_q, W_B), F32),
        scratch_shapes=[pltpu.VMEM((nh, tq, LANES), F32), pltpu.VMEM((nh, tq, LANES), F32),
                        pltpu.VMEM((nh, tq, V_HD), F32)],
        compiler_params=_cparams("parallel", "parallel", "arbitrary"),
        name="mla_attention",
    )(q, k_nope, k_rope, v)


def _kv_up_kernel(ckv_ref, kr_ref, wuk_ref, wuv_ref, kn_ref, krb_ref, v_ref):
    c = ckv_ref[...].astype(BF16)
    krb_ref[...] = kr_ref[...].astype(BF16)
    kn_ref[...] = jnp.dot(c, wuk_ref[...], preferred_element_type=F32).astype(BF16)
    v_ref[...] = jnp.dot(c, wuv_ref[...], preferred_element_type=F32).astype(BF16)


def _kv_up(ckv_n, krope128, w_uk, w_uv):
    r = ckv_n.shape[0]
    tm = ROW_TILE
    row = lambda w: pl.BlockSpec((tm, w), lambda i: (i, 0))
    return pl.pallas_call(
        _kv_up_kernel, grid=(r // tm,),
        in_specs=[row(KV_RANK), row(LANES), _const_spec((KV_RANK, H_B * NOPE)),
                  _const_spec((KV_RANK, H_B * V_HD))],
        out_specs=[row(H_B * NOPE), row(LANES), row(H_B * V_HD)],
        out_shape=(jax.ShapeDtypeStruct((r, H_B * NOPE), BF16), jax.ShapeDtypeStruct((r, LANES), BF16),
                   jax.ShapeDtypeStruct((r, H_B * V_HD), BF16)),
        compiler_params=_cparams("parallel"), name="mla_kv_up",
    )(ckv_n, krope128, w_uk, w_uv)


_CONV_SUB = 64
_CONV_HALO = -(-CONV_W // SUBLANES) * SUBLANES


def _conv_kernel(xp_ref, w_ref, b_ref, g_ref, beta_ref, o_ref, *, tt):
    t0 = pl.program_id(1) * tt
    w = w_ref[...]
    rows = min(_CONV_SUB, tt)
    for sub in range(tt // rows):
        base = pl.multiple_of(t0 + sub * rows, rows)
        xa = xp_ref[pl.ds(base, rows + _CONV_HALO), :]
        acc = None
        for b in range(SUBLANES):
            z = None
            for a in range(-(-CONV_W // SUBLANES)):
                tap = SUBLANES * a + b
                if tap < CONV_W:
                    term = xa[SUBLANES * a:SUBLANES * a + rows + SUBLANES, :] * w[tap:tap + 1, :]
                    z = term if z is None else z + term
            acc = z[b:b + rows, :] if acc is None else acc + z[b:b + rows, :]
        y = acc + b_ref[...]
        mu = jnp.mean(y, axis=-1, keepdims=True)
        yc = y - mu
        var = jnp.mean(yc * yc, axis=-1, keepdims=True)
        y = yc * lax.rsqrt(var + EPS) * g_ref[...] + beta_ref[...]
        o_ref[sub * rows:(sub + 1) * rows, :] = y * jax.nn.sigmoid(y)


def _conv_module(xp, w_dw, b_dw, ln_g, ln_b, *, t, tt):
    b, t_p, _ = xp.shape
    kern = functools.partial(_conv_kernel, tt=tt)
    return pl.pallas_call(
        kern, grid=(b, t // tt),
        in_specs=[pl.BlockSpec((None, t_p, C_C), lambda bi, ti: (bi, 0, 0)),
                  _const_spec((CONV_W, C_C)), _const_spec((1, C_C)), _const_spec((1, C_C)), _const_spec((1, C_C))],
        out_specs=pl.BlockSpec((None, tt, C_C), lambda bi, ti: (bi, ti, 0)),
        out_shape=jax.ShapeDtypeStruct((b, t, C_C), F32),
        compiler_params=_cparams("parallel", "arbitrary"), name="conv_module",
    )(xp, w_dw, b_dw, ln_g, ln_b)


def _out_proj_kernel(oa_ref, ob_ref, oc_ref, x_ref, gout_ref, wout_ref, gffn_ref, wr_ref, br_ref,
                     xn_ref, h2_ref, rl_ref):
    g = gout_ref[...]
    na = _rms(oa_ref[...], g[:, :W_A]).astype(BF16)
    nb = _rms(ob_ref[...], g[:, W_A:W_A + W_B]).astype(BF16)
    nc = _rms(oc_ref[...], g[:, W_A + W_B:]).astype(BF16)
    mix = jnp.dot(na, wout_ref[:W_A, :], preferred_element_type=F32)
    mix = mix + jnp.dot(nb, wout_ref[W_A:W_A + W_B, :], preferred_element_type=F32)
    mix = mix + jnp.dot(nc, wout_ref[W_A + W_B:, :], preferred_element_type=F32)
    xn = x_ref[...] + mix
    xn_ref[...] = xn
    h2 = _rms(xn, gffn_ref[...]).astype(BF16)
    rl_ref[...] = jnp.dot(h2, wr_ref[...], preferred_element_type=F32) + br_ref[...]
    bits = pltpu.bitcast(h2.astype(F32), jnp.uint32)
    h2_ref[...] = (bits[:, :D_MODEL // 2] >> 16) | (bits[:, D_MODEL // 2:] & jnp.uint32(0xFFFF0000))


def _out_proj(o_a, o_b, o_c, x, g_out, w_out, g_ffn, w_r, b_r):
    n = x.shape[0]
    tm = ROW_TILE
    row = lambda w: pl.BlockSpec((tm, w), lambda i: (i, 0))
    return pl.pallas_call(
        _out_proj_kernel, grid=(n // tm,),
        in_specs=[row(W_A), row(W_B), row(C_C), row(D_MODEL), _const_spec((1, D_MODEL)),
                  _const_spec((D_MODEL, D_MODEL)), _const_spec((1, D_MODEL)),
                  _const_spec((D_MODEL, LANES)), _const_spec((1, LANES))],
        out_specs=[row(D_MODEL), row(D_MODEL // 2), row(LANES)],
        out_shape=(jax.ShapeDtypeStruct((n, D_MODEL), F32), jax.ShapeDtypeStruct((n, D_MODEL // 2), jnp.uint32),
                   jax.ShapeDtypeStruct((n, LANES), F32)),
        compiler_params=_cparams("parallel"), name="out_proj_router",
    )(o_a, o_b, o_c, x, g_out, w_out, g_ffn, w_r, b_r)


def _expert_kernel(be_ref, nv_ref, src_ref, dst_ref, h_hbm, wg_ref, wu_ref, wd_ref, y_hbm,
                   xbuf, ybuf, zbuf, wgb_ref, wub_ref, wdb_ref, gsem, ssem, zsem, *, n_blocks, dump_row0):
    te = EXPERT_TILE
    xt = D_MODEL // 2 // LANES
    yt = D_MODEL // LANES
    i = pl.program_id(0)
    nv = nv_ref[0]
    cur = i % 2
    nxt = 1 - cur
    valid = i < nv

    def gather_copy(tok, buf, r):
        return pltpu.make_async_copy(h_hbm.at[tok], xbuf.at[buf, pl.ds(r * (xt + 1), xt), :], gsem.at[buf])

    def scatter_copy(row, buf, r):
        return pltpu.make_async_copy(ybuf.at[buf, pl.ds(r * (yt + 1), yt), :], y_hbm.at[row], ssem.at[buf])

    def gather_block(block, buf):
        for r in range(te):
            gather_copy(src_ref[block * te + r], buf, r).start(priority=1)

    def scatter_block(block, buf):
        for r in range(te):
            scatter_copy(dst_ref[block * te + r], buf, r).start(priority=r % 2)

    def wait_gather(buf):
        for r in range(te):
            gather_copy(0, buf, 0).wait()

    def wait_scatter(buf):
        for r in range(te):
            scatter_copy(0, buf, 0).wait()

    @pl.when(i == 0)
    def _():
        gather_block(0, 0)
        zbuf[...] = jnp.zeros(zbuf.shape, F32)
        fills = [pltpu.make_async_copy(zbuf, y_hbm.at[pl.ds(dump_row0 + k * SUBLANES, SUBLANES)], zsem)
                 for k in range(EXPERT_DUMP_ROWS // SUBLANES)]
        for fill in fills:
            fill.start()
        for fill in fills:
            fill.wait()

    prev_e = be_ref[jnp.maximum(i - 1, 0)]

    @pl.when(valid & ((i == 0) | (be_ref[i] != prev_e)))
    def _():
        wgb_ref[...] = wg_ref[...].astype(BF16)
        wub_ref[...] = wu_ref[...].astype(BF16)
        wdb_ref[...] = wd_ref[...].astype(BF16)

    @pl.when(valid)
    def _():
        wait_gather(cur)

    @pl.when(valid & (i >= 2))
    def _():
        wait_scatter(cur)

    def step(scatter_previous):
        gather_block(jnp.minimum(i + 1, n_blocks - 1), nxt)
        if scatter_previous:
            scatter_block(i - 1, nxt)
        u = jnp.concatenate([xbuf[cur, pl.ds(j, te, stride=xt + 1), :] for j in range(xt)], axis=1)
        x_lo = pltpu.bitcast(u << 16, F32).astype(BF16)
        x_hi = pltpu.bitcast(u & jnp.uint32(0xFFFF0000), F32).astype(BF16)
        half = D_MODEL // 2
        gate = (jnp.dot(x_lo, wgb_ref[:half, :], preferred_element_type=F32)
                + jnp.dot(x_hi, wgb_ref[half:, :], preferred_element_type=F32))
        up = (jnp.dot(x_lo, wub_ref[:half, :], preferred_element_type=F32)
              + jnp.dot(x_hi, wub_ref[half:, :], preferred_element_type=F32))
        hid = (gate * jax.nn.sigmoid(gate) * up).astype(BF16)
        y = jnp.dot(hid, wdb_ref[...], preferred_element_type=F32)
        for j in range(yt):
            ybuf[cur, pl.ds(j, te, stride=yt + 1), :] = y[:, j * LANES:(j + 1) * LANES]

    @pl.when(valid & (i == 0))
    def _():
        step(False)

    @pl.when(valid & (i > 0))
    def _():
        step(True)

    @pl.when(i == nv - 1)
    def _():
        scatter_block(i, cur)
        wait_scatter(cur)
        wait_gather(nxt)

    @pl.when((i == nv - 1) & (i > 0))
    def _():
        wait_scatter(nxt)


def _expert_blocks(block_e, n_valid, slot_src, slot_dst, h2_tiles, w_gate, w_up, w_down, *, layer):
    n_tok = h2_tiles.shape[0]
    te = EXPERT_TILE
    n_blocks = block_e.shape[0]
    xt, yt = D_MODEL // 2 // LANES, D_MODEL // LANES
    weights = lambda r, c: pl.BlockSpec((None, None, r, c), lambda i, be, nv, src, dst: (layer, be[i], 0, 0))
    grid_spec = pltpu.PrefetchScalarGridSpec(
        num_scalar_prefetch=4, grid=(n_blocks,),
        in_specs=[pl.BlockSpec(memory_space=pl.ANY), weights(D_MODEL, D_EXPERT), weights(D_MODEL, D_EXPERT),
                  weights(D_EXPERT, D_MODEL)],
        out_specs=pl.BlockSpec(memory_space=pl.ANY),
        scratch_shapes=[pltpu.VMEM((2, te * (xt + 1), LANES), jnp.uint32),
                        pltpu.VMEM((2, te * (yt + 1), LANES), F32),
                        pltpu.VMEM((SUBLANES, yt, LANES), F32),
                        pltpu.VMEM((D_MODEL, D_EXPERT), BF16), pltpu.VMEM((D_MODEL, D_EXPERT), BF16),
                        pltpu.VMEM((D_EXPERT, D_MODEL), BF16),
                        pltpu.SemaphoreType.DMA((2,)), pltpu.SemaphoreType.DMA((2,)), pltpu.SemaphoreType.DMA(())])
    kern = functools.partial(_expert_kernel, n_blocks=n_blocks, dump_row0=TOP_K * n_tok)
    return pl.pallas_call(
        kern, grid_spec=grid_spec,
        out_shape=jax.ShapeDtypeStruct((TOP_K * n_tok + EXPERT_DUMP_ROWS, yt, LANES), F32),
        compiler_params=_cparams("arbitrary"), name="expert_blocks",
    )(block_e, n_valid, slot_src, slot_dst, h2_tiles, w_gate, w_up, w_down)


def _route(rlog):
    n = rlog.shape[0]
    idx = jnp.arange(n)
    g_logits = rlog[:, :N_GROUPS]
    g_sel = jnp.argmax(g_logits, axis=-1)
    g_w = jax.nn.softmax(g_logits, axis=-1)[idx, g_sel]
    e_logits = rlog[:, N_GROUPS:N_GROUPS + N_EXPERTS].reshape(n, N_GROUPS, EXPERTS_PER_GROUP)[idx, g_sel]
    top_v, top_i = lax.top_k(e_logits, TOP_K)
    gate = jax.nn.softmax(top_v, axis=-1) * g_w[:, None]
    expert_ids = (g_sel[:, None] * EXPERTS_PER_GROUP + top_i).astype(jnp.int32)
    return expert_ids, gate


def _dispatch_plan(expert_ids):
    n = expert_ids.shape[0]
    te = EXPERT_TILE
    a = n * TOP_K
    n_blocks = -(-a // te) + N_EXPERTS
    rows = n_blocks * te
    flat_e = expert_ids.reshape(-1)
    experts = jnp.arange(N_EXPERTS, dtype=jnp.int32)
    counts = jnp.sum((flat_e[:, None] == experts[None, :]).astype(jnp.int32), axis=0)
    padded = (counts + te - 1) // te * te
    pad_end = jnp.cumsum(padded)
    pad_start = pad_end - padded
    start = jnp.cumsum(counts) - counts
    order = jnp.argsort(flat_e).astype(jnp.int32)
    slot = jnp.arange(rows, dtype=jnp.int32)
    slot_e = jnp.minimum(jnp.sum((pad_end[None, :] <= slot[:, None]).astype(jnp.int32), axis=-1), N_EXPERTS - 1)
    slot_hot = (slot_e[:, None] == experts[None, :]).astype(jnp.int32)
    slot_rank = slot - jnp.sum(slot_hot * pad_start[None, :], axis=-1)
    filled = slot_rank < jnp.sum(slot_hot * counts[None, :], axis=-1)
    src = jnp.where(filled, jnp.sum(slot_hot * start[None, :], axis=-1) + slot_rank, 0)
    slot_a = order[src]
    slot_src = jnp.where(filled, slot_a // TOP_K, 0)
    slot_dst = jnp.where(filled, (slot_a % TOP_K) * n + slot_a // TOP_K, a + slot % EXPERT_DUMP_ROWS)
    block_e = slot_e.reshape(n_blocks, te)[:, 0]
    n_valid = (pad_end[-1] // te).astype(jnp.int32).reshape(1)
    return slot_src, slot_dst, block_e, n_valid


def _combine_kernel(x_ref, y0_ref, y1_ref, g0_ref, g1_ref, gfin_ref, o_ref, *, final):
    tm = x_ref.shape[0]
    yt = D_MODEL // LANES
    g0 = g0_ref[...]
    g1 = g1_ref[...]
    x = jnp.concatenate(
        [x_ref[:, j * LANES:(j + 1) * LANES]
         + (y0_ref[pl.ds(j, tm, stride=yt), :] * g0 + y1_ref[pl.ds(j, tm, stride=yt), :] * g1) for j in range(yt)],
        axis=1)
    o_ref[...] = _rms(x, gfin_ref[...]) if final else x


def _combine(x, ys, gate, g_final, *, row0, n_tok, final):
    n = x.shape[0]
    tm = ROW_TILE
    yt = D_MODEL // LANES
    assert row0 % tm == 0 and n_tok % tm == 0
    row = pl.BlockSpec((tm, D_MODEL), lambda i: (i, 0))
    lanes = pl.BlockSpec((tm, LANES), lambda i: (i, 0))
    expert_rows = lambda k: pl.BlockSpec((tm * yt, LANES), lambda i: (i + (row0 + k * n_tok) // tm, 0))
    return pl.pallas_call(
        functools.partial(_combine_kernel, final=final), grid=(n // tm,),
        in_specs=[row, expert_rows(0), expert_rows(1), lanes, lanes, _const_spec((1, D_MODEL))],
        out_specs=row, out_shape=jax.ShapeDtypeStruct((n, D_MODEL), F32),
        compiler_params=_cparams("parallel"), name="moe_combine",
    )(x, ys, ys, *gate, g_final)


def _rope_tables(pos):
    half = ROPE // 2
    inv = ROPE_THETA ** (-jnp.arange(half, dtype=F32) / half)
    ang = pos.astype(F32)[:, None] * inv[None, :]
    zeros = jnp.zeros((pos.shape[0], LANES - ROPE), F32)
    cos = jnp.concatenate([jnp.cos(ang), jnp.cos(ang), zeros], axis=-1)
    sin = jnp.concatenate([-jnp.sin(ang), jnp.sin(ang), zeros], axis=-1)
    return cos, sin


def _prep_w_in(w_in):
    qa, ka, va, fa, cq, ckv, kr, ua, ug = jnp.split(
        w_in, np.cumsum([W_A, W_A, W_A, H_A, Q_RANK, KV_RANK, ROPE, C_C])[:].tolist(), axis=1)
    pad = jnp.zeros((D_MODEL, LANES - ROPE - H_A), w_in.dtype)
    return jnp.concatenate([qa * (D_HA ** -0.5), ka, va, cq, ckv, ua, ug, kr, fa, pad], axis=1).astype(BF16)


def _prep_w_uq(w_uq):
    w = w_uq.reshape(Q_RANK, H_B, NOPE + ROPE)
    w = jnp.pad(w, ((0, 0), (0, 0), (0, MLA_SLAB - NOPE - ROPE)))
    return w.reshape(Q_RANK, H_B * MLA_SLAB).astype(BF16)


def _prep_w_ukv(w_ukv):
    w = w_ukv.reshape(KV_RANK, H_B, NOPE + V_HD)
    return (w[:, :, :NOPE].reshape(KV_RANK, H_B * NOPE).astype(BF16),
            w[:, :, NOPE:].reshape(KV_RANK, H_B * V_HD).astype(BF16))


def _pad_lanes(x, width):
    return jnp.pad(x, ((0, 0),) * (x.ndim - 1) + ((0, width - x.shape[-1]),))


def _round_up(x, m):
    return -(-x // m) * m


def kernel(x_prompt, x_sample, cache_fox_k, cache_fox_v, cache_fox_logf, cache_mla_ckv, cache_mla_krope, state_conv, g_mix, w_in, b_f, g_q, w_uq, g_kv, w_ukv, w_dw, b_dw, ln_g, ln_b, g_out, w_out, g_ffn, w_rg, b_rg, w_re, b_re, w_gate, w_up, w_down, g_final):
    bp, tp, _ = x_prompt.shape
    bs, ts, _ = x_sample.shape
    n_past = cache_fox_k.shape[2]
    n_p, n_s = bp * tp, bs * ts
    depth = g_mix.shape[0]

    halo = CONV_W - 1
    tkp_s = _round_up(n_past + ts, LANES)

    x_p = x_prompt.reshape(n_p, D_MODEL)
    x_s = x_sample.reshape(n_s, D_MODEL)
    cos_p, sin_p = _rope_tables(jnp.arange(tp))
    cos_s, sin_s = _rope_tables(n_past + jnp.arange(ts))
    rope_p = (jnp.tile(cos_p, (bp, 1)), jnp.tile(sin_p, (bp, 1)))
    rope_s = (jnp.tile(cos_s, (bs, 1)), jnp.tile(sin_s, (bs, 1)))

    def split_heads(c):
        return c.reshape(c.shape[0], H_A // 2, 2, c.shape[-1])

    states_p, states_s = [], []
    for l in range(depth):
        w_uk, w_uv = _prep_w_ukv(w_ukv[l])
        b_f128 = jnp.pad(b_f[l], (_FA_LANE, LANES - _FA_LANE - H_A))[None, :]
        in_w = (g_mix[l][None], _prep_w_in(w_in[l]), g_q[l][None], g_kv[l][None], _prep_w_uq(w_uq[l]), b_f128)
        qa_p, ka_p, va_p, kab_p, vab_p, lf_p, qm_p, ckv_p, kr_p, u_p = _in_proj(x_p, *in_w, *rope_p)
        qa_s, ka_s, va_s, kab_s, vab_s, lf_s, qm_s, ckv_s, kr_s, u_s = _in_proj(x_s, *in_w, *rope_s)

        lf_p = lf_p.reshape(bp, tp, H_A)
        lf_s = lf_s.reshape(bs, ts, H_A)
        c_p = _cumsum_time(lf_p.transpose(0, 2, 1))
        lf_all = jnp.concatenate([cache_fox_logf[l].astype(F32), lf_s], axis=1)
        c_s = _cumsum_time(_pad_lanes(lf_all.transpose(0, 2, 1), tkp_s))
        o_a_p = _fox_attention(
            qa_p.reshape(bp, tp, W_A), kab_p.reshape(bp, tp, W_A), vab_p.reshape(bp, tp, W_A),
            split_heads(c_p).transpose(0, 1, 3, 2), split_heads(c_p), n_past=0, tq=TQ_PROMPT, tk=TK_PROMPT,
            pairs_per_step=FOX_PAIRS_PER_STEP)
        k_all = jnp.concatenate([cache_fox_k[l].reshape(bs, n_past, W_A).astype(BF16),
                                 kab_s.reshape(bs, ts, W_A)], axis=1)
        v_all = jnp.concatenate([cache_fox_v[l].reshape(bs, n_past, W_A).astype(BF16),
                                 vab_s.reshape(bs, ts, W_A)], axis=1)
        o_a_s = _fox_attention(
            qa_s.reshape(bs, ts, W_A), k_all, v_all,
            split_heads(c_s[:, :, n_past:n_past + ts]).transpose(0, 1, 3, 2), split_heads(c_s),
            n_past=n_past, tq=ts, tk=TK_SAMPLE, pairs_per_step=H_A // 2)

        ckv_s = ckv_s.reshape(bs, ts, KV_RANK)
        kr_s = kr_s.reshape(bs, ts, LANES)
        ckv_all = jnp.concatenate([cache_mla_ckv[l].astype(F32), ckv_s], axis=1)
        kr_all = jnp.concatenate([_pad_lanes(cache_mla_krope[l].astype(F32), LANES), kr_s], axis=1)
        kn_p, krb_p, vm_p = _kv_up(ckv_p, kr_p, w_uk, w_uv)
        kn_s, krb_s, vm_s = _kv_up(ckv_all.reshape(-1, KV_RANK), kr_all.reshape(-1, LANES), w_uk, w_uv)
        o_b_p = _mla_attention(
            qm_p.reshape(bp, tp, H_B * MLA_SLAB), kn_p.reshape(bp, tp, H_B * NOPE), krb_p.reshape(bp, tp, LANES),
            vm_p.reshape(bp, tp, W_B), n_past=0, tq=TQ_PROMPT, tk=TK_PROMPT, heads_per_step=MLA_HEADS_PER_STEP)
        o_b_s = _mla_attention(
            qm_s.reshape(bs, ts, H_B * MLA_SLAB), kn_s.reshape(bs, n_past + ts, H_B * NOPE),
            krb_s.reshape(bs, n_past + ts, LANES), vm_s.reshape(bs, n_past + ts, W_B),
            n_past=n_past, tq=ts, tk=TK_SAMPLE, heads_per_step=H_B)

        xp_p = jnp.pad(u_p.reshape(bp, tp, C_C), ((0, 0), (halo, _CONV_HALO - halo), (0, 0)))
        xp_s = jnp.concatenate([state_conv[l].astype(F32), u_s.reshape(bs, ts, C_C)], axis=1)
        conv_p = xp_p[:, tp:tp + halo]
        conv_s = xp_s[:, ts:ts + halo]
        xp_s = jnp.pad(xp_s, ((0, 0), (0, _CONV_HALO - halo), (0, 0)))
        conv_w = (w_dw[l], b_dw[l][None], ln_g[l][None], ln_b[l][None])
        o_c_p = _conv_module(xp_p, *conv_w, t=tp, tt=CONV_TILE)
        o_c_s = _conv_module(xp_s, *conv_w, t=ts, tt=ts)

        w_r = _pad_lanes(jnp.concatenate([w_rg[l], w_re[l]], axis=1), LANES).astype(BF16)
        b_r = _pad_lanes(jnp.concatenate([b_rg[l], b_re[l].reshape(-1)])[None, :].astype(F32), LANES)
        out_w = (g_out[l][None], w_out[l].astype(BF16), g_ffn[l][None], w_r, b_r)
        x_p, h2_p, rlog_p = _out_proj(o_a_p.reshape(n_p, W_A), o_b_p.reshape(n_p, W_B), o_c_p.reshape(n_p, C_C),
                                      x_p, *out_w)
        x_s, h2_s, rlog_s = _out_proj(o_a_s.reshape(n_s, W_A), o_b_s.reshape(n_s, W_B), o_c_s.reshape(n_s, C_C),
                                      x_s, *out_w)

        expert_ids, gate = _route(jnp.concatenate([rlog_p, rlog_s], axis=0))
        slot_src, slot_dst, block_e, n_valid = _dispatch_plan(expert_ids)
        h2_tiles = jnp.concatenate([h2_p, h2_s], axis=0).reshape(n_p + n_s, D_MODEL // 2 // LANES, LANES)
        ys = _expert_blocks(block_e, n_valid, slot_src, slot_dst, h2_tiles, w_gate, w_up, w_down, layer=l)
        final = l == depth - 1
        ys = ys.reshape(-1, LANES)
        gates = lambda rows: [jnp.broadcast_to(gate[rows, k:k + 1], (gate[rows].shape[0], LANES))
                              for k in range(TOP_K)]
        x_p = _combine(x_p, ys, gates(slice(0, n_p)), g_final[None], row0=0, n_tok=n_p + n_s, final=final)
        x_s = _combine(x_s, ys, gates(slice(n_p, None)), g_final[None], row0=n_p, n_tok=n_p + n_s, final=final)

        states_p.append((ka_p.reshape(bp, tp, H_A, D_HA), va_p.reshape(bp, tp, H_A, D_HA), lf_p,
                         ckv_p.reshape(bp, tp, KV_RANK), kr_p[:, :ROPE].reshape(bp, tp, ROPE), conv_p))
        states_s.append((ka_s.reshape(bs, ts, H_A, D_HA), va_s.reshape(bs, ts, H_A, D_HA), lf_s,
                         ckv_s, kr_s[:, :, :ROPE], conv_s))

    y_p = x_p.reshape(bp, tp, D_MODEL)
    y_s = x_s.reshape(bs, ts, D_MODEL)
    p_out = [jnp.stack(a) for a in zip(*states_p)]
    s_out = [jnp.stack(a) for a in zip(*states_s)]
    return (y_p, y_s, *p_out, *s_out)
```

```python
import functools

import numpy as np
import jax
import jax.numpy as jnp
from jax import lax
from jax.experimental import pallas as pl
from jax.experimental.pallas import tpu as pltpu

F32 = jnp.float32
BF16 = jnp.bfloat16

D_MODEL = 2048
DEPTH = 4
CHUNK = 64
H_A, D_HA = 8, 64
W_A = H_A * D_HA
H_B, NOPE, ROPE, V_HD = 8, 128, 64, 128
Q_RANK, KV_RANK = 512, 256
W_B = H_B * V_HD
C_C = D_MODEL - W_A - W_B
CONV_W = 31
_CHUNK_SHIFT = CHUNK.bit_length() - 1
assert 1 << _CHUNK_SHIFT == CHUNK
ROPE_THETA = 10000.0
N_GROUPS, EXPERTS_PER_GROUP = 4, 8
N_EXPERTS = N_GROUPS * EXPERTS_PER_GROUP
TOP_K = 2
D_EXPERT = 512
EPS = 1e-6
NEG_INF = -1e30
LOG2E = 1.4426950408889634

LANES = 128
SUBLANES = 8
TQ_PROMPT = 512
TK_PROMPT = 1024
TK_SAMPLE = 1024
MLA_HEADS_PER_STEP = 4
FOX_PAIRS_PER_STEP = 2
CONV_TILE = 256
MLA_SLAB = NOPE + LANES
ROW_TILE = 256
IN_PROJ_TILE = 512
EXPERT_TILE = 256
EXPERT_DUMP_ROWS = 2 * EXPERT_TILE
VMEM_LIMIT = 56 * 1024 * 1024

_C_QA, _C_KA, _C_VA, _C_CQ, _C_CKV, _C_UA, _C_UG, _C_KRF = 0, 512, 1024, 1536, 2048, 2304, 2816, 3328
IN_COLS = _C_KRF + LANES
_FA_LANE = ROPE


def _cparams(*sem):
    return pltpu.CompilerParams(dimension_semantics=sem, vmem_limit_bytes=VMEM_LIMIT)


def _const_spec(shape):
    nd = len(shape)
    return pl.BlockSpec(shape, lambda *_: (0,) * nd)


def _rms(x, g):
    return x * lax.rsqrt(jnp.mean(x * x, axis=-1, keepdims=True) + EPS) * g


def _rope_block(x, cos, sin_signed):
    lane = lax.broadcasted_iota(jnp.int32, x.shape, 1)
    partner = jnp.where(lane < ROPE // 2, pltpu.roll(x, LANES - ROPE // 2, 1), pltpu.roll(x, ROPE // 2, 1))
    return x * cos + partner * sin_signed


def _store_heads(o_ref, x):
    rows = x.shape[0]
    for hd in range(H_A):
        o_ref[pl.ds(hd, rows, stride=H_A), :] = x[:, hd * D_HA:(hd + 1) * D_HA]


def _in_proj_kernel(x_ref, gmix_ref, win_ref, gq_ref, gkv_ref, wuq_ref, bf_ref, cos_ref, sin_ref,
                    qa_ref, ka_ref, va_ref, kab_ref, vab_ref, logf_ref, qmla_ref, ckv_ref, kr_ref, u_ref):
    h = _rms(x_ref[...], gmix_ref[...]).astype(BF16)

    def proj(c0, width):
        return jnp.dot(h, win_ref[:, c0:c0 + width], preferred_element_type=F32)

    qa_ref[...] = (proj(_C_QA, W_A) * LOG2E).astype(BF16)
    ka = proj(_C_KA, W_A)
    _store_heads(ka_ref, ka)
    kab_ref[...] = ka.astype(BF16)
    va = proj(_C_VA, W_A)
    _store_heads(va_ref, va)
    vab_ref[...] = va.astype(BF16)

    cos = cos_ref[...]
    sin = sin_ref[...]
    krf = proj(_C_KRF, LANES)
    kr_ref[...] = _rope_block(krf, cos, sin)
    z = krf + bf_ref[...]
    logsig = jnp.minimum(z, 0.0) - jnp.log(1.0 + jnp.exp(-jnp.abs(z)))
    logf_ref[...] = logsig[:, _FA_LANE:_FA_LANE + H_A]

    ckv_ref[...] = _rms(proj(_C_CKV, KV_RANK), gkv_ref[...])

    u_ref[...] = proj(_C_UA, C_C) * jax.nn.sigmoid(proj(_C_UG, C_C))

    cqn = _rms(proj(_C_CQ, Q_RANK), gq_ref[...]).astype(BF16)
    scale = (NOPE + ROPE) ** -0.5 * LOG2E
    for hd in range(H_B):
        q = jnp.dot(cqn, wuq_ref[:, hd * MLA_SLAB:(hd + 1) * MLA_SLAB], preferred_element_type=F32)
        qmla_ref[:, hd * MLA_SLAB:hd * MLA_SLAB + NOPE] = (q[:, :NOPE] * scale).astype(BF16)
        qmla_ref[:, hd * MLA_SLAB + NOPE:(hd + 1) * MLA_SLAB] = (
            _rope_block(q[:, NOPE:], cos, sin) * scale).astype(BF16)


def _in_proj(x, g_mix, w_in_p, g_q, g_kv, w_uq_p, b_f128, cos, sin):
    n = x.shape[0]
    tm = min(IN_PROJ_TILE, n)
    assert n % tm == 0
    row = lambda w: pl.BlockSpec((tm, w), lambda i: (i, 0))
    heads = pl.BlockSpec((tm * H_A, D_HA), lambda i: (i, 0))
    out_shapes = (
        jax.ShapeDtypeStruct((n, W_A), BF16),
        jax.ShapeDtypeStruct((n * H_A, D_HA), F32),
        jax.ShapeDtypeStruct((n * H_A, D_HA), F32),
        jax.ShapeDtypeStruct((n, W_A), BF16),
        jax.ShapeDtypeStruct((n, W_A), BF16),
        jax.ShapeDtypeStruct((n, H_A), F32),
        jax.ShapeDtypeStruct((n, H_B * MLA_SLAB), BF16),
        jax.ShapeDtypeStruct((n, KV_RANK), F32),
        jax.ShapeDtypeStruct((n, LANES), F32),
        jax.ShapeDtypeStruct((n, C_C), F32),
    )
    return pl.pallas_call(
        _in_proj_kernel,
        grid=(n // tm,),
        in_specs=[row(D_MODEL), _const_spec((1, D_MODEL)), _const_spec((D_MODEL, IN_COLS)),
                  _const_spec((1, Q_RANK)), _const_spec((1, KV_RANK)),
                  _const_spec((Q_RANK, H_B * MLA_SLAB)), _const_spec((1, LANES)), row(LANES), row(LANES)],
        out_specs=[row(W_A), heads, heads, row(W_A), row(W_A), row(H_A), row(H_B * MLA_SLAB),
                   row(KV_RANK), row(LANES), row(C_C)],
        out_shape=out_shapes,
        compiler_params=_cparams("parallel"),
        name="in_proj",
    )(x, g_mix, w_in_p, g_q, g_kv, w_uq_p, b_f128, cos, sin)


def _cumsum_kernel(x_ref, o_ref):
    c = x_ref[...]
    t = c.shape[-1]
    lane = lax.broadcasted_iota(jnp.int32, c.shape, 1)
    s = 1
    while s < t:
        c = c + jnp.where(lane >= s, pltpu.roll(c, s, 1), 0.0)
        s *= 2
    o_ref[...] = c * LOG2E


def _cumsum_time(logf_t):
    b, hh, t = logf_t.shape
    spec = pl.BlockSpec((None, hh, t), lambda i: (i, 0, 0))
    return pl.pallas_call(
        _cumsum_kernel, grid=(b,), in_specs=[spec], out_specs=spec,
        out_shape=jax.ShapeDtypeStruct((b, hh, t), F32),
        compiler_params=_cparams("parallel"), name="cumsum_logf",
    )(logf_t)


def _softmax_step(s, v, row_bias, m_ref, l_ref, acc_ref):
    width = s.shape[1]
    s_max = jnp.max(s, axis=-1, keepdims=True)
    m_old = m_ref[...]
    m_new = jnp.maximum(m_old, s_max if row_bias is None else s_max + row_bias)
    alpha = jnp.exp2(m_old - m_new)
    offset = m_new if row_bias is None else m_new - row_bias
    if width % LANES == 0:
        p = [jnp.exp2(s[:, c * LANES:(c + 1) * LANES] - offset) for c in range(width // LANES)]
        l_add = functools.reduce(lambda a, b: a + b, p)
        p = p[0].astype(BF16) if len(p) == 1 else jnp.concatenate([x.astype(BF16) for x in p], axis=1)
    else:
        p = jnp.exp2(s - offset[:, :1])
        lane = lax.broadcasted_iota(jnp.int32, m_old.shape, 1)
        l_add = jnp.where(lane == 0, jnp.sum(p, axis=-1, keepdims=True), 0.0)
        p = p.astype(BF16)
    l_ref[...] = alpha * l_ref[...] + l_add
    acc_ref[...] = alpha * acc_ref[...] + jnp.dot(p, v, preferred_element_type=F32)
    m_ref[...] = m_new


def _flash_sweep(heads, n_open, d0, tq, tk, visible, split_diagonal, m_ref, l_ref, acc_ref):
    for h in range(len(heads)):
        m_ref[h] = jnp.full(m_ref.shape[1:], NEG_INF, F32)
        l_ref[h] = jnp.zeros(l_ref.shape[1:], F32)
        acc_ref[h] = jnp.zeros(acc_ref.shape[1:], F32)

    def chunk(k0, width, masked=False, r0=0, c0=0):
        nr = tq - r0
        rows = (slice(None),) if r0 == 0 else (pl.ds(r0, nr),)
        for h, (q, key_chunk, value_chunk, col_bias, row_bias) in enumerate(heads):
            s = _qk(q[r0:], key_chunk(k0, width))
            if col_bias is not None:
                s = s + col_bias(k0, width)
            if masked:
                r = r0 + lax.broadcasted_iota(jnp.int32, (nr, width), 0)
                c = c0 + lax.broadcasted_iota(jnp.int32, (nr, width), 1)
                s = jnp.where(visible(r, c), s, NEG_INF)
            _softmax_step(s, value_chunk(k0, width), None if row_bias is None else row_bias[r0:],
                          m_ref.at[(h, *rows)], l_ref.at[(h, *rows)], acc_ref.at[(h, *rows)])

    n_wide, n_narrow = n_open

    def wide_chunk(i, carry):
        chunk(pl.multiple_of(i * tk, tk), tk)
        return carry

    def narrow_chunk(i, carry):
        chunk(pl.multiple_of(n_wide * tk + i * tq, tq), tq)
        return carry

    lax.fori_loop(0, n_wide, wide_chunk, 0)
    if not (isinstance(n_narrow, int) and n_narrow == 0):
        lax.fori_loop(0, n_narrow, narrow_chunk, 0)
    if split_diagonal:
        half = tq // 2
        chunk(d0, half, masked=True)
        chunk(pl.multiple_of(d0 + half, half), half, masked=True, r0=half, c0=half)
    else:
        chunk(d0, tq, masked=True)
    return [acc_ref[h] / jnp.sum(l_ref[h], axis=-1, keepdims=True) for h in range(len(heads))]


def _split_diagonal(tq, n_past, single_tile):
    half = tq // 2
    return not single_tile and half % LANES == 0 and half % CHUNK == 0 and n_past % CHUNK == 0


def _qk(q, k):
    return lax.dot_general(q, k, (((1,), (1,)), ((), ())), preferred_element_type=F32)


def _tile_extent(n_past, tq, tk, single_tile):
    if single_tile:
        return (n_past // tk, (n_past % tk) // tq), n_past
    start = n_past + pl.program_id(2) * tq
    n_wide = start // tk
    n_narrow = 0 if tk == tq else (start - n_wide * tk) // tq
    return (n_wide, n_narrow), pl.multiple_of(start, tq)


def _fox_kernel(q_ref, k_ref, v_ref, cq_ref, ck_ref, o_ref, m_ref, l_ref, acc_ref, *, tq, tk, n_past, single_tile):
    lane = lax.broadcasted_iota(jnp.int32, (tq, LANES), 1)
    n_open, d0 = _tile_extent(n_past, tq, tk, single_tile)
    heads = []
    for pair in range(q_ref.shape[1] // LANES):
        lanes = slice(pair * LANES, (pair + 1) * LANES)
        q = q_ref[:, lanes]
        key_chunk = functools.partial(lambda k0, width, lanes: k_ref[pl.ds(k0, width), lanes], lanes=lanes)
        value_chunk = functools.partial(lambda k0, width, lanes: v_ref[pl.ds(k0, width), lanes], lanes=lanes)
        for j in range(2):
            qj = jnp.where((lane >= j * D_HA) & (lane < (j + 1) * D_HA), q, jnp.zeros_like(q))
            col_bias = functools.partial(
                lambda k0, width, pair, j: -ck_ref[pair, j:j + 1, pl.ds(k0, width)], pair=pair, j=j)
            row_bias = jnp.broadcast_to(cq_ref[pair, :, j:j + 1], (tq, LANES))
            heads.append((qj, key_chunk, value_chunk, col_bias, row_bias))
    outs = _flash_sweep(heads, n_open, d0, tq, tk, lambda r, c: c <= r, _split_diagonal(tq, n_past, single_tile),
                        m_ref, l_ref, acc_ref)
    for pair in range(q_ref.shape[1] // LANES):
        o_ref[:, pair * LANES:(pair + 1) * LANES] = jnp.where(lane < D_HA, outs[2 * pair], outs[2 * pair + 1])


def _fox_attention(q, k, v, c_q, c_k, *, n_past, tq, tk, pairs_per_step):
    b, t_q, _ = q.shape
    t_k = k.shape[1]
    t_kp = c_k.shape[-1]
    assert t_q % tq == 0 and tk % tq == 0 and n_past % tq == 0
    npair = pairs_per_step
    kern = functools.partial(_fox_kernel, tq=tq, tk=tk, n_past=n_past, single_tile=t_q == tq)
    return pl.pallas_call(
        kern,
        grid=(b, H_A // (2 * npair), t_q // tq),
        in_specs=[pl.BlockSpec((None, tq, npair * LANES), lambda bi, hp, qi: (bi, qi, hp)),
                  pl.BlockSpec((None, t_k, npair * LANES), lambda bi, hp, qi: (bi, 0, hp)),
                  pl.BlockSpec((None, t_k, npair * LANES), lambda bi, hp, qi: (bi, 0, hp)),
                  pl.BlockSpec((None, npair, tq, 2), lambda bi, hp, qi: (bi, hp, qi, 0)),
                  pl.BlockSpec((None, npair, 2, t_kp), lambda bi, hp, qi: (bi, hp, 0, 0))],
        out_specs=pl.BlockSpec((None, tq, npair * LANES), lambda bi, hp, qi: (bi, qi, hp)),
        out_shape=jax.ShapeDtypeStruct((b, t_q, W_A), F32),
        scratch_shapes=[pltpu.VMEM((2 * npair, tq, LANES), F32)] * 3,
        compiler_params=_cparams("parallel", "parallel", "arbitrary"),
        name="fox_attention",
    )(q, k, v, c_q, c_k)


def _mla_kernel(q_ref, kn_ref, kr_ref, v_ref, o_ref, m_ref, l_ref, acc_ref, *, tq, tk, n_past, single_tile):
    n_open, d0 = _tile_extent(n_past, tq, tk, single_tile)
    heads = []
    for j in range(q_ref.shape[1] // MLA_SLAB):
        key_chunk = functools.partial(
            lambda k0, width, j: jnp.concatenate(
                [kn_ref[pl.ds(k0, width), j * NOPE:(j + 1) * NOPE], kr_ref[pl.ds(k0, width), :]], axis=1), j=j)
        value_chunk = functools.partial(lambda k0, width, j: v_ref[pl.ds(k0, width), j * V_HD:(j + 1) * V_HD], j=j)
        heads.append((q_ref[:, j * MLA_SLAB:(j + 1) * MLA_SLAB], key_chunk, value_chunk, None, None))
    visible = lambda r, c: (d0 + c) >> _CHUNK_SHIFT <= (d0 + r) >> _CHUNK_SHIFT
    outs = _flash_sweep(heads, n_open, d0, tq, tk, visible, _split_diagonal(tq, n_past, single_tile),
                        m_ref, l_ref, acc_ref)
    for j in range(q_ref.shape[1] // MLA_SLAB):
        o_ref[:, j * V_HD:(j + 1) * V_HD] = outs[j]


def _mla_attention(q, k_nope, k_rope, v, *, n_past, tq, tk, heads_per_step):
    b, t_q, _ = q.shape
    t_k = k_nope.shape[1]
    assert t_q % tq == 0 and tk % tq == 0 and n_past % tq == 0
    nh = heads_per_step
    kern = functools.partial(_mla_kernel, tq=tq, tk=tk, n_past=n_past, single_tile=t_q == tq)
    return pl.pallas_call(
        kern,
        grid=(b, H_B // nh, t_q // tq),
        in_specs=[pl.BlockSpec((None, tq, nh * MLA_SLAB), lambda bi, hd, qi: (bi, qi, hd)),
                  pl.BlockSpec((None, t_k, nh * NOPE), lambda bi, hd, qi: (bi, 0, hd)),
                  pl.BlockSpec((None, t_k, LANES), lambda bi, hd, qi: (bi, 0, 0)),
                  pl.BlockSpec((None, t_k, nh * V_HD), lambda bi, hd, qi: (bi, 0, hd))],
        out_specs=pl.BlockSpec((None, tq, nh * V_HD), lambda bi, hd, qi: (bi, qi, hd)),
        out_shape=jax.ShapeDtypeStruct((b, t_q, W_B), F32),
        scratch_shapes=[pltpu.VMEM((nh, tq, LANES), F32), pltpu.VMEM((nh, tq, LANES), F32),
                        pltpu.VMEM((nh, tq, V_HD), F32)],
        compiler_params=_cparams("parallel", "parallel", "arbitrary"),
        name="mla_attention",
    )(q, k_nope, k_rope, v)


def _kv_up_kernel(ckv_ref, kr_ref, wuk_ref, wuv_ref, kn_ref, krb_ref, v_ref):
    c = ckv_ref[...].astype(BF16)
    krb_ref[...] = kr_ref[...].astype(BF16)
    kn_ref[...] = jnp.dot(c, wuk_ref[...], preferred_element_type=F32).astype(BF16)
    v_ref[...] = jnp.dot(c, wuv_ref[...], preferred_element_type=F32).astype(BF16)


def _kv_up(ckv_n, krope128, w_uk, w_uv):
    r = ckv_n.shape[0]
    tm = next(t for t in (4 * ROW_TILE, 2 * ROW_TILE, ROW_TILE) if r % t == 0)
    row = lambda w: pl.BlockSpec((tm, w), lambda i: (i, 0))
    return pl.pallas_call(
        _kv_up_kernel, grid=(r // tm,),
        in_specs=[row(KV_RANK), row(LANES), _const_spec((KV_RANK, H_B * NOPE)),
                  _const_spec((KV_RANK, H_B * V_HD))],
        out_specs=[row(H_B * NOPE), row(LANES), row(H_B * V_HD)],
        out_shape=(jax.ShapeDtypeStruct((r, H_B * NOPE), BF16), jax.ShapeDtypeStruct((r, LANES), BF16),
                   jax.ShapeDtypeStruct((r, H_B * V_HD), BF16)),
        compiler_params=_cparams("parallel"), name="mla_kv_up",
    )(ckv_n, krope128, w_uk, w_uv)


_CONV_SUB = 64
_CONV_HALO = -(-CONV_W // SUBLANES) * SUBLANES


def _conv_kernel(xp_ref, w_ref, b_ref, g_ref, beta_ref, o_ref, *, tt):
    t0 = pl.program_id(1) * tt
    w = w_ref[...]
    rows = min(_CONV_SUB, tt)
    for sub in range(tt // rows):
        base = pl.multiple_of(t0 + sub * rows, rows)
        xa = xp_ref[pl.ds(base, rows + _CONV_HALO), :]
        acc = None
        for b in range(SUBLANES):
            z = None
            for a in range(-(-CONV_W // SUBLANES)):
                tap = SUBLANES * a + b
                if tap < CONV_W:
                    term = xa[SUBLANES * a:SUBLANES * a + rows + SUBLANES, :] * w[tap:tap + 1, :]
                    z = term if z is None else z + term
            acc = z[b:b + rows, :] if acc is None else acc + z[b:b + rows, :]
        y = acc + b_ref[...]
        mu = jnp.mean(y, axis=-1, keepdims=True)
        yc = y - mu
        var = jnp.mean(yc * yc, axis=-1, keepdims=True)
        y = yc * lax.rsqrt(var + EPS) * g_ref[...] + beta_ref[...]
        o_ref[sub * rows:(sub + 1) * rows, :] = y * jax.nn.sigmoid(y)


def _conv_module(xp, w_dw, b_dw, ln_g, ln_b, *, t, tt):
    b, t_p, _ = xp.shape
    kern = functools.partial(_conv_kernel, tt=tt)
    return pl.pallas_call(
        kern, grid=(b, t // tt),
        in_specs=[pl.BlockSpec((None, t_p, C_C), lambda bi, ti: (bi, 0, 0)),
                  _const_spec((CONV_W, C_C)), _const_spec((1, C_C)), _const_spec((1, C_C)), _const_spec((1, C_C))],
        out_specs=pl.BlockSpec((None, tt, C_C), lambda bi, ti: (bi, ti, 0)),
        out_shape=jax.ShapeDtypeStruct((b, t, C_C), F32),
        compiler_params=_cparams("parallel", "arbitrary"), name="conv_module",
    )(xp, w_dw, b_dw, ln_g, ln_b)


def _out_proj_kernel(oa_ref, ob_ref, oc_ref, x_ref, gout_ref, wout_ref, gffn_ref, wr_ref, br_ref,
                     xn_ref, h2_ref, rl_ref):
    g = gout_ref[...]
    na = _rms(oa_ref[...], g[:, :W_A]).astype(BF16)
    nb = _rms(ob_ref[...], g[:, W_A:W_A + W_B]).astype(BF16)
    nc = _rms(oc_ref[...], g[:, W_A + W_B:]).astype(BF16)
    mix = jnp.dot(na, wout_ref[:W_A, :], preferred_element_type=F32)
    mix = mix + jnp.dot(nb, wout_ref[W_A:W_A + W_B, :], preferred_element_type=F32)
    mix = mix + jnp.dot(nc, wout_ref[W_A + W_B:, :], preferred_element_type=F32)
    xn = x_ref[...] + mix
    xn_ref[...] = xn
    h2 = _rms(xn, gffn_ref[...]).astype(BF16)
    rl_ref[...] = jnp.dot(h2, wr_ref[...], preferred_element_type=F32) + br_ref[...]
    bits = pltpu.bitcast(h2.astype(F32), jnp.uint32)
    packed = (bits[:, :D_MODEL // 2] >> 16) | (bits[:, D_MODEL // 2:] & jnp.uint32(0xFFFF0000))
    tiles = D_MODEL // 2 // LANES
    for j in range(tiles):
        h2_ref[pl.ds(j, packed.shape[0], stride=tiles), :] = packed[:, j * LANES:(j + 1) * LANES]


def _out_proj(o_a, o_b, o_c, x, g_out, w_out, g_ffn, w_r, b_r):
    n = x.shape[0]
    tm = ROW_TILE
    row = lambda w: pl.BlockSpec((tm, w), lambda i: (i, 0))
    return pl.pallas_call(
        _out_proj_kernel, grid=(n // tm,),
        in_specs=[row(W_A), row(W_B), row(C_C), row(D_MODEL), _const_spec((1, D_MODEL)),
                  _const_spec((D_MODEL, D_MODEL)), _const_spec((1, D_MODEL)),
                  _const_spec((D_MODEL, LANES)), _const_spec((1, LANES))],
        out_specs=[row(D_MODEL), pl.BlockSpec((tm * (D_MODEL // 2 // LANES), LANES), lambda i: (i, 0)), row(LANES)],
        out_shape=(jax.ShapeDtypeStruct((n, D_MODEL), F32),
                   jax.ShapeDtypeStruct((n * (D_MODEL // 2 // LANES), LANES), jnp.uint32),
                   jax.ShapeDtypeStruct((n, LANES), F32)),
        compiler_params=_cparams("parallel"), name="out_proj_router",
    )(o_a, o_b, o_c, x, g_out, w_out, g_ffn, w_r, b_r)


def _expert_kernel(be_ref, nv_ref, src_ref, dst_ref, h_hbm, wg_ref, wu_ref, wd_ref, y_hbm,
                   xbuf, ybuf, zbuf, wgb_ref, wub_ref, wdb_ref, gsem, ssem, zsem, *, n_blocks, dump_row0):
    te = EXPERT_TILE
    xt = D_MODEL // 2 // LANES
    yt = D_MODEL // LANES
    i = pl.program_id(0)
    nv = nv_ref[0]
    cur = i % 2
    nxt = 1 - cur
    valid = i < nv

    def gather_copy(tok, buf, r):
        return pltpu.make_async_copy(h_hbm.at[tok], xbuf.at[buf, pl.ds(r * (xt + 1), xt), :], gsem.at[buf])

    def scatter_copy(row, buf, r):
        return pltpu.make_async_copy(ybuf.at[buf, pl.ds(r * (yt + 1), yt), :], y_hbm.at[row], ssem.at[buf])

    def gather_block(block, buf):
        for r in range(te):
            gather_copy(src_ref[block * te + r], buf, r).start(priority=1)

    def scatter_block(block, buf):
        for r in range(te):
            scatter_copy(dst_ref[block * te + r], buf, r).start(priority=r % 2)

    def wait_gather(buf):
        for r in range(te):
            gather_copy(0, buf, 0).wait()

    def wait_scatter(buf):
        for r in range(te):
            scatter_copy(0, buf, 0).wait()

    @pl.when(i == 0)
    def _():
        gather_block(0, 0)
        zbuf[...] = jnp.zeros(zbuf.shape, F32)
        fills = [pltpu.make_async_copy(zbuf, y_hbm.at[pl.ds(dump_row0 + k * SUBLANES, SUBLANES)], zsem)
                 for k in range(EXPERT_DUMP_ROWS // SUBLANES)]
        for fill in fills:
            fill.start()
        for fill in fills:
            fill.wait()

    prev_e = be_ref[jnp.maximum(i - 1, 0)]

    @pl.when(valid & ((i == 0) | (be_ref[i] != prev_e)))
    def _():
        wgb_ref[...] = wg_ref[...].astype(BF16)
        wub_ref[...] = wu_ref[...].astype(BF16)
        wdb_ref[...] = wd_ref[...].astype(BF16)

    @pl.when(valid)
    def _():
        wait_gather(cur)

    @pl.when(valid & (i >= 2))
    def _():
        wait_scatter(cur)

    def step(scatter_previous):
        gather_block(jnp.minimum(i + 1, n_blocks - 1), nxt)
        if scatter_previous:
            scatter_block(i - 1, nxt)
        u = jnp.concatenate([xbuf[cur, pl.ds(j, te, stride=xt + 1), :] for j in range(xt)], axis=1)
        x_lo = pltpu.bitcast(u << 16, F32).astype(BF16)
        x_hi = pltpu.bitcast(u & jnp.uint32(0xFFFF0000), F32).astype(BF16)
        half = D_MODEL // 2
        gate = (jnp.dot(x_lo, wgb_ref[:half, :], preferred_element_type=F32)
                + jnp.dot(x_hi, wgb_ref[half:, :], preferred_element_type=F32))
        up = (jnp.dot(x_lo, wub_ref[:half, :], preferred_element_type=F32)
              + jnp.dot(x_hi, wub_ref[half:, :], preferred_element_type=F32))
        hid = (gate * jax.nn.sigmoid(gate) * up).astype(BF16)
        y = jnp.dot(hid, wdb_ref[...], preferred_element_type=F32)
        for j in range(yt):
            ybuf[cur, pl.ds(j, te, stride=yt + 1), :] = y[:, j * LANES:(j + 1) * LANES]

    @pl.when(valid & (i == 0))
    def _():
        step(False)

    @pl.when(valid & (i > 0))
    def _():
        step(True)

    @pl.when(i == nv - 1)
    def _():
        scatter_block(i, cur)
        wait_scatter(cur)
        wait_gather(nxt)

    @pl.when((i == nv - 1) & (i > 0))
    def _():
        wait_scatter(nxt)


def _expert_blocks(block_e, n_valid, slot_src, slot_dst, h2_tiles, w_gate, w_up, w_down, *, layer):
    n_tok = h2_tiles.shape[0]
    te = EXPERT_TILE
    n_blocks = block_e.shape[0]
    xt, yt = D_MODEL // 2 // LANES, D_MODEL // LANES
    weights = lambda r, c: pl.BlockSpec((None, None, r, c), lambda i, be, nv, src, dst: (layer, be[i], 0, 0))
    grid_spec = pltpu.PrefetchScalarGridSpec(
        num_scalar_prefetch=4, grid=(n_blocks,),
        in_specs=[pl.BlockSpec(memory_space=pl.ANY), weights(D_MODEL, D_EXPERT), weights(D_MODEL, D_EXPERT),
                  weights(D_EXPERT, D_MODEL)],
        out_specs=pl.BlockSpec(memory_space=pl.ANY),
        scratch_shapes=[pltpu.VMEM((2, te * (xt + 1), LANES), jnp.uint32),
                        pltpu.VMEM((2, te * (yt + 1), LANES), F32),
                        pltpu.VMEM((SUBLANES, yt, LANES), F32),
                        pltpu.VMEM((D_MODEL, D_EXPERT), BF16), pltpu.VMEM((D_MODEL, D_EXPERT), BF16),
                        pltpu.VMEM((D_EXPERT, D_MODEL), BF16),
                        pltpu.SemaphoreType.DMA((2,)), pltpu.SemaphoreType.DMA((2,)), pltpu.SemaphoreType.DMA(())])
    kern = functools.partial(_expert_kernel, n_blocks=n_blocks, dump_row0=TOP_K * n_tok)
    return pl.pallas_call(
        kern, grid_spec=grid_spec,
        out_shape=jax.ShapeDtypeStruct((TOP_K * n_tok + EXPERT_DUMP_ROWS, yt, LANES), F32),
        compiler_params=_cparams("arbitrary"), name="expert_blocks",
    )(block_e, n_valid, slot_src, slot_dst, h2_tiles, w_gate, w_up, w_down)


def _route(rlog):
    n = rlog.shape[0]
    idx = jnp.arange(n)
    g_logits = rlog[:, :N_GROUPS]
    g_sel = jnp.argmax(g_logits, axis=-1)
    g_w = jax.nn.softmax(g_logits, axis=-1)[idx, g_sel]
    e_logits = rlog[:, N_GROUPS:N_GROUPS + N_EXPERTS].reshape(n, N_GROUPS, EXPERTS_PER_GROUP)[idx, g_sel]
    top_v, top_i = lax.top_k(e_logits, TOP_K)
    gate = jax.nn.softmax(top_v, axis=-1) * g_w[:, None]
    expert_ids = (g_sel[:, None] * EXPERTS_PER_GROUP + top_i).astype(jnp.int32)
    return expert_ids, gate


def _dispatch_plan(expert_ids):
    n = expert_ids.shape[0]
    te = EXPERT_TILE
    a = n * TOP_K
    n_blocks = -(-a // te) + N_EXPERTS
    rows = n_blocks * te
    flat_e = expert_ids.reshape(-1)
    experts = jnp.arange(N_EXPERTS, dtype=jnp.int32)
    counts = jnp.sum((flat_e[:, None] == experts[None, :]).astype(jnp.int32), axis=0)
    padded = (counts + te - 1) // te * te
    pad_end = jnp.cumsum(padded)
    pad_start = pad_end - padded
    start = jnp.cumsum(counts) - counts
    order = jnp.argsort(flat_e).astype(jnp.int32)
    slot = jnp.arange(rows, dtype=jnp.int32)
    slot_e = jnp.minimum(jnp.sum((pad_end[None, :] <= slot[:, None]).astype(jnp.int32), axis=-1), N_EXPERTS - 1)
    slot_hot = (slot_e[:, None] == experts[None, :]).astype(jnp.int32)
    slot_rank = slot - jnp.sum(slot_hot * pad_start[None, :], axis=-1)
    filled = slot_rank < jnp.sum(slot_hot * counts[None, :], axis=-1)
    src = jnp.where(filled, jnp.sum(slot_hot * start[None, :], axis=-1) + slot_rank, 0)
    slot_a = order[src]
    slot_src = jnp.where(filled, slot_a // TOP_K, 0)
    slot_dst = jnp.where(filled, (slot_a % TOP_K) * n + slot_a // TOP_K, a + slot % EXPERT_DUMP_ROWS)
    block_e = slot_e.reshape(n_blocks, te)[:, 0]
    n_valid = (pad_end[-1] // te).astype(jnp.int32).reshape(1)
    return slot_src, slot_dst, block_e, n_valid


def _combine_kernel(x_ref, y0_ref, y1_ref, g0_ref, g1_ref, gfin_ref, o_ref, *, final):
    tm = x_ref.shape[0]
    yt = D_MODEL // LANES
    g0 = g0_ref[...]
    g1 = g1_ref[...]
    x = jnp.concatenate(
        [x_ref[:, j * LANES:(j + 1) * LANES]
         + (y0_ref[pl.ds(j, tm, stride=yt), :] * g0 + y1_ref[pl.ds(j, tm, stride=yt), :] * g1) for j in range(yt)],
        axis=1)
    o_ref[...] = _rms(x, gfin_ref[...]) if final else x


def _combine(x, ys, gate, g_final, *, row0, n_tok, final):
    n = x.shape[0]
    tm = ROW_TILE
    yt = D_MODEL // LANES
    assert row0 % tm == 0 and n_tok % tm == 0
    row = pl.BlockSpec((tm, D_MODEL), lambda i: (i, 0))
    lanes = pl.BlockSpec((tm, LANES), lambda i: (i, 0))
    expert_rows = lambda k: pl.BlockSpec((tm * yt, LANES), lambda i: (i + (row0 + k * n_tok) // tm, 0))
    return pl.pallas_call(
        functools.partial(_combine_kernel, final=final), grid=(n // tm,),
        in_specs=[row, expert_rows(0), expert_rows(1), lanes, lanes, _const_spec((1, D_MODEL))],
        out_specs=row, out_shape=jax.ShapeDtypeStruct((n, D_MODEL), F32),
        compiler_params=_cparams("parallel"), name="moe_combine",
    )(x, ys, ys, *gate, g_final)


def _rope_tables(pos):
    half = ROPE // 2
    inv = ROPE_THETA ** (-jnp.arange(half, dtype=F32) / half)
    ang = pos.astype(F32)[:, None] * inv[None, :]
    zeros = jnp.zeros((pos.shape[0], LANES - ROPE), F32)
    cos = jnp.concatenate([jnp.cos(ang), jnp.cos(ang), zeros], axis=-1)
    sin = jnp.concatenate([-jnp.sin(ang), jnp.sin(ang), zeros], axis=-1)
    return cos, sin


def _prep_w_in(w_in):
    qa, ka, va, fa, cq, ckv, kr, ua, ug = jnp.split(
        w_in, np.cumsum([W_A, W_A, W_A, H_A, Q_RANK, KV_RANK, ROPE, C_C])[:].tolist(), axis=1)
    pad = jnp.zeros((D_MODEL, LANES - ROPE - H_A), w_in.dtype)
    return jnp.concatenate([qa * (D_HA ** -0.5), ka, va, cq, ckv, ua, ug, kr, fa, pad], axis=1).astype(BF16)


def _prep_w_uq(w_uq):
    w = w_uq.reshape(Q_RANK, H_B, NOPE + ROPE)
    w = jnp.pad(w, ((0, 0), (0, 0), (0, MLA_SLAB - NOPE - ROPE)))
    return w.reshape(Q_RANK, H_B * MLA_SLAB).astype(BF16)


def _prep_w_ukv(w_ukv):
    w = w_ukv.reshape(KV_RANK, H_B, NOPE + V_HD)
    return (w[:, :, :NOPE].reshape(KV_RANK, H_B * NOPE).astype(BF16),
            w[:, :, NOPE:].reshape(KV_RANK, H_B * V_HD).astype(BF16))


def _pad_lanes(x, width):
    return jnp.pad(x, ((0, 0),) * (x.ndim - 1) + ((0, width - x.shape[-1]),))


def _round_up(x, m):
    return -(-x // m) * m


def kernel(x_prompt, x_sample, cache_fox_k, cache_fox_v, cache_fox_logf, cache_mla_ckv, cache_mla_krope, state_conv, g_mix, w_in, b_f, g_q, w_uq, g_kv, w_ukv, w_dw, b_dw, ln_g, ln_b, g_out, w_out, g_ffn, w_rg, b_rg, w_re, b_re, w_gate, w_up, w_down, g_final):
    bp, tp, _ = x_prompt.shape
    bs, ts, _ = x_sample.shape
    n_past = cache_fox_k.shape[2]
    n_p, n_s = bp * tp, bs * ts
    depth = g_mix.shape[0]

    halo = CONV_W - 1
    tkp_s = _round_up(n_past + ts, LANES)

    x_p = x_prompt.reshape(n_p, D_MODEL)
    x_s = x_sample.reshape(n_s, D_MODEL)
    cos_p, sin_p = _rope_tables(jnp.arange(tp))
    cos_s, sin_s = _rope_tables(n_past + jnp.arange(ts))
    rope_p = (jnp.tile(cos_p, (bp, 1)), jnp.tile(sin_p, (bp, 1)))
    rope_s = (jnp.tile(cos_s, (bs, 1)), jnp.tile(sin_s, (bs, 1)))

    def split_heads(c):
        return c.reshape(c.shape[0], H_A // 2, 2, c.shape[-1])

    states_p, states_s = [], []
    for l in range(depth):
        w_uk, w_uv = _prep_w_ukv(w_ukv[l])
        b_f128 = jnp.pad(b_f[l], (_FA_LANE, LANES - _FA_LANE - H_A))[None, :]
        in_w = (g_mix[l][None], _prep_w_in(w_in[l]), g_q[l][None], g_kv[l][None], _prep_w_uq(w_uq[l]), b_f128)
        qa_p, ka_p, va_p, kab_p, vab_p, lf_p, qm_p, ckv_p, kr_p, u_p = _in_proj(x_p, *in_w, *rope_p)
        qa_s, ka_s, va_s, kab_s, vab_s, lf_s, qm_s, ckv_s, kr_s, u_s = _in_proj(x_s, *in_w, *rope_s)

        lf_p = lf_p.reshape(bp, tp, H_A)
        lf_s = lf_s.reshape(bs, ts, H_A)
        c_p = _cumsum_time(lf_p.transpose(0, 2, 1))
        lf_all = jnp.concatenate([cache_fox_logf[l].astype(F32), lf_s], axis=1)
        c_s = _cumsum_time(_pad_lanes(lf_all.transpose(0, 2, 1), tkp_s))
        o_a_p = _fox_attention(
            qa_p.reshape(bp, tp, W_A), kab_p.reshape(bp, tp, W_A), vab_p.reshape(bp, tp, W_A),
            split_heads(c_p).transpose(0, 1, 3, 2), split_heads(c_p), n_past=0, tq=TQ_PROMPT, tk=TK_PROMPT,
            pairs_per_step=FOX_PAIRS_PER_STEP)
        k_all = jnp.concatenate([cache_fox_k[l].reshape(bs, n_past, W_A).astype(BF16),
                                 kab_s.reshape(bs, ts, W_A)], axis=1)
        v_all = jnp.concatenate([cache_fox_v[l].reshape(bs, n_past, W_A).astype(BF16),
                                 vab_s.reshape(bs, ts, W_A)], axis=1)
        o_a_s = _fox_attention(
            qa_s.reshape(bs, ts, W_A), k_all, v_all,
            split_heads(c_s[:, :, n_past:n_past + ts]).transpose(0, 1, 3, 2), split_heads(c_s),
            n_past=n_past, tq=ts, tk=TK_SAMPLE, pairs_per_step=H_A // 2)

        ckv_s = ckv_s.reshape(bs, ts, KV_RANK)
        kr_s = kr_s.reshape(bs, ts, LANES)
        ckv_all = jnp.concatenate([cache_mla_ckv[l].astype(F32), ckv_s], axis=1)
        kr_all = jnp.concatenate([_pad_lanes(cache_mla_krope[l].astype(F32), LANES), kr_s], axis=1)
        kn_p, krb_p, vm_p = _kv_up(ckv_p, kr_p, w_uk, w_uv)
        kn_s, krb_s, vm_s = _kv_up(ckv_all.reshape(-1, KV_RANK), kr_all.reshape(-1, LANES), w_uk, w_uv)
        o_b_p = _mla_attention(
            qm_p.reshape(bp, tp, H_B * MLA_SLAB), kn_p.reshape(bp, tp, H_B * NOPE), krb_p.reshape(bp, tp, LANES),
            vm_p.reshape(bp, tp, W_B), n_past=0, tq=TQ_PROMPT, tk=TK_PROMPT, heads_per_step=MLA_HEADS_PER_STEP)
        o_b_s = _mla_attention(
            qm_s.reshape(bs, ts, H_B * MLA_SLAB), kn_s.reshape(bs, n_past + ts, H_B * NOPE),
            krb_s.reshape(bs, n_past + ts, LANES), vm_s.reshape(bs, n_past + ts, W_B),
            n_past=n_past, tq=ts, tk=TK_SAMPLE, heads_per_step=H_B)

        xp_p = jnp.pad(u_p.reshape(bp, tp, C_C), ((0, 0), (halo, _CONV_HALO - halo), (0, 0)))
        xp_s = jnp.concatenate([state_conv[l].astype(F32), u_s.reshape(bs, ts, C_C)], axis=1)
        conv_p = xp_p[:, tp:tp + halo]
        conv_s = xp_s[:, ts:ts + halo]
        xp_s = jnp.pad(xp_s, ((0, 0), (0, _CONV_HALO - halo), (0, 0)))
        conv_w = (w_dw[l], b_dw[l][None], ln_g[l][None], ln_b[l][None])
        o_c_p = _conv_module(xp_p, *conv_w, t=tp, tt=CONV_TILE)
        o_c_s = _conv_module(xp_s, *conv_w, t=ts, tt=ts)

        w_r = _pad_lanes(jnp.concatenate([w_rg[l], w_re[l]], axis=1), LANES).astype(BF16)
        b_r = _pad_lanes(jnp.concatenate([b_rg[l], b_re[l].reshape(-1)])[None, :].astype(F32), LANES)
        out_w = (g_out[l][None], w_out[l].astype(BF16), g_ffn[l][None], w_r, b_r)
        x_p, h2_p, rlog_p = _out_proj(o_a_p.reshape(n_p, W_A), o_b_p.reshape(n_p, W_B), o_c_p.reshape(n_p, C_C),
                                      x_p, *out_w)
        x_s, h2_s, rlog_s = _out_proj(o_a_s.reshape(n_s, W_A), o_b_s.reshape(n_s, W_B), o_c_s.reshape(n_s, C_C),
                                      x_s, *out_w)

        expert_ids, gate = _route(jnp.concatenate([rlog_p, rlog_s], axis=0))
        slot_src, slot_dst, block_e, n_valid = _dispatch_plan(expert_ids)
        h2_tiles = jnp.concatenate([h2_p, h2_s], axis=0).reshape(n_p + n_s, D_MODEL // 2 // LANES, LANES)
        ys = _expert_blocks(block_e, n_valid, slot_src, slot_dst, h2_tiles, w_gate, w_up, w_down, layer=l)
        final = l == depth - 1
        ys = ys.reshape(-1, LANES)
        gates = lambda rows: [jnp.broadcast_to(gate[rows, k:k + 1], (gate[rows].shape[0], LANES))
                              for k in range(TOP_K)]
        x_p = _combine(x_p, ys, gates(slice(0, n_p)), g_final[None], row0=0, n_tok=n_p + n_s, final=final)
        x_s = _combine(x_s, ys, gates(slice(n_p, None)), g_final[None], row0=n_p, n_tok=n_p + n_s, final=final)

        states_p.append((ka_p.reshape(bp, tp, H_A, D_HA), va_p.reshape(bp, tp, H_A, D_HA), lf_p,
                         ckv_p.reshape(bp, tp, KV_RANK), kr_p[:, :ROPE].reshape(bp, tp, ROPE), conv_p))
        states_s.append((ka_s.reshape(bs, ts, H_A, D_HA), va_s.reshape(bs, ts, H_A, D_HA), lf_s,
                         ckv_s, kr_s[:, :, :ROPE], conv_s))

    y_p = x_p.reshape(bp, tp, D_MODEL)
    y_s = x_s.reshape(bs, ts, D_MODEL)
    p_out = [jnp.stack(a) for a in zip(*states_p)]
    s_out = [jnp.stack(a) for a in zip(*states_s)]
    return (y_p, y_s, *p_out, *s_out)
```

```python
import functools

import numpy as np
import jax
import jax.numpy as jnp
from jax import lax
from jax.experimental import pallas as pl
from jax.experimental.pallas import tpu as pltpu

F32 = jnp.float32
BF16 = jnp.bfloat16

D_MODEL = 2048
DEPTH = 4
CHUNK = 64
H_A, D_HA = 8, 64
W_A = H_A * D_HA
H_B, NOPE, ROPE, V_HD = 8, 128, 64, 128
Q_RANK, KV_RANK = 512, 256
W_B = H_B * V_HD
C_C = D_MODEL - W_A - W_B
CONV_W = 31
_CHUNK_SHIFT = CHUNK.bit_length() - 1
assert 1 << _CHUNK_SHIFT == CHUNK
ROPE_THETA = 10000.0
N_GROUPS, EXPERTS_PER_GROUP = 4, 8
N_EXPERTS = N_GROUPS * EXPERTS_PER_GROUP
TOP_K = 2
D_EXPERT = 512
EPS = 1e-6
NEG_INF = -1e30
LOG2E = 1.4426950408889634

LANES = 128
SUBLANES = 8
TQ_PROMPT = 512
TK_PROMPT = 1024
TK_SAMPLE = 1024
MLA_HEADS_PER_STEP = 4
FOX_PAIRS_PER_STEP = 2
CONV_TILE = 512
MLA_SLAB = NOPE + LANES
ROW_TILE = 256
IN_PROJ_TILE = 512
EXPERT_TILE = 256
EXPERT_DUMP_ROWS = 2 * EXPERT_TILE
VMEM_LIMIT = 56 * 1024 * 1024

_C_QA, _C_KA, _C_VA, _C_CQ, _C_CKV, _C_UA, _C_UG, _C_KRF = 0, 512, 1024, 1536, 2048, 2304, 2816, 3328
IN_COLS = _C_KRF + LANES
_FA_LANE = ROPE


def _cparams(*sem):
    return pltpu.CompilerParams(dimension_semantics=sem, vmem_limit_bytes=VMEM_LIMIT)


def _const_spec(shape):
    nd = len(shape)
    return pl.BlockSpec(shape, lambda *_: (0,) * nd)


def _rms(x, g):
    return x * lax.rsqrt(jnp.mean(x * x, axis=-1, keepdims=True) + EPS) * g


def _rope_block(x, cos, sin_signed):
    lane = lax.broadcasted_iota(jnp.int32, x.shape, 1)
    partner = jnp.where(lane < ROPE // 2, pltpu.roll(x, LANES - ROPE // 2, 1), pltpu.roll(x, ROPE // 2, 1))
    return x * cos + partner * sin_signed


def _store_heads(o_ref, x):
    rows = x.shape[0]
    for hd in range(H_A):
        o_ref[pl.ds(hd, rows, stride=H_A), :] = x[:, hd * D_HA:(hd + 1) * D_HA]


def _in_proj_kernel(x_ref, gmix_ref, win_ref, gq_ref, gkv_ref, wuq_ref, bf_ref, cos_ref, sin_ref,
                    qa_ref, ka_ref, va_ref, kab_ref, vab_ref, logf_ref, qmla_ref, ckv_ref, kr_ref, u_ref):
    h = _rms(x_ref[...], gmix_ref[...]).astype(BF16)

    def proj(c0, width):
        return jnp.dot(h, win_ref[:, c0:c0 + width], preferred_element_type=F32)

    qa_ref[...] = (proj(_C_QA, W_A) * LOG2E).astype(BF16)
    ka = proj(_C_KA, W_A)
    _store_heads(ka_ref, ka)
    kab_ref[...] = ka.astype(BF16)
    va = proj(_C_VA, W_A)
    _store_heads(va_ref, va)
    vab_ref[...] = va.astype(BF16)

    cos = cos_ref[...]
    sin = sin_ref[...]
    krf = proj(_C_KRF, LANES)
    kr_ref[...] = _rope_block(krf, cos, sin)
    z = krf + bf_ref[...]
    logsig = jnp.minimum(z, 0.0) - jnp.log(1.0 + jnp.exp(-jnp.abs(z)))
    logf_ref[...] = logsig[:, _FA_LANE:_FA_LANE + H_A]

    ckv_ref[...] = _rms(proj(_C_CKV, KV_RANK), gkv_ref[...])

    u_ref[...] = proj(_C_UA, C_C) * jax.nn.sigmoid(proj(_C_UG, C_C))

    cqn = _rms(proj(_C_CQ, Q_RANK), gq_ref[...]).astype(BF16)
    scale = (NOPE + ROPE) ** -0.5 * LOG2E
    for hd in range(H_B):
        q = jnp.dot(cqn, wuq_ref[:, hd * MLA_SLAB:(hd + 1) * MLA_SLAB], preferred_element_type=F32)
        qmla_ref[:, hd * MLA_SLAB:hd * MLA_SLAB + NOPE] = (q[:, :NOPE] * scale).astype(BF16)
        qmla_ref[:, hd * MLA_SLAB + NOPE:(hd + 1) * MLA_SLAB] = (
            _rope_block(q[:, NOPE:], cos, sin) * scale).astype(BF16)


def _in_proj(x, g_mix, w_in_p, g_q, g_kv, w_uq_p, b_f128, cos, sin):
    n = x.shape[0]
    tm = min(IN_PROJ_TILE, n)
    assert n % tm == 0
    row = lambda w: pl.BlockSpec((tm, w), lambda i: (i, 0))
    heads = pl.BlockSpec((tm * H_A, D_HA), lambda i: (i, 0))
    out_shapes = (
        jax.ShapeDtypeStruct((n, W_A), BF16),
        jax.ShapeDtypeStruct((n * H_A, D_HA), F32),
        jax.ShapeDtypeStruct((n * H_A, D_HA), F32),
        jax.ShapeDtypeStruct((n, W_A), BF16),
        jax.ShapeDtypeStruct((n, W_A), BF16),
        jax.ShapeDtypeStruct((n, H_A), F32),
        jax.ShapeDtypeStruct((n, H_B * MLA_SLAB), BF16),
        jax.ShapeDtypeStruct((n, KV_RANK), F32),
        jax.ShapeDtypeStruct((n, LANES), F32),
        jax.ShapeDtypeStruct((n, C_C), F32),
    )
    return pl.pallas_call(
        _in_proj_kernel,
        grid=(n // tm,),
        in_specs=[row(D_MODEL), _const_spec((1, D_MODEL)), _const_spec((D_MODEL, IN_COLS)),
                  _const_spec((1, Q_RANK)), _const_spec((1, KV_RANK)),
                  _const_spec((Q_RANK, H_B * MLA_SLAB)), _const_spec((1, LANES)), row(LANES), row(LANES)],
        out_specs=[row(W_A), heads, heads, row(W_A), row(W_A), row(H_A), row(H_B * MLA_SLAB),
                   row(KV_RANK), row(LANES), row(C_C)],
        out_shape=out_shapes,
        compiler_params=_cparams("parallel"),
        name="in_proj",
    )(x, g_mix, w_in_p, g_q, g_kv, w_uq_p, b_f128, cos, sin)


def _cumsum_kernel(x_ref, o_ref):
    c = x_ref[...]
    t = c.shape[-1]
    lane = lax.broadcasted_iota(jnp.int32, c.shape, 1)
    s = 1
    while s < t:
        c = c + jnp.where(lane >= s, pltpu.roll(c, s, 1), 0.0)
        s *= 2
    o_ref[...] = c * LOG2E


def _cumsum_time(logf_t):
    b, hh, t = logf_t.shape
    spec = pl.BlockSpec((None, hh, t), lambda i: (i, 0, 0))
    return pl.pallas_call(
        _cumsum_kernel, grid=(b,), in_specs=[spec], out_specs=spec,
        out_shape=jax.ShapeDtypeStruct((b, hh, t), F32),
        compiler_params=_cparams("parallel"), name="cumsum_logf",
    )(logf_t)


def _softmax_step(s, v, row_bias, m_ref, l_ref, acc_ref):
    width = s.shape[1]
    s_max = jnp.max(s, axis=-1, keepdims=True)
    m_old = m_ref[...]
    m_new = jnp.maximum(m_old, s_max if row_bias is None else s_max + row_bias)
    alpha = jnp.exp2(m_old - m_new)
    offset = m_new if row_bias is None else m_new - row_bias
    if width % LANES == 0:
        p = [jnp.exp2(s[:, c * LANES:(c + 1) * LANES] - offset) for c in range(width // LANES)]
        l_add = functools.reduce(lambda a, b: a + b, p)
        p = p[0].astype(BF16) if len(p) == 1 else jnp.concatenate([x.astype(BF16) for x in p], axis=1)
    else:
        p = jnp.exp2(s - offset[:, :1])
        lane = lax.broadcasted_iota(jnp.int32, m_old.shape, 1)
        l_add = jnp.where(lane == 0, jnp.sum(p, axis=-1, keepdims=True), 0.0)
        p = p.astype(BF16)
    l_ref[...] = alpha * l_ref[...] + l_add
    acc_ref[...] = alpha * acc_ref[...] + jnp.dot(p, v, preferred_element_type=F32)
    m_ref[...] = m_new


def _flash_sweep(heads, n_open, d0, tq, tk, visible, split_diagonal, m_ref, l_ref, acc_ref):
    for h in range(len(heads)):
        m_ref[h] = jnp.full(m_ref.shape[1:], NEG_INF, F32)
        l_ref[h] = jnp.zeros(l_ref.shape[1:], F32)
        acc_ref[h] = jnp.zeros(acc_ref.shape[1:], F32)

    def chunk(k0, width, masked=False, r0=0, c0=0):
        nr = tq - r0
        rows = (slice(None),) if r0 == 0 else (pl.ds(r0, nr),)
        for h, (q, key_chunk, value_chunk, col_bias, row_bias) in enumerate(heads):
            s = _qk(q[r0:], key_chunk(k0, width))
            if col_bias is not None:
                s = s + col_bias(k0, width)
            if masked:
                r = r0 + lax.broadcasted_iota(jnp.int32, (nr, width), 0)
                c = c0 + lax.broadcasted_iota(jnp.int32, (nr, width), 1)
                s = jnp.where(visible(r, c), s, NEG_INF)
            _softmax_step(s, value_chunk(k0, width), None if row_bias is None else row_bias[r0:],
                          m_ref.at[(h, *rows)], l_ref.at[(h, *rows)], acc_ref.at[(h, *rows)])

    n_wide, n_narrow = n_open

    def wide_chunk(i, carry):
        chunk(pl.multiple_of(i * tk, tk), tk)
        return carry

    def narrow_chunk(i, carry):
        chunk(pl.multiple_of(n_wide * tk + i * tq, tq), tq)
        return carry

    lax.fori_loop(0, n_wide, wide_chunk, 0)
    if not (isinstance(n_narrow, int) and n_narrow == 0):
        lax.fori_loop(0, n_narrow, narrow_chunk, 0)
    if split_diagonal:
        half = tq // 2
        chunk(d0, half, masked=True)
        chunk(pl.multiple_of(d0 + half, half), half, masked=True, r0=half, c0=half)
    else:
        chunk(d0, tq, masked=True)
    return [acc_ref[h] / jnp.sum(l_ref[h], axis=-1, keepdims=True) for h in range(len(heads))]


def _split_diagonal(tq, n_past, single_tile):
    half = tq // 2
    return not single_tile and half % LANES == 0 and half % CHUNK == 0 and n_past % CHUNK == 0


def _qk(q, k):
    return lax.dot_general(q, k, (((1,), (1,)), ((), ())), preferred_element_type=F32)


def _tile_extent(n_past, tq, tk, single_tile):
    if single_tile:
        return (n_past // tk, (n_past % tk) // tq), n_past
    start = n_past + pl.program_id(2) * tq
    n_wide = start // tk
    n_narrow = 0 if tk == tq else (start - n_wide * tk) // tq
    return (n_wide, n_narrow), pl.multiple_of(start, tq)


def _fox_kernel(q_ref, k_ref, v_ref, cq_ref, ck_ref, o_ref, m_ref, l_ref, acc_ref, *, tq, tk, n_past, single_tile):
    lane = lax.broadcasted_iota(jnp.int32, (tq, LANES), 1)
    n_open, d0 = _tile_extent(n_past, tq, tk, single_tile)
    heads = []
    for pair in range(q_ref.shape[1] // LANES):
        lanes = slice(pair * LANES, (pair + 1) * LANES)
        q = q_ref[:, lanes]
        key_chunk = functools.partial(lambda k0, width, lanes: k_ref[pl.ds(k0, width), lanes], lanes=lanes)
        value_chunk = functools.partial(lambda k0, width, lanes: v_ref[pl.ds(k0, width), lanes], lanes=lanes)
        for j in range(2):
            qj = jnp.where((lane >= j * D_HA) & (lane < (j + 1) * D_HA), q, jnp.zeros_like(q))
            col_bias = functools.partial(
                lambda k0, width, pair, j: -ck_ref[pair, j:j + 1, pl.ds(k0, width)], pair=pair, j=j)
            row_bias = jnp.broadcast_to(cq_ref[pair, :, j:j + 1], (tq, LANES))
            heads.append((qj, key_chunk, value_chunk, col_bias, row_bias))
    outs = _flash_sweep(heads, n_open, d0, tq, tk, lambda r, c: c <= r, _split_diagonal(tq, n_past, single_tile),
                        m_ref, l_ref, acc_ref)
    for pair in range(q_ref.shape[1] // LANES):
        o_ref[:, pair * LANES:(pair + 1) * LANES] = jnp.where(lane < D_HA, outs[2 * pair], outs[2 * pair + 1])


def _fox_attention(q, k, v, c_q, c_k, *, n_past, tq, tk, pairs_per_step):
    b, t_q, _ = q.shape
    t_k = k.shape[1]
    t_kp = c_k.shape[-1]
    assert t_q % tq == 0 and tk % tq == 0 and n_past % tq == 0
    npair = pairs_per_step
    kern = functools.partial(_fox_kernel, tq=tq, tk=tk, n_past=n_past, single_tile=t_q == tq)
    return pl.pallas_call(
        kern,
        grid=(b, H_A // (2 * npair), t_q // tq),
        in_specs=[pl.BlockSpec((None, tq, npair * LANES), lambda bi, hp, qi: (bi, qi, hp)),
                  pl.BlockSpec((None, t_k, npair * LANES), lambda bi, hp, qi: (bi, 0, hp)),
                  pl.BlockSpec((None, t_k, npair * LANES), lambda bi, hp, qi: (bi, 0, hp)),
                  pl.BlockSpec((None, npair, tq, 2), lambda bi, hp, qi: (bi, hp, qi, 0)),
                  pl.BlockSpec((None, npair, 2, t_kp), lambda bi, hp, qi: (bi, hp, 0, 0))],
        out_specs=pl.BlockSpec((None, tq, npair * LANES), lambda bi, hp, qi: (bi, qi, hp)),
        out_shape=jax.ShapeDtypeStruct((b, t_q, W_A), F32),
        scratch_shapes=[pltpu.VMEM((2 * npair, tq, LANES), F32)] * 3,
        compiler_params=_cparams("parallel", "parallel", "arbitrary"),
        name="fox_attention",
    )(q, k, v, c_q, c_k)


def _mla_kernel(q_ref, kn_ref, kr_ref, v_ref, o_ref, m_ref, l_ref, acc_ref, *, tq, tk, n_past, single_tile):
    n_open, d0 = _tile_extent(n_past, tq, tk, single_tile)
    heads = []
    for j in range(q_ref.shape[1] // MLA_SLAB):
        key_chunk = functools.partial(
            lambda k0, width, j: jnp.concatenate(
                [kn_ref[pl.ds(k0, width), j * NOPE:(j + 1) * NOPE], kr_ref[pl.ds(k0, width), :]], axis=1), j=j)
        value_chunk = functools.partial(lambda k0, width, j: v_ref[pl.ds(k0, width), j * V_HD:(j + 1) * V_HD], j=j)
        heads.append((q_ref[:, j * MLA_SLAB:(j + 1) * MLA_SLAB], key_chunk, value_chunk, None, None))
    visible = lambda r, c: (d0 + c) >> _CHUNK_SHIFT <= (d0 + r) >> _CHUNK_SHIFT
    outs = _flash_sweep(heads, n_open, d0, tq, tk, visible, _split_diagonal(tq, n_past, single_tile),
                        m_ref, l_ref, acc_ref)
    for j in range(q_ref.shape[1] // MLA_SLAB):
        o_ref[:, j * V_HD:(j + 1) * V_HD] = outs[j]


def _mla_attention(q, k_nope, k_rope, v, *, n_past, tq, tk, heads_per_step):
    b, t_q, _ = q.shape
    t_k = k_nope.shape[1]
    assert t_q % tq == 0 and tk % tq == 0 and n_past % tq == 0
    nh = heads_per_step
    kern = functools.partial(_mla_kernel, tq=tq, tk=tk, n_past=n_past, single_tile=t_q == tq)
    return pl.pallas_call(
        kern,
        grid=(b, H_B // nh, t_q // tq),
        in_specs=[pl.BlockSpec((None, tq, nh * MLA_SLAB), lambda bi, hd, qi: (bi, qi, hd)),
                  pl.BlockSpec((None, t_k, nh * NOPE), lambda bi, hd, qi: (bi, 0, hd)),
                  pl.BlockSpec((None, t_k, LANES), lambda bi, hd, qi: (bi, 0, 0)),
                  pl.BlockSpec((None, t_k, nh * V_HD), lambda bi, hd, qi: (bi, 0, hd))],
        out_specs=pl.BlockSpec((None, tq, nh * V_HD), lambda bi, hd, qi: (bi, qi, hd)),
        out_shape=jax.ShapeDtypeStruct((b, t_q, W_B), F32),
        scratch_shapes=[pltpu.VMEM((nh, tq, LANES), F32), pltpu.VMEM((nh, tq, LANES), F32),
                        pltpu.VMEM((nh, tq, V_HD), F32)],
        compiler_params=_cparams("parallel", "parallel", "arbitrary"),
        name="mla_attention",
    )(q, k_nope, k_rope, v)


def _kv_up_kernel(ckv_ref, kr_ref, wuk_ref, wuv_ref, kn_ref, krb_ref, v_ref):
    c = ckv_ref[...].astype(BF16)
    krb_ref[...] = kr_ref[...].astype(BF16)
    kn_ref[...] = jnp.dot(c, wuk_ref[...], preferred_element_type=F32).astype(BF16)
    v_ref[...] = jnp.dot(c, wuv_ref[...], preferred_element_type=F32).astype(BF16)


def _kv_up(ckv_n, krope128, w_uk, w_uv):
    r = ckv_n.shape[0]
    tm = next(t for t in (4 * ROW_TILE, 2 * ROW_TILE, ROW_TILE) if r % t == 0)
    row = lambda w: pl.BlockSpec((tm, w), lambda i: (i, 0))
    return pl.pallas_call(
        _kv_up_kernel, grid=(r // tm,),
        in_specs=[row(KV_RANK), row(LANES), _const_spec((KV_RANK, H_B * NOPE)),
                  _const_spec((KV_RANK, H_B * V_HD))],
        out_specs=[row(H_B * NOPE), row(LANES), row(H_B * V_HD)],
        out_shape=(jax.ShapeDtypeStruct((r, H_B * NOPE), BF16), jax.ShapeDtypeStruct((r, LANES), BF16),
                   jax.ShapeDtypeStruct((r, H_B * V_HD), BF16)),
        compiler_params=_cparams("parallel"), name="mla_kv_up",
    )(ckv_n, krope128, w_uk, w_uv)


_CONV_SUB = 64
_CONV_HALO = -(-CONV_W // SUBLANES) * SUBLANES


def _conv_kernel(xp_ref, w_ref, b_ref, g_ref, beta_ref, o_ref, *, tt):
    t0 = pl.program_id(1) * tt
    w = w_ref[...]
    rows = min(_CONV_SUB, tt)
    for sub in range(tt // rows):
        base = pl.multiple_of(t0 + sub * rows, rows)
        xa = xp_ref[pl.ds(base, rows + _CONV_HALO), :]
        acc = None
        for b in range(SUBLANES):
            z = None
            for a in range(-(-CONV_W // SUBLANES)):
                tap = SUBLANES * a + b
                if tap < CONV_W:
                    term = xa[SUBLANES * a:SUBLANES * a + rows + SUBLANES, :] * w[tap:tap + 1, :]
                    z = term if z is None else z + term
            acc = z[b:b + rows, :] if acc is None else acc + z[b:b + rows, :]
        y = acc + b_ref[...]
        mu = jnp.mean(y, axis=-1, keepdims=True)
        yc = y - mu
        var = jnp.mean(yc * yc, axis=-1, keepdims=True)
        y = yc * lax.rsqrt(var + EPS) * g_ref[...] + beta_ref[...]
        o_ref[sub * rows:(sub + 1) * rows, :] = y * jax.nn.sigmoid(y)


def _conv_module(xp, w_dw, b_dw, ln_g, ln_b, *, t, tt):
    b, t_p, _ = xp.shape
    kern = functools.partial(_conv_kernel, tt=tt)
    return pl.pallas_call(
        kern, grid=(b, t // tt),
        in_specs=[pl.BlockSpec((None, t_p, C_C), lambda bi, ti: (bi, 0, 0)),
                  _const_spec((CONV_W, C_C)), _const_spec((1, C_C)), _const_spec((1, C_C)), _const_spec((1, C_C))],
        out_specs=pl.BlockSpec((None, tt, C_C), lambda bi, ti: (bi, ti, 0)),
        out_shape=jax.ShapeDtypeStruct((b, t, C_C), F32),
        compiler_params=_cparams("parallel", "arbitrary"), name="conv_module",
    )(xp, w_dw, b_dw, ln_g, ln_b)


def _out_proj_kernel(oa_ref, ob_ref, oc_ref, x_ref, gout_ref, wout_ref, gffn_ref, wr_ref, br_ref,
                     xn_ref, h2_ref, rl_ref):
    g = gout_ref[...]
    na = _rms(oa_ref[...], g[:, :W_A]).astype(BF16)
    nb = _rms(ob_ref[...], g[:, W_A:W_A + W_B]).astype(BF16)
    nc = _rms(oc_ref[...], g[:, W_A + W_B:]).astype(BF16)
    mix = jnp.dot(na, wout_ref[:W_A, :], preferred_element_type=F32)
    mix = mix + jnp.dot(nb, wout_ref[W_A:W_A + W_B, :], preferred_element_type=F32)
    mix = mix + jnp.dot(nc, wout_ref[W_A + W_B:, :], preferred_element_type=F32)
    xn = x_ref[...] + mix
    xn_ref[...] = xn
    h2 = _rms(xn, gffn_ref[...]).astype(BF16)
    rl_ref[...] = jnp.dot(h2, wr_ref[...], preferred_element_type=F32) + br_ref[...]
    bits = pltpu.bitcast(h2.astype(F32), jnp.uint32)
    packed = (bits[:, :D_MODEL // 2] >> 16) | (bits[:, D_MODEL // 2:] & jnp.uint32(0xFFFF0000))
    tiles = D_MODEL // 2 // LANES
    for j in range(tiles):
        h2_ref[pl.ds(j, packed.shape[0], stride=tiles), :] = packed[:, j * LANES:(j + 1) * LANES]


def _out_proj(o_a, o_b, o_c, x, g_out, w_out, g_ffn, w_r, b_r):
    n = x.shape[0]
    tm = ROW_TILE
    row = lambda w: pl.BlockSpec((tm, w), lambda i: (i, 0))
    return pl.pallas_call(
        _out_proj_kernel, grid=(n // tm,),
        in_specs=[row(W_A), row(W_B), row(C_C), row(D_MODEL), _const_spec((1, D_MODEL)),
                  _const_spec((D_MODEL, D_MODEL)), _const_spec((1, D_MODEL)),
                  _const_spec((D_MODEL, LANES)), _const_spec((1, LANES))],
        out_specs=[row(D_MODEL), pl.BlockSpec((tm * (D_MODEL // 2 // LANES), LANES), lambda i: (i, 0)), row(LANES)],
        out_shape=(jax.ShapeDtypeStruct((n, D_MODEL), F32),
                   jax.ShapeDtypeStruct((n * (D_MODEL // 2 // LANES), LANES), jnp.uint32),
                   jax.ShapeDtypeStruct((n, LANES), F32)),
        compiler_params=_cparams("parallel"), name="out_proj_router",
    )(o_a, o_b, o_c, x, g_out, w_out, g_ffn, w_r, b_r)


def _expert_kernel(be_ref, nv_ref, src_ref, dst_ref, h_hbm, wg_ref, wu_ref, wd_ref, y_hbm,
                   xbuf, ybuf, zbuf, wgb_ref, wub_ref, wdb_ref, gsem, ssem, zsem, *, n_blocks, dump_row0):
    te = EXPERT_TILE
    xt = D_MODEL // 2 // LANES
    yt = D_MODEL // LANES
    i = pl.program_id(0)
    nv = nv_ref[0]
    cur = i % 2
    nxt = 1 - cur
    valid = i < nv

    def gather_copy(tok, buf, r):
        return pltpu.make_async_copy(h_hbm.at[tok], xbuf.at[buf, pl.ds(r * (xt + 1), xt), :], gsem.at[buf])

    def scatter_copy(row, buf, r):
        return pltpu.make_async_copy(ybuf.at[buf, pl.ds(r * (yt + 1), yt), :], y_hbm.at[row], ssem.at[buf])

    def gather_block(block, buf):
        for r in range(te):
            gather_copy(src_ref[block * te + r], buf, r).start(priority=1)

    def scatter_block(block, buf):
        for r in range(te):
            scatter_copy(dst_ref[block * te + r], buf, r).start(priority=r % 2)

    def wait_gather(buf):
        for r in range(te):
            gather_copy(0, buf, 0).wait()

    def wait_scatter(buf):
        for r in range(te):
            scatter_copy(0, buf, 0).wait()

    @pl.when(i == 0)
    def _():
        gather_block(0, 0)
        zbuf[...] = jnp.zeros(zbuf.shape, F32)
        fills = [pltpu.make_async_copy(zbuf, y_hbm.at[pl.ds(dump_row0 + k * SUBLANES, SUBLANES)], zsem)
                 for k in range(EXPERT_DUMP_ROWS // SUBLANES)]
        for fill in fills:
            fill.start()
        for fill in fills:
            fill.wait()

    prev_e = be_ref[jnp.maximum(i - 1, 0)]

    @pl.when(valid & ((i == 0) | (be_ref[i] != prev_e)))
    def _():
        wgb_ref[...] = wg_ref[...].astype(BF16)
        wub_ref[...] = wu_ref[...].astype(BF16)
        wdb_ref[...] = wd_ref[...].astype(BF16)

    @pl.when(valid)
    def _():
        wait_gather(cur)

    @pl.when(valid & (i >= 2))
    def _():
        wait_scatter(cur)

    def step(scatter_previous):
        gather_block(jnp.minimum(i + 1, n_blocks - 1), nxt)
        if scatter_previous:
            scatter_block(i - 1, nxt)
        u = jnp.concatenate([xbuf[cur, pl.ds(j, te, stride=xt + 1), :] for j in range(xt)], axis=1)
        x_lo = pltpu.bitcast(u << 16, F32).astype(BF16)
        x_hi = pltpu.bitcast(u & jnp.uint32(0xFFFF0000), F32).astype(BF16)
        half = D_MODEL // 2
        gate = (jnp.dot(x_lo, wgb_ref[:half, :], preferred_element_type=F32)
                + jnp.dot(x_hi, wgb_ref[half:, :], preferred_element_type=F32))
        up = (jnp.dot(x_lo, wub_ref[:half, :], preferred_element_type=F32)
              + jnp.dot(x_hi, wub_ref[half:, :], preferred_element_type=F32))
        hid = (gate * jax.nn.sigmoid(gate) * up).astype(BF16)
        y = jnp.dot(hid, wdb_ref[...], preferred_element_type=F32)
        for j in range(yt):
            ybuf[cur, pl.ds(j, te, stride=yt + 1), :] = y[:, j * LANES:(j + 1) * LANES]

    @pl.when(valid & (i == 0))
    def _():
        step(False)

    @pl.when(valid & (i > 0))
    def _():
        step(True)

    @pl.when(i == nv - 1)
    def _():
        scatter_block(i, cur)
        wait_scatter(cur)
        wait_gather(nxt)

    @pl.when((i == nv - 1) & (i > 0))
    def _():
        wait_scatter(nxt)


def _expert_blocks(block_e, n_valid, slot_src, slot_dst, h2_tiles, w_gate, w_up, w_down, *, layer):
    n_tok = h2_tiles.shape[0]
    te = EXPERT_TILE
    n_blocks = block_e.shape[0]
    xt, yt = D_MODEL // 2 // LANES, D_MODEL // LANES
    weights = lambda r, c: pl.BlockSpec((None, None, r, c), lambda i, be, nv, src, dst: (layer, be[i], 0, 0))
    grid_spec = pltpu.PrefetchScalarGridSpec(
        num_scalar_prefetch=4, grid=(n_blocks,),
        in_specs=[pl.BlockSpec(memory_space=pl.ANY), weights(D_MODEL, D_EXPERT), weights(D_MODEL, D_EXPERT),
                  weights(D_EXPERT, D_MODEL)],
        out_specs=pl.BlockSpec(memory_space=pl.ANY),
        scratch_shapes=[pltpu.VMEM((2, te * (xt + 1), LANES), jnp.uint32),
                        pltpu.VMEM((2, te * (yt + 1), LANES), F32),
                        pltpu.VMEM((SUBLANES, yt, LANES), F32),
                        pltpu.VMEM((D_MODEL, D_EXPERT), BF16), pltpu.VMEM((D_MODEL, D_EXPERT), BF16),
                        pltpu.VMEM((D_EXPERT, D_MODEL), BF16),
                        pltpu.SemaphoreType.DMA((2,)), pltpu.SemaphoreType.DMA((2,)), pltpu.SemaphoreType.DMA(())])
    kern = functools.partial(_expert_kernel, n_blocks=n_blocks, dump_row0=TOP_K * n_tok)
    return pl.pallas_call(
        kern, grid_spec=grid_spec,
        out_shape=jax.ShapeDtypeStruct((TOP_K * n_tok + EXPERT_DUMP_ROWS, yt, LANES), F32),
        compiler_params=_cparams("arbitrary"), name="expert_blocks",
    )(block_e, n_valid, slot_src, slot_dst, h2_tiles, w_gate, w_up, w_down)


def _route(rlog):
    n = rlog.shape[0]
    idx = jnp.arange(n)
    g_logits = rlog[:, :N_GROUPS]
    g_sel = jnp.argmax(g_logits, axis=-1)
    g_w = jax.nn.softmax(g_logits, axis=-1)[idx, g_sel]
    e_logits = rlog[:, N_GROUPS:N_GROUPS + N_EXPERTS].reshape(n, N_GROUPS, EXPERTS_PER_GROUP)[idx, g_sel]
    top_v, top_i = lax.top_k(e_logits, TOP_K)
    gate = jax.nn.softmax(top_v, axis=-1) * g_w[:, None]
    expert_ids = (g_sel[:, None] * EXPERTS_PER_GROUP + top_i).astype(jnp.int32)
    return expert_ids, gate


def _dispatch_plan(expert_ids):
    n = expert_ids.shape[0]
    te = EXPERT_TILE
    a = n * TOP_K
    n_blocks = -(-a // te) + N_EXPERTS
    rows = n_blocks * te
    flat_e = expert_ids.reshape(-1)
    experts = jnp.arange(N_EXPERTS, dtype=jnp.int32)
    counts = jnp.sum((flat_e[:, None] == experts[None, :]).astype(jnp.int32), axis=0)
    padded = (counts + te - 1) // te * te
    pad_end = jnp.cumsum(padded)
    pad_start = pad_end - padded
    start = jnp.cumsum(counts) - counts
    order = jnp.argsort(flat_e).astype(jnp.int32)
    slot = jnp.arange(rows, dtype=jnp.int32)
    slot_e = jnp.minimum(jnp.sum((pad_end[None, :] <= slot[:, None]).astype(jnp.int32), axis=-1), N_EXPERTS - 1)
    slot_hot = (slot_e[:, None] == experts[None, :]).astype(jnp.int32)
    slot_rank = slot - jnp.sum(slot_hot * pad_start[None, :], axis=-1)
    filled = slot_rank < jnp.sum(slot_hot * counts[None, :], axis=-1)
    src = jnp.where(filled, jnp.sum(slot_hot * start[None, :], axis=-1) + slot_rank, 0)
    slot_a = order[src]
    slot_src = jnp.where(filled, slot_a // TOP_K, 0)
    slot_dst = jnp.where(filled, (slot_a % TOP_K) * n + slot_a // TOP_K, a + slot % EXPERT_DUMP_ROWS)
    block_e = slot_e.reshape(n_blocks, te)[:, 0]
    n_valid = (pad_end[-1] // te).astype(jnp.int32).reshape(1)
    return slot_src, slot_dst, block_e, n_valid


def _combine_kernel(x_ref, y0_ref, y1_ref, g0_ref, g1_ref, gfin_ref, o_ref, *, final):
    tm = x_ref.shape[0]
    yt = D_MODEL // LANES
    g0 = g0_ref[...]
    g1 = g1_ref[...]
    x = jnp.concatenate(
        [x_ref[:, j * LANES:(j + 1) * LANES]
         + (y0_ref[pl.ds(j, tm, stride=yt), :] * g0 + y1_ref[pl.ds(j, tm, stride=yt), :] * g1) for j in range(yt)],
        axis=1)
    o_ref[...] = _rms(x, gfin_ref[...]) if final else x


def _combine(x, ys, gate, g_final, *, row0, n_tok, final):
    n = x.shape[0]
    tm = ROW_TILE
    yt = D_MODEL // LANES
    assert row0 % tm == 0 and n_tok % tm == 0
    row = pl.BlockSpec((tm, D_MODEL), lambda i: (i, 0))
    lanes = pl.BlockSpec((tm, LANES), lambda i: (i, 0))
    expert_rows = lambda k: pl.BlockSpec((tm * yt, LANES), lambda i: (i + (row0 + k * n_tok) // tm, 0))
    return pl.pallas_call(
        functools.partial(_combine_kernel, final=final), grid=(n // tm,),
        in_specs=[row, expert_rows(0), expert_rows(1), lanes, lanes, _const_spec((1, D_MODEL))],
        out_specs=row, out_shape=jax.ShapeDtypeStruct((n, D_MODEL), F32),
        compiler_params=_cparams("parallel"), name="moe_combine",
    )(x, ys, ys, *gate, g_final)


def _rope_tables(pos):
    half = ROPE // 2
    inv = ROPE_THETA ** (-jnp.arange(half, dtype=F32) / half)
    ang = pos.astype(F32)[:, None] * inv[None, :]
    zeros = jnp.zeros((pos.shape[0], LANES - ROPE), F32)
    cos = jnp.concatenate([jnp.cos(ang), jnp.cos(ang), zeros], axis=-1)
    sin = jnp.concatenate([-jnp.sin(ang), jnp.sin(ang), zeros], axis=-1)
    return cos, sin


def _prep_w_in(w_in):
    qa, ka, va, fa, cq, ckv, kr, ua, ug = jnp.split(
        w_in, np.cumsum([W_A, W_A, W_A, H_A, Q_RANK, KV_RANK, ROPE, C_C])[:].tolist(), axis=1)
    pad = jnp.zeros((D_MODEL, LANES - ROPE - H_A), w_in.dtype)
    return jnp.concatenate([qa * (D_HA ** -0.5), ka, va, cq, ckv, ua, ug, kr, fa, pad], axis=1).astype(BF16)


def _prep_w_uq(w_uq):
    w = w_uq.reshape(Q_RANK, H_B, NOPE + ROPE)
    w = jnp.pad(w, ((0, 0), (0, 0), (0, MLA_SLAB - NOPE - ROPE)))
    return w.reshape(Q_RANK, H_B * MLA_SLAB).astype(BF16)


def _prep_w_ukv(w_ukv):
    w = w_ukv.reshape(KV_RANK, H_B, NOPE + V_HD)
    return (w[:, :, :NOPE].reshape(KV_RANK, H_B * NOPE).astype(BF16),
            w[:, :, NOPE:].reshape(KV_RANK, H_B * V_HD).astype(BF16))


def _pad_lanes(x, width):
    return jnp.pad(x, ((0, 0),) * (x.ndim - 1) + ((0, width - x.shape[-1]),))


def _round_up(x, m):
    return -(-x // m) * m


def kernel(x_prompt, x_sample, cache_fox_k, cache_fox_v, cache_fox_logf, cache_mla_ckv, cache_mla_krope, state_conv, g_mix, w_in, b_f, g_q, w_uq, g_kv, w_ukv, w_dw, b_dw, ln_g, ln_b, g_out, w_out, g_ffn, w_rg, b_rg, w_re, b_re, w_gate, w_up, w_down, g_final):
    bp, tp, _ = x_prompt.shape
    bs, ts, _ = x_sample.shape
    n_past = cache_fox_k.shape[2]
    n_p, n_s = bp * tp, bs * ts
    depth = g_mix.shape[0]

    halo = CONV_W - 1
    tkp_s = _round_up(n_past + ts, LANES)

    x_p = x_prompt.reshape(n_p, D_MODEL)
    x_s = x_sample.reshape(n_s, D_MODEL)
    cos_p, sin_p = _rope_tables(jnp.arange(tp))
    cos_s, sin_s = _rope_tables(n_past + jnp.arange(ts))
    rope_p = (jnp.tile(cos_p, (bp, 1)), jnp.tile(sin_p, (bp, 1)))
    rope_s = (jnp.tile(cos_s, (bs, 1)), jnp.tile(sin_s, (bs, 1)))

    def split_heads(c):
        return c.reshape(c.shape[0], H_A // 2, 2, c.shape[-1])

    states_p, states_s = [], []
    for l in range(depth):
        w_uk, w_uv = _prep_w_ukv(w_ukv[l])
        b_f128 = jnp.pad(b_f[l], (_FA_LANE, LANES - _FA_LANE - H_A))[None, :]
        in_w = (g_mix[l][None], _prep_w_in(w_in[l]), g_q[l][None], g_kv[l][None], _prep_w_uq(w_uq[l]), b_f128)
        qa_p, ka_p, va_p, kab_p, vab_p, lf_p, qm_p, ckv_p, kr_p, u_p = _in_proj(x_p, *in_w, *rope_p)
        qa_s, ka_s, va_s, kab_s, vab_s, lf_s, qm_s, ckv_s, kr_s, u_s = _in_proj(x_s, *in_w, *rope_s)

        lf_p = lf_p.reshape(bp, tp, H_A)
        lf_s = lf_s.reshape(bs, ts, H_A)
        c_p = _cumsum_time(lf_p.transpose(0, 2, 1))
        lf_all = jnp.concatenate([cache_fox_logf[l].astype(F32), lf_s], axis=1)
        c_s = _cumsum_time(_pad_lanes(lf_all.transpose(0, 2, 1), tkp_s))
        o_a_p = _fox_attention(
            qa_p.reshape(bp, tp, W_A), kab_p.reshape(bp, tp, W_A), vab_p.reshape(bp, tp, W_A),
            split_heads(c_p).transpose(0, 1, 3, 2), split_heads(c_p), n_past=0, tq=TQ_PROMPT, tk=TK_PROMPT,
            pairs_per_step=FOX_PAIRS_PER_STEP)
        k_all = jnp.concatenate([cache_fox_k[l].reshape(bs, n_past, W_A).astype(BF16),
                                 kab_s.reshape(bs, ts, W_A)], axis=1)
        v_all = jnp.concatenate([cache_fox_v[l].reshape(bs, n_past, W_A).astype(BF16),
                                 vab_s.reshape(bs, ts, W_A)], axis=1)
        o_a_s = _fox_attention(
            qa_s.reshape(bs, ts, W_A), k_all, v_all,
            split_heads(c_s[:, :, n_past:n_past + ts]).transpose(0, 1, 3, 2), split_heads(c_s),
            n_past=n_past, tq=ts, tk=TK_SAMPLE, pairs_per_step=H_A // 2)

        ckv_s = ckv_s.reshape(bs, ts, KV_RANK)
        kr_s = kr_s.reshape(bs, ts, LANES)
        ckv_all = jnp.concatenate([cache_mla_ckv[l].astype(F32), ckv_s], axis=1)
        kr_all = jnp.concatenate([_pad_lanes(cache_mla_krope[l].astype(F32), LANES), kr_s], axis=1)
        kn_p, krb_p, vm_p = _kv_up(ckv_p, kr_p, w_uk, w_uv)
        kn_s, krb_s, vm_s = _kv_up(ckv_all.reshape(-1, KV_RANK), kr_all.reshape(-1, LANES), w_uk, w_uv)
        o_b_p = _mla_attention(
            qm_p.reshape(bp, tp, H_B * MLA_SLAB), kn_p.reshape(bp, tp, H_B * NOPE), krb_p.reshape(bp, tp, LANES),
            vm_p.reshape(bp, tp, W_B), n_past=0, tq=TQ_PROMPT, tk=TK_PROMPT, heads_per_step=MLA_HEADS_PER_STEP)
        o_b_s = _mla_attention(
            qm_s.reshape(bs, ts, H_B * MLA_SLAB), kn_s.reshape(bs, n_past + ts, H_B * NOPE),
            krb_s.reshape(bs, n_past + ts, LANES), vm_s.reshape(bs, n_past + ts, W_B),
            n_past=n_past, tq=ts, tk=TK_SAMPLE, heads_per_step=H_B)

        xp_p = jnp.pad(u_p.reshape(bp, tp, C_C), ((0, 0), (halo, _CONV_HALO - halo), (0, 0)))
        xp_s = jnp.concatenate([state_conv[l].astype(F32), u_s.reshape(bs, ts, C_C)], axis=1)
        conv_p = xp_p[:, tp:tp + halo]
        conv_s = xp_s[:, ts:ts + halo]
        xp_s = jnp.pad(xp_s, ((0, 0), (0, _CONV_HALO - halo), (0, 0)))
        conv_w = (w_dw[l], b_dw[l][None], ln_g[l][None], ln_b[l][None])
        o_c_p = _conv_module(xp_p, *conv_w, t=tp, tt=CONV_TILE)
        o_c_s = _conv_module(xp_s, *conv_w, t=ts, tt=ts)

        w_r = _pad_lanes(jnp.concatenate([w_rg[l], w_re[l]], axis=1), LANES).astype(BF16)
        b_r = _pad_lanes(jnp.concatenate([b_rg[l], b_re[l].reshape(-1)])[None, :].astype(F32), LANES)
        out_w = (g_out[l][None], w_out[l].astype(BF16), g_ffn[l][None], w_r, b_r)
        x_p, h2_p, rlog_p = _out_proj(o_a_p.reshape(n_p, W_A), o_b_p.reshape(n_p, W_B), o_c_p.reshape(n_p, C_C),
                                      x_p, *out_w)
        x_s, h2_s, rlog_s = _out_proj(o_a_s.reshape(n_s, W_A), o_b_s.reshape(n_s, W_B), o_c_s.reshape(n_s, C_C),
                                      x_s, *out_w)

        expert_ids, gate = _route(jnp.concatenate([rlog_p, rlog_s], axis=0))
        slot_src, slot_dst, block_e, n_valid = _dispatch_plan(expert_ids)
        h2_tiles = jnp.concatenate([h2_p, h2_s], axis=0).reshape(n_p + n_s, D_MODEL // 2 // LANES, LANES)
        ys = _expert_blocks(block_e, n_valid, slot_src, slot_dst, h2_tiles, w_gate, w_up, w_down, layer=l)
        final = l == depth - 1
        ys = ys.reshape(-1, LANES)
        gates = lambda rows: [jnp.broadcast_to(gate[rows, k:k + 1], (gate[rows].shape[0], LANES))
                              for k in range(TOP_K)]
        x_p = _combine(x_p, ys, gates(slice(0, n_p)), g_final[None], row0=0, n_tok=n_p + n_s, final=final)
        x_s = _combine(x_s, ys, gates(slice(n_p, None)), g_final[None], row0=n_p, n_tok=n_p + n_s, final=final)

        states_p.append((ka_p.reshape(bp, tp, H_A, D_HA), va_p.reshape(bp, tp, H_A, D_HA), lf_p,
                         ckv_p.reshape(bp, tp, KV_RANK), kr_p[:, :ROPE].reshape(bp, tp, ROPE), conv_p))
        states_s.append((ka_s.reshape(bs, ts, H_A, D_HA), va_s.reshape(bs, ts, H_A, D_HA), lf_s,
                         ckv_s, kr_s[:, :, :ROPE], conv_s))

    y_p = x_p.reshape(bp, tp, D_MODEL)
    y_s = x_s.reshape(bs, ts, D_MODEL)
    p_out = [jnp.stack(a) for a in zip(*states_p)]
    s_out = [jnp.stack(a) for a in zip(*states_s)]
    return (y_p, y_s, *p_out, *s_out)
```
